```python
import math
import jax, jax.numpy as jnp
from jax import lax
import numpy as np

D_MODEL = 1024
BATCH = 2
SEQ = 8192
DEPTH = 1

N_HEADS = 8
HEAD_DIM = 64
ATTN_WIDTH = N_HEADS * HEAD_DIM
MOBA_BLOCK = 256
MOBA_TOPK = 3
Q_CHUNK = 128
SSM_GROUP = 16
SSM_GROUPS = 32
SSM_WIDTH = SSM_GROUP * SSM_GROUPS
SSM_STATE = 64
DT_MIN = 1e-3
DT_MAX = 1e-1
D_FF = 2816
EPS = 1e-6
IN_WIDTH = 3 * ATTN_WIDTH + SSM_WIDTH + 2 * D_MODEL
SPLIT_POINTS = (ATTN_WIDTH, 2 * ATTN_WIDTH, 3 * ATTN_WIDTH, 3 * ATTN_WIDTH + SSM_WIDTH,
                3 * ATTN_WIDTH + SSM_WIDTH + D_MODEL)

kernel_name = "hybrid_moba_s5_macaron"


def rms_norm(x, g):
    xf = x.astype(jnp.float32)
    y = xf * lax.rsqrt(jnp.mean(xf * xf, axis=-1, keepdims=True) + EPS)
    return (y * g.astype(jnp.float32)).astype(x.dtype)


def swiglu(h, w_gate, w_up, w_down):
    return (jax.nn.silu(h @ w_gate) * (h @ w_up)) @ w_down


def alibi_slopes(n_heads):
    return jnp.asarray(2.0 ** (-8.0 * np.arange(1, n_heads + 1) / n_heads), dtype=jnp.float32)


def moba_attention(q, k, v):
    B, H, S, Dh = q.shape
    nb = -(-S // MOBA_BLOCK)
    pad = nb * MOBA_BLOCK - S
    qf = q.astype(jnp.float32)
    kp = jnp.pad(k.astype(jnp.float32), ((0, 0), (0, 0), (0, pad), (0, 0)))
    vp = jnp.pad(v.astype(jnp.float32), ((0, 0), (0, 0), (0, pad), (0, 0)))
    kb = kp.reshape(B, H, nb, MOBA_BLOCK, Dh)
    vb = vp.reshape(B, H, nb, MOBA_BLOCK, Dh)
    k_mean = jnp.mean(kb, axis=3)
    gate = jnp.einsum('bhsd,bhnd->bhsn', qf, k_mean)
    own_blk = jnp.arange(S) // MOBA_BLOCK
    past = jnp.arange(nb)[None, :] < own_blk[:, None]
    gate = jnp.where(past[None, None], gate, -jnp.inf)
    k_sel = min(MOBA_TOPK, nb)
    _, sel = lax.top_k(gate, k_sel)
    scale = HEAD_DIM ** -0.5
    slopes = alibi_slopes(H)
    bi = jnp.arange(B)[:, None, None, None]
    hi = jnp.arange(H)[None, :, None, None]
    n_chunks = S // Q_CHUNK

    def chunk(c):
        t0 = c * Q_CHUNK
        qc = lax.dynamic_slice_in_dim(qf, t0, Q_CHUNK, axis=2)
        selc = lax.dynamic_slice_in_dim(sel, t0, Q_CHUNK, axis=2)
        tq = t0 + jnp.arange(Q_CHUNK)
        blk = t0 // MOBA_BLOCK
        k_own = lax.dynamic_index_in_dim(kb, blk, axis=2, keepdims=False)
        v_own = lax.dynamic_index_in_dim(vb, blk, axis=2, keepdims=False)
        s_own = blk * MOBA_BLOCK + jnp.arange(MOBA_BLOCK)
        dist_own = (tq[:, None] - s_own[None, :])
        sc_own = jnp.einsum('bhqd,bhkd->bhqk', qc, k_own) * scale
        sc_own = sc_own - slopes[:, None, None] * jnp.abs(dist_own).astype(jnp.float32)
        sc_own = jnp.where((dist_own >= 0)[None, None], sc_own, -jnp.inf)
        k_g = kb[bi, hi, selc]
        v_g = vb[bi, hi, selc]
        sc_sel = jnp.einsum('bhqd,bhqnkd->bhqnk', qc, k_g) * scale
        s_pos = selc[..., None] * MOBA_BLOCK + jnp.arange(MOBA_BLOCK)
        dist_sel = jnp.abs(tq[:, None, None] - s_pos).astype(jnp.float32)
        sc_sel = sc_sel - slopes[:, None, None, None] * dist_sel
        valid = selc < blk
        sc_sel = jnp.where(valid[..., None], sc_sel, -jnp.inf)
        scores = jnp.concatenate(
            [sc_own, sc_sel.reshape(B, H, Q_CHUNK, k_sel * MOBA_BLOCK)], axis=-1)
        p = jax.nn.softmax(scores, axis=-1)
        p_own = p[..., :MOBA_BLOCK]
        p_sel = p[..., MOBA_BLOCK:].reshape(B, H, Q_CHUNK, k_sel, MOBA_BLOCK)
        return (jnp.einsum('bhqk,bhkd->bhqd', p_own, v_own)
                + jnp.einsum('bhqnk,bhqnkd->bhqd', p_sel, v_g))

    outs = lax.map(chunk, jnp.arange(n_chunks))
    out = outs.transpose(1, 2, 0, 3, 4).reshape(B, H, S, Dh)
    return out.astype(q.dtype)


def s5_ssm(u, lam_re, lam_im, log_dt, b_re, b_im, c_re, c_im, d_skip):
    Bt, S, _ = u.shape
    uf = u.astype(jnp.float32).reshape(Bt, S, SSM_GROUPS, SSM_GROUP)
    lam_re = lam_re.astype(jnp.float32)
    lam_im = lam_im.astype(jnp.float32)
    dt = jnp.exp(log_dt.astype(jnp.float32))[:, None]
    mag = jnp.exp(lam_re * dt)
    ab_r = mag * jnp.cos(lam_im * dt)
    ab_i = mag * jnp.sin(lam_im * dt)
    num_r = ab_r - 1.0
    num_i = ab_i
    den = lam_re * lam_re + lam_im * lam_im
    f_r = (num_r * lam_re + num_i * lam_im) / den
    f_i = (num_i * lam_re - num_r * lam_im) / den
    br = b_re.astype(jnp.float32)
    bim = b_im.astype(jnp.float32)
    bb_r = f_r[..., None] * br - f_i[..., None] * bim
    bb_i = f_r[..., None] * bim + f_i[..., None] * br
    xr = jnp.einsum('bsgh,gph->bsgp', uf, bb_r)
    xi = jnp.einsum('bsgh,gph->bsgp', uf, bb_i)
    a_r = jnp.broadcast_to(ab_r, xr.shape)
    a_i = jnp.broadcast_to(ab_i, xr.shape)

    def combine(e1, e2):
        a1r, a1i, b1r, b1i = e1
        a2r, a2i, b2r, b2i = e2
        return (a2r * a1r - a2i * a1i,
                a2r * a1i + a2i * a1r,
                a2r * b1r - a2i * b1i + b2r,
                a2r * b1i + a2i * b1r + b2i)

    _, _, hr, hi = lax.associative_scan(combine, (a_r, a_i, xr, xi), axis=1)
    y = (jnp.einsum('bsgp,ghp->bsgh', hr, c_re.astype(jnp.float32))
         - jnp.einsum('bsgp,ghp->bsgh', hi, c_im.astype(jnp.float32))
         + d_skip.astype(jnp.float32).reshape(SSM_GROUPS, SSM_GROUP) * uf)
    return y.reshape(Bt, S, SSM_WIDTH)


def setup_inputs(seed: int = 0) -> dict:
    key = jax.random.key(seed)
    ks = jax.random.split(key, 32)
    L = DEPTH
    f32 = jnp.float32

    def w(k, shape, fan_in):
        return jax.random.normal(k, shape, f32) * fan_in ** -0.5

    def gain(k, shape):
        return 1.0 + 0.02 * jax.random.normal(k, shape, f32)

    n = jnp.arange(SSM_STATE, dtype=f32)
    lam_re = -0.5 + 0.01 * jax.random.normal(ks[10], (L, SSM_GROUPS, SSM_STATE), f32)
    lam_im = math.pi * n + 0.01 * jax.random.normal(ks[11], (L, SSM_GROUPS, SSM_STATE), f32)
    log_dt = jax.random.uniform(ks[12], (L, SSM_GROUPS), f32, math.log(DT_MIN), math.log(DT_MAX))
    return {
        "x": jax.random.normal(ks[0], (BATCH, SEQ, D_MODEL), f32),
        "ffn1_norm": gain(ks[1], (L, D_MODEL)),
        "ffn1_w_gate": w(ks[2], (L, D_MODEL, D_FF), D_MODEL),
        "ffn1_w_up": w(ks[3], (L, D_MODEL, D_FF), D_MODEL),
        "ffn1_w_down": w(ks[4], (L, D_FF, D_MODEL), D_FF),
        "mix_norm": gain(ks[5], (L, D_MODEL)),
        "w_in": w(ks[6], (L, D_MODEL, IN_WIDTH), D_MODEL),
        "w_attn_up": w(ks[7], (L, ATTN_WIDTH, D_MODEL), ATTN_WIDTH),
        "ssm_lambda_re": lam_re,
        "ssm_lambda_im": lam_im,
        "ssm_log_dt": log_dt,
        "ssm_b_re": w(ks[13], (L, SSM_GROUPS, SSM_STATE, SSM_GROUP), 2 * SSM_GROUP),
        "ssm_b_im": w(ks[14], (L, SSM_GROUPS, SSM_STATE, SSM_GROUP), 2 * SSM_GROUP),
        "ssm_c_re": w(ks[15], (L, SSM_GROUPS, SSM_GROUP, SSM_STATE), 2 * SSM_STATE),
        "ssm_c_im": w(ks[16], (L, SSM_GROUPS, SSM_GROUP, SSM_STATE), 2 * SSM_STATE),
        "ssm_d": jax.random.normal(ks[17], (L, SSM_WIDTH), f32),
        "w_ssm_glu": w(ks[18], (L, SSM_WIDTH, 2 * D_MODEL), SSM_WIDTH),
        "w_out": w(ks[19], (L, D_MODEL, D_MODEL), D_MODEL),
        "ffn2_norm": gain(ks[20], (L, D_MODEL)),
        "ffn2_w_gate": w(ks[21], (L, D_MODEL, D_FF), D_MODEL),
        "ffn2_w_up": w(ks[22], (L, D_MODEL, D_FF), D_MODEL),
        "ffn2_w_down": w(ks[23], (L, D_FF, D_MODEL), D_FF),
        "final_norm": gain(ks[24], (D_MODEL,)),
    }


def reference(x, ffn1_norm, ffn1_w_gate, ffn1_w_up, ffn1_w_down, mix_norm, w_in, w_attn_up,
              ssm_lambda_re, ssm_lambda_im, ssm_log_dt, ssm_b_re, ssm_b_im, ssm_c_re, ssm_c_im,
              ssm_d, w_ssm_glu, w_out, ffn2_norm, ffn2_w_gate, ffn2_w_up, ffn2_w_down,
              final_norm):
    B, S, _ = x.shape
    for l in range(DEPTH):
        h = rms_norm(x, ffn1_norm[l])
        x = x + 0.5 * swiglu(h, ffn1_w_gate[l], ffn1_w_up[l], ffn1_w_down[l])
        h = rms_norm(x, mix_norm[l])
        proj = h @ w_in[l]
        q, k, v, u, g_attn, g_ssm = jnp.split(proj, SPLIT_POINTS, axis=-1)
        q = q.reshape(B, S, N_HEADS, HEAD_DIM).transpose(0, 2, 1, 3)
        k = k.reshape(B, S, N_HEADS, HEAD_DIM).transpose(0, 2, 1, 3)
        v = v.reshape(B, S, N_HEADS, HEAD_DIM).transpose(0, 2, 1, 3)
        attn = moba_attention(q, k, v).transpose(0, 2, 1, 3).reshape(B, S, ATTN_WIDTH)
        attn = attn @ w_attn_up[l]
        y = s5_ssm(u, ssm_lambda_re[l], ssm_lambda_im[l], ssm_log_dt[l], ssm_b_re[l],
                   ssm_b_im[l], ssm_c_re[l], ssm_c_im[l], ssm_d[l]).astype(x.dtype)
        y_val, y_gate = jnp.split(jax.nn.gelu(y) @ w_ssm_glu[l], 2, axis=-1)
        ssm_out = y_val * jax.nn.sigmoid(y_gate)
        merged = jax.nn.sigmoid(g_attn) * attn + jax.nn.sigmoid(g_ssm) * ssm_out
        x = x + merged @ w_out[l]
        h = rms_norm(x, ffn2_norm[l])
        x = x + 0.5 * swiglu(h, ffn2_w_gate[l], ffn2_w_up[l], ffn2_w_down[l])
    return rms_norm(x, final_norm)
```

```python
import functools
import math

import numpy as np
import jax
import jax.numpy as jnp
from jax import lax
from jax.experimental import pallas as pl
from jax.experimental.pallas import tpu as pltpu

F32 = jnp.float32
BF16 = jnp.bfloat16
HIGHEST = lax.Precision.HIGHEST

D_MODEL = 1024
N_HEADS = 8
HEAD_DIM = 64
ATTN_WIDTH = N_HEADS * HEAD_DIM
MOBA_BLOCK = 256
MOBA_TOPK = 3
SSM_GROUP = 16
SSM_GROUPS = 32
SSM_WIDTH = SSM_GROUP * SSM_GROUPS
SSM_STATE = 64
D_FF = 2816
EPS = 1e-6

VMEM_LIMIT_BYTES = 56 * 1024 * 1024
TOKEN_TILE = 256
SSM_CHUNK = 16
SSM_PAIR = 2
SCAN_LANES = 512


def _const_spec(shape):
    nd = len(shape)
    return pl.BlockSpec(shape, lambda *_: (0,) * nd, pipeline_mode=pl.Buffered(1))


def _rms(x, g):
    return x * lax.rsqrt(jnp.mean(x * x, axis=-1, keepdims=True) + EPS) * g


def _swiglu_half_step(x, g_norm, wg_ref, wu_ref, wd_ref):
    h = _rms(x, g_norm).astype(BF16)
    gate = jnp.dot(h, wg_ref[...], preferred_element_type=F32)
    up = jnp.dot(h, wu_ref[...], preferred_element_type=F32)
    act = (gate * jax.nn.sigmoid(gate) * up).astype(BF16)
    return x + 0.5 * jnp.dot(act, wd_ref[...], preferred_element_type=F32)


def _ffn1_proj_kernel(x_ref, n1_ref, wg_ref, wu_ref, wd_ref, n2_ref, win_ref,
                      x1_ref, q_ref, k_ref, v_ref, u_ref, ga_ref, gs_ref):
    x1 = _swiglu_half_step(x_ref[...], n1_ref[...], wg_ref, wu_ref, wd_ref)
    x1_ref[...] = x1
    h = _rms(x1, n2_ref[...]).astype(BF16)
    col = 0
    for ref in (q_ref, k_ref, v_ref, u_ref, ga_ref, gs_ref):
        width = ref.shape[-1]
        ref[...] = jnp.dot(h, win_ref[:, col:col + width],
                           preferred_element_type=F32).astype(ref.dtype)
        col += width


def _ffn1_proj(x, n1, wg, wu, wd, n2, win):
    tokens = x.shape[0]
    tm = TOKEN_TILE
    row = lambda w: pl.BlockSpec((tm, w), lambda i: (i, 0))
    widths = (ATTN_WIDTH, ATTN_WIDTH, ATTN_WIDTH, SSM_WIDTH, D_MODEL, D_MODEL)
    dtypes = (F32, F32, BF16, F32, F32, F32)
    return pl.pallas_call(
        _ffn1_proj_kernel,
        grid=(tokens // tm,),
        in_specs=[row(D_MODEL), _const_spec(n1.shape), _const_spec(wg.shape),
                  _const_spec(wu.shape), _const_spec(wd.shape), _const_spec(n2.shape),
                  _const_spec(win.shape)],
        out_specs=[row(D_MODEL)] + [row(w) for w in widths],
        out_shape=[jax.ShapeDtypeStruct((tokens, D_MODEL), F32)]
        + [jax.ShapeDtypeStruct((tokens, w), dt) for w, dt in zip(widths, dtypes)],
        compiler_params=pltpu.CompilerParams(
            dimension_semantics=("arbitrary",), vmem_limit_bytes=VMEM_LIMIT_BYTES),
        name="ffn1_proj",
    )(x, n1, wg, wu, wd, n2, win)


def _moba_kernel(slopes_ref, qt_ref, k_ref, vt_ref, o_ref, kmean_ref, mask_ref):
    h = pl.program_id(1)
    i = pl.program_id(2)
    blk = MOBA_BLOCK
    n_blocks = k_ref.shape[0]
    neg_inf = -jnp.inf

    @pl.when(i == 0)
    def _():
        kmean_ref[...] = jnp.mean(k_ref[...], axis=1)

    slope = slopes_ref[h]
    qt = qt_ref[...]

    gate = jnp.dot(kmean_ref[...], qt, precision=HIGHEST, preferred_element_type=F32)
    blk_id = lax.broadcasted_iota(jnp.int32, gate.shape, 0).astype(F32)
    g = jnp.where(blk_id < i.astype(F32), gate, neg_inf)
    sel = jnp.zeros(gate.shape, jnp.bool_)
    for _ in range(MOBA_TOPK):
        top = jnp.max(g, axis=0, keepdims=True)
        first = jnp.min(jnp.where(g == top, blk_id, float(n_blocks)), axis=0, keepdims=True)
        pick = (blk_id == first) & (top > neg_inf)
        sel = sel | pick
        g = jnp.where(pick, neg_inf, g)
    mask_ref[...] = jnp.where(sel, 0.0, neg_inf).astype(F32)

    qs = (qt * (HEAD_DIM ** -0.5)).astype(BF16)
    s_io = lax.broadcasted_iota(jnp.int32, (blk, blk), 0)
    t_io = lax.broadcasted_iota(jnp.int32, (blk, blk), 1)
    bias_rel = slope * (s_io - t_io).astype(F32)

    def scores(j):
        kj = k_ref[j].astype(BF16)
        return jnp.dot(kj, qs, preferred_element_type=F32) + bias_rel

    def values(j):
        return vt_ref[j]

    s = jnp.where(s_io <= t_io, scores(i), neg_inf)
    m = jnp.max(s, axis=0, keepdims=True)
    p = jnp.exp(s - m)
    l = jnp.sum(p, axis=0, keepdims=True)
    acc = jnp.dot(values(i), p.astype(BF16), preferred_element_type=F32)

    def body(j, carry):
        m, l, acc = carry
        row = mask_ref[pl.ds(j, 1), :] + slope * ((j - i) * blk).astype(F32)
        s = scores(j) + row
        m_new = jnp.maximum(m, jnp.max(s, axis=0, keepdims=True))
        alpha = jnp.exp(m - m_new)
        p = jnp.exp(s - m_new)
        l = alpha * l + jnp.sum(p, axis=0, keepdims=True)
        acc = alpha * acc + jnp.dot(values(j), p.astype(BF16), preferred_element_type=F32)
        return m_new, l, acc

    m, l, acc = lax.fori_loop(0, i, body, (m, l, acc))
    o_ref[...] = acc / l


def _moba_attention(qt, k, vt, slopes):
    B, H, Dh, S = qt.shape
    blk = MOBA_BLOCK
    n_blocks = S // blk
    return pl.pallas_call(
        _moba_kernel,
        grid=(B, H, n_blocks),
        in_specs=[
            pl.BlockSpec(memory_space=pltpu.SMEM),
            pl.BlockSpec((None, None, Dh, blk), lambda b, h, i: (b, h, 0, i)),
            pl.BlockSpec((None, None, n_blocks, blk, Dh), lambda b, h, i: (b, h, 0, 0, 0)),
            pl.BlockSpec((None, None, n_blocks, Dh, blk), lambda b, h, i: (b, h, 0, 0, 0)),
        ],
        out_specs=pl.BlockSpec((None, None, Dh, blk), lambda b, h, i: (b, h, 0, i)),
        out_shape=jax.ShapeDtypeStruct((B, H, Dh, S), F32),
        scratch_shapes=[pltpu.VMEM((n_blocks, Dh), F32), pltpu.VMEM((n_blocks, blk), F32)],
        compiler_params=pltpu.CompilerParams(
            dimension_semantics=("arbitrary", "arbitrary", "arbitrary"),
            vmem_limit_bytes=VMEM_LIMIT_BYTES),
        name="moba_attn",
    )(slopes, qt, k, vt)


def _ssm_matrices(lam_re, lam_im, log_dt, b_re, b_im, c_re, c_im, d_skip):
    T, G, P, Hc = SSM_CHUNK, SSM_GROUPS, SSM_STATE, SSM_GROUP
    dt = jnp.exp(log_dt)[:, None]
    mag = jnp.exp(lam_re * dt)
    ar = mag * jnp.cos(lam_im * dt)
    ai = mag * jnp.sin(lam_im * dt)
    nr, ni = ar - 1.0, ai
    den = lam_re * lam_re + lam_im * lam_im
    fr = (nr * lam_re + ni * lam_im) / den
    fi = (ni * lam_re - nr * lam_im) / den
    bbr = fr[..., None] * b_re - fi[..., None] * b_im
    bbi = fr[..., None] * b_im + fi[..., None] * b_re
    pr, pi = [jnp.ones_like(ar)], [jnp.zeros_like(ar)]
    for _ in range(T):
        pr, pi = pr + [pr[-1] * ar - pi[-1] * ai], pi + [pr[-1] * ai + pi[-1] * ar]
    pw_r, pw_i = jnp.stack(pr), jnp.stack(pi)

    ca_r = c_re[None] * pw_r[:, :, None, :] - c_im[None] * pw_i[:, :, None, :]
    ca_i = c_re[None] * pw_i[:, :, None, :] + c_im[None] * pw_r[:, :, None, :]
    kern = (jnp.einsum('tgop,gpi->tgoi', ca_r[:T], bbr, precision=HIGHEST)
            - jnp.einsum('tgop,gpi->tgoi', ca_i[:T], bbi, precision=HIGHEST))
    kern = jnp.concatenate([kern, jnp.zeros_like(kern[:1])], axis=0)
    s_idx = np.arange(T)[:, None]
    t_idx = np.arange(T)[None, :]
    tau = np.where(t_idx >= s_idx, t_idx - s_idx, T)
    w = kern[tau]
    skip = d_skip.reshape(G, Hc)[:, :, None] * jnp.eye(Hc, dtype=F32)[None]
    w = w + jnp.eye(T, dtype=F32)[:, :, None, None, None] * skip[None, None]
    w = w.transpose(2, 0, 4, 1, 3).reshape(G, T * Hc, T * Hc)

    rev_r, rev_i = pw_r[T - 1::-1][:T], pw_i[T - 1::-1][:T]
    n_r = rev_r[..., None] * bbr[None] - rev_i[..., None] * bbi[None]
    n_i = rev_r[..., None] * bbi[None] + rev_i[..., None] * bbr[None]
    n_r = n_r.transpose(1, 0, 3, 2).reshape(G, T * Hc, P)
    n_i = n_i.transpose(1, 0, 3, 2).reshape(G, T * Hc, P)

    m_r = ca_r[1:].transpose(1, 3, 0, 2).reshape(G, P, T * Hc)
    m_i = (-ca_i[1:]).transpose(1, 3, 0, 2).reshape(G, P, T * Hc)

    Q = G // SSM_PAIR

    def pair_diag(a):
        a = a.reshape(Q, SSM_PAIR, *a.shape[1:])
        z = jnp.zeros_like(a[:, 0])
        return jnp.concatenate([jnp.concatenate([a[:, 0], z], axis=2),
                                jnp.concatenate([z, a[:, 1]], axis=2)], axis=1)

    w_pair = pair_diag(w)
    n_pair = jnp.concatenate([pair_diag(n_r), pair_diag(n_i)], axis=2)
    m_pair = jnp.concatenate([pair_diag(m_r), pair_diag(m_i)], axis=1)
    a_chunk = jnp.stack([pw_r[T].reshape(1, G * P), pw_i[T].reshape(1, G * P)])
    return w_pair, n_pair, m_pair, a_chunk


def _ssm_state_in_kernel(u_ref, n_ref, er_ref, ei_ref):
    e = jnp.dot(u_ref[...], n_ref[...], precision=HIGHEST, preferred_element_type=F32)
    half = er_ref.shape[-1]
    er_ref[...] = e[:, :half]
    ei_ref[...] = e[:, half:]


def _ssm_scan_kernel(a_ref, er_ref, ei_ref, hr_ref, hi_ref, *, n_seq):
    ar = a_ref[0]
    ai = a_ref[1]
    steps = er_ref.shape[0] // n_seq
    zero = jnp.zeros_like(ar)

    def body(c, carry):
        out = []
        for b in range(n_seq):
            hr, hi = carry[2 * b], carry[2 * b + 1]
            r = b * steps + c
            hr_ref[pl.ds(r, 1), :] = hr
            hi_ref[pl.ds(r, 1), :] = hi
            er = er_ref[pl.ds(r, 1), :]
            ei = ei_ref[pl.ds(r, 1), :]
            out += [ar * hr - ai * hi + er, ar * hi + ai * hr + ei]
        return tuple(out)

    lax.fori_loop(0, steps, body, (zero,) * (2 * n_seq))


def _ssm_out_kernel(u_ref, w_ref, hr_ref, hi_ref, m_ref, y_ref):
    half = hr_ref.shape[-1]
    y = jnp.dot(u_ref[...], w_ref[...], precision=HIGHEST, preferred_element_type=F32)
    y += jnp.dot(hr_ref[...], m_ref[:half, :], precision=HIGHEST, preferred_element_type=F32)
    y += jnp.dot(hi_ref[...], m_ref[half:, :], precision=HIGHEST, preferred_element_type=F32)
    y_ref[...] = y


def _s5_ssm(u, n_batch, mats):
    w_pair, n_pair, m_pair, a_chunk = mats
    tokens = u.shape[0]
    T, Hc, P = SSM_CHUNK, SSM_GROUP, SSM_STATE
    Q = SSM_GROUPS // SSM_PAIR
    rows = tokens // T
    pair_in = SSM_PAIR * T * Hc
    pair_st = SSM_PAIR * P
    n_state = SSM_GROUPS * P
    u2 = u.reshape(rows, T, Q, SSM_PAIR, Hc).transpose(2, 0, 3, 1, 4).reshape(Q, rows, pair_in)

    cparams = pltpu.CompilerParams(dimension_semantics=("arbitrary",),
                                   vmem_limit_bytes=VMEM_LIMIT_BYTES)
    er, ei = pl.pallas_call(
        _ssm_state_in_kernel,
        grid=(Q,),
        in_specs=[pl.BlockSpec((None, rows, pair_in), lambda q: (q, 0, 0)),
                  pl.BlockSpec((None, pair_in, 2 * pair_st), lambda q: (q, 0, 0))],
        out_specs=[pl.BlockSpec((rows, pair_st), lambda q: (0, q))] * 2,
        out_shape=[jax.ShapeDtypeStruct((rows, n_state), F32)] * 2,
        compiler_params=cparams, name="ssm_state_in",
    )(u2, n_pair)

    lane_spec = pl.BlockSpec((rows, SCAN_LANES), lambda q: (0, q))
    hr, hi = pl.pallas_call(
        functools.partial(_ssm_scan_kernel, n_seq=n_batch),
        grid=(n_state // SCAN_LANES,),
        in_specs=[pl.BlockSpec((2, 1, SCAN_LANES), lambda q: (0, 0, q)), lane_spec, lane_spec],
        out_specs=[lane_spec, lane_spec],
        out_shape=[jax.ShapeDtypeStruct((rows, n_state), F32)] * 2,
        compiler_params=cparams, name="ssm_scan",
    )(a_chunk, er, ei)

    st_spec = pl.BlockSpec((rows, pair_st), lambda q: (0, q))
    y2 = pl.pallas_call(
        _ssm_out_kernel,
        grid=(Q,),
        in_specs=[pl.BlockSpec((None, rows, pair_in), lambda q: (q, 0, 0)),
                  pl.BlockSpec((None, pair_in, pair_in), lambda q: (q, 0, 0)),
                  st_spec, st_spec,
                  pl.BlockSpec((None, 2 * pair_st, pair_in), lambda q: (q, 0, 0))],
        out_specs=pl.BlockSpec((None, rows, pair_in), lambda q: (q, 0, 0)),
        out_shape=jax.ShapeDtypeStruct((Q, rows, pair_in), F32),
        compiler_params=cparams, name="ssm_out",
    )(u2, w_pair, hr, hi, m_pair)
    return (y2.reshape(Q, rows, SSM_PAIR, T, Hc).transpose(1, 3, 0, 2, 4)
            .reshape(tokens, SSM_WIDTH))


def _post_kernel(x1_ref, attn_ref, y_ref, ga_ref, gs_ref, wup_ref, wglu_ref, wout_ref,
                 n3_ref, wg_ref, wu_ref, wd_ref, nf_ref, o_ref):
    attn = jnp.dot(attn_ref[...], wup_ref[...], preferred_element_type=F32)
    y = jax.nn.gelu(y_ref[...], approximate=True).astype(BF16)
    glu = jnp.dot(y, wglu_ref[...], preferred_element_type=F32)
    ssm_out = glu[:, :D_MODEL] * jax.nn.sigmoid(glu[:, D_MODEL:])
    merged = jax.nn.sigmoid(ga_ref[...]) * attn + jax.nn.sigmoid(gs_ref[...]) * ssm_out
    x2 = x1_ref[...] + jnp.dot(merged.astype(BF16), wout_ref[...], preferred_element_type=F32)
    x3 = _swiglu_half_step(x2, n3_ref[...], wg_ref, wu_ref, wd_ref)
    o_ref[...] = _rms(x3, nf_ref[...])


def _post(x1, attn, y, ga, gs, wup, wglu, wout, n3, wg, wu, wd, nf):
    tokens = x1.shape[0]
    tm = TOKEN_TILE
    row = lambda w: pl.BlockSpec((tm, w), lambda i: (i, 0))
    consts = (wup, wglu, wout, n3, wg, wu, wd, nf)
    return pl.pallas_call(
        _post_kernel,
        grid=(tokens // tm,),
        in_specs=[row(D_MODEL), row(ATTN_WIDTH), row(SSM_WIDTH), row(D_MODEL), row(D_MODEL)]
        + [_const_spec(c.shape) for c in consts],
        out_specs=row(D_MODEL),
        out_shape=jax.ShapeDtypeStruct((tokens, D_MODEL), F32),
        compiler_params=pltpu.CompilerParams(
            dimension_semantics=("arbitrary",), vmem_limit_bytes=VMEM_LIMIT_BYTES),
        name="post",
    )(x1, attn, y, ga, gs, *consts)


def kernel(x, ffn1_norm, ffn1_w_gate, ffn1_w_up, ffn1_w_down, mix_norm, w_in, w_attn_up, ssm_lambda_re, ssm_lambda_im, ssm_log_dt, ssm_b_re, ssm_b_im, ssm_c_re, ssm_c_im, ssm_d, w_ssm_glu, w_out, ffn2_norm, ffn2_w_gate, ffn2_w_up, ffn2_w_down, final_norm):
    B, S, D = x.shape
    depth = ffn1_norm.shape[0]
    assert depth == 1, "final norm is fused into the single layer's last stage"
    tokens = B * S
    slopes = jnp.asarray(2.0 ** (-8.0 * np.arange(1, N_HEADS + 1) / N_HEADS), dtype=F32)
    bf = lambda w: w.astype(BF16)
    xt = x.reshape(tokens, D)
    for l in range(depth):
        x1, q, k, v, u, ga, gs = _ffn1_proj(
            xt, ffn1_norm[l][None], bf(ffn1_w_gate[l]), bf(ffn1_w_up[l]), bf(ffn1_w_down[l]),
            mix_norm[l][None], bf(w_in[l]))
        nb = S // MOBA_BLOCK
        blocks = lambda a: a.reshape(B, nb, MOBA_BLOCK, N_HEADS, HEAD_DIM)
        qt = q.reshape(B, S, N_HEADS, HEAD_DIM).transpose(0, 2, 3, 1)
        kh = blocks(k).transpose(0, 3, 1, 2, 4)
        vt = blocks(v).transpose(0, 3, 1, 4, 2)
        ot = _moba_attention(qt, kh, vt, slopes)
        attn = ot.transpose(0, 3, 1, 2).reshape(tokens, ATTN_WIDTH).astype(BF16)
        mats = _ssm_matrices(ssm_lambda_re[l], ssm_lambda_im[l], ssm_log_dt[l], ssm_b_re[l],
                             ssm_b_im[l], ssm_c_re[l], ssm_c_im[l], ssm_d[l])
        y = _s5_ssm(u, B, mats)
        xt = _post(x1, attn, y, ga, gs, bf(w_attn_up[l]), bf(w_ssm_glu[l]), bf(w_out[l]),
                   ffn2_norm[l][None], bf(ffn2_w_gate[l]), bf(ffn2_w_up[l]), bf(ffn2_w_down[l]),
                   final_norm[None])
    return xt.reshape(B, S, D)
```

```python
import functools
import math

import numpy as np
import jax
import jax.numpy as jnp
from jax import lax
from jax.experimental import pallas as pl
from jax.experimental.pallas import tpu as pltpu

F32 = jnp.float32
BF16 = jnp.bfloat16
HIGHEST = lax.Precision.HIGHEST

D_MODEL = 1024
N_HEADS = 8
HEAD_DIM = 64
ATTN_WIDTH = N_HEADS * HEAD_DIM
MOBA_BLOCK = 256
MOBA_TOPK = 3
SSM_GROUP = 16
SSM_GROUPS = 32
SSM_WIDTH = SSM_GROUP * SSM_GROUPS
SSM_STATE = 64
D_FF = 2816
EPS = 1e-6
LOG2_E = math.log2(math.e)

VMEM_LIMIT_BYTES = 56 * 1024 * 1024
TOKEN_TILE = MOBA_BLOCK
HEADS_PER_STEP = 4
SOFTMAX_ROWS = 32
SSM_CHUNK = 16
SSM_PAIR = 2
SCAN_LANES = 512


def _const_spec(shape):
    nd = len(shape)
    return pl.BlockSpec(shape, lambda *_: (0,) * nd, pipeline_mode=pl.Buffered(1))


def _rms(x, g):
    return x * lax.rsqrt(jnp.mean(x * x, axis=-1, keepdims=True) + EPS) * g


def _swiglu_half_step(x, g_norm, wg_ref, wu_ref, wd_ref):
    h = _rms(x, g_norm).astype(BF16)
    gate = jnp.dot(h, wg_ref[...], preferred_element_type=F32)
    up = jnp.dot(h, wu_ref[...], preferred_element_type=F32)
    act = (gate * jax.nn.sigmoid(gate) * up).astype(BF16)
    return x + 0.5 * jnp.dot(act, wd_ref[...], preferred_element_type=F32)


def _ffn1_proj_kernel(x_ref, n1_ref, wg_ref, wu_ref, wd_ref, n2_ref, win_ref,
                      x1_ref, q_ref, k_ref, v_ref, u_ref, ga_ref, gs_ref):
    x1 = _swiglu_half_step(x_ref[...], n1_ref[...], wg_ref, wu_ref, wd_ref)
    x1_ref[...] = x1
    h = _rms(x1, n2_ref[...]).astype(BF16)
    col = 0
    for ref in (q_ref, k_ref, v_ref, u_ref, ga_ref, gs_ref):
        width = ref.shape[-1]
        ref[...] = jnp.dot(h, win_ref[:, col:col + width],
                           preferred_element_type=F32).astype(ref.dtype)
        col += width


def _ffn1_proj(x, n1, wg, wu, wd, n2, win):
    tokens = x.shape[0]
    tm = TOKEN_TILE
    row = lambda w: pl.BlockSpec((tm, w), lambda i: (i, 0))
    widths = (ATTN_WIDTH, ATTN_WIDTH, ATTN_WIDTH, SSM_WIDTH, D_MODEL, D_MODEL)
    dtypes = (F32, F32, BF16, F32, F32, F32)
    return pl.pallas_call(
        _ffn1_proj_kernel,
        grid=(tokens // tm,),
        in_specs=[row(D_MODEL), _const_spec(n1.shape), _const_spec(wg.shape),
                  _const_spec(wu.shape), _const_spec(wd.shape), _const_spec(n2.shape),
                  _const_spec(win.shape)],
        out_specs=[row(D_MODEL)] + [row(w) for w in widths],
        out_shape=[jax.ShapeDtypeStruct((tokens, D_MODEL), F32)]
        + [jax.ShapeDtypeStruct((tokens, w), dt) for w, dt in zip(widths, dtypes)],
        compiler_params=pltpu.CompilerParams(
            dimension_semantics=("arbitrary",), vmem_limit_bytes=VMEM_LIMIT_BYTES),
        name="ffn1_proj",
    )(x, n1, wg, wu, wd, n2, win)


def _moba_kernel(slopes_ref, q_ref, k_ref, v_ref, o_ref, kmean_ref, mask_ref, bias_ref, vt_ref,
                 t_a, t_b, p_a, p_b):
    hg = pl.program_id(1)
    i = pl.program_id(2)
    blk = MOBA_BLOCK
    n_blocks = k_ref.shape[0]
    n_heads = HEADS_PER_STEP
    neg_inf = -jnp.inf
    s_io = lax.broadcasted_iota(jnp.int32, (blk, blk), 0)
    slopes = [slopes_ref[hg * n_heads + hh] * LOG2_E for hh in range(n_heads)]
    rows = lambda hh: slice(hh * HEAD_DIM, (hh + 1) * HEAD_DIM)

    @pl.when(i == 0)
    def _():
        kmean_ref[...] = jnp.mean(k_ref[...], axis=1)
        for hh in range(n_heads):
            bias_ref[hh] = slopes[hh] * s_io.astype(F32)

        def transpose_values(j, _):
            vt_ref[j] = v_ref[j].astype(F32).T.astype(BF16)
            return _
        lax.fori_loop(0, n_blocks, transpose_values, 0)

    qt = q_ref[...].T
    row_head = lax.broadcasted_iota(jnp.int32, qt.shape, 0) // HEAD_DIM
    blk_id = lax.broadcasted_iota(jnp.int32, (n_blocks, blk), 0).astype(F32)
    past = blk_id < i.astype(F32)

    qs = []
    for hh in range(n_heads):
        qm = jnp.where(row_head == hh, qt, 0.0)
        gate = jnp.dot(kmean_ref[...], qm, precision=HIGHEST, preferred_element_type=F32)
        g = jnp.where(past, gate, neg_inf)
        sel = jnp.zeros(g.shape, jnp.bool_)
        for _ in range(MOBA_TOPK):
            top = jnp.max(g, axis=0, keepdims=True)
            first = jnp.min(jnp.where(g == top, blk_id, float(n_blocks)), axis=0, keepdims=True)
            pick = (blk_id == first) & (top > neg_inf)
            sel = sel | pick
            g = jnp.where(pick, neg_inf, g)
        mask_ref[hh] = jnp.where(sel, 0.0, neg_inf).astype(F32)
        qs.append((qm * (HEAD_DIM ** -0.5 * LOG2_E)).astype(BF16))

    last_block = n_blocks - 1
    chunks = [slice(c, c + SOFTMAX_ROWS) for c in range(0, blk, SOFTMAX_ROWS)]

    def qk_into(t_ref, j):
        kj = k_ref[j].astype(BF16)
        for hh in range(n_heads):
            t_ref[hh] = jnp.dot(kj, qs[hh], preferred_element_type=F32) + bias_ref[hh]

    def pv(p_ref, j, alpha, acc):
        vj = vt_ref[j]
        return tuple(alpha[hh] * acc[hh] + jnp.dot(vj[rows(hh)], p_ref[hh],
                                                   preferred_element_type=F32)
                     for hh in range(n_heads))

    def softmax_into(t_ref, p_ref, hh, row, m, l, causal):
        def logits(c):
            t = t_ref[hh, c, :]
            if not causal:
                return t
            key = lax.broadcasted_iota(jnp.int32, t.shape, 0) + c.start
            query = lax.broadcasted_iota(jnp.int32, t.shape, 1)
            return jnp.where(key <= query, t, neg_inf)

        cmax = logits(chunks[0])
        for c in chunks[1:]:
            cmax = jnp.maximum(cmax, logits(c))
        cmax = jnp.max(cmax, axis=0, keepdims=True) + row
        m_new = cmax if m is None else jnp.maximum(m, cmax)
        shift = row - m_new
        psum = None
        for c in chunks:
            p = jnp.exp2(logits(c) + shift)
            p_ref[hh, c, :] = p.astype(BF16)
            psum = p if psum is None else psum + p
        psum = jnp.sum(psum, axis=0, keepdims=True)
        if m is None:
            return jnp.ones_like(psum), m_new, psum
        alpha = jnp.exp2(m - m_new)
        return alpha, m_new, alpha * l + psum

    def block_row(hh, j):
        return mask_ref[hh, pl.ds(j, 1), :] + slopes[hh] * ((j - i) * blk).astype(F32)

    qk_into(t_b, i)
    qk_into(t_a, 0)
    zero_row = jnp.zeros((1, blk), F32)
    own = [softmax_into(t_b, p_b, hh, zero_row, None, None, True) for hh in range(n_heads)]
    alpha0, m0, l0 = (tuple(x) for x in zip(*own))
    acc0 = (jnp.zeros((HEAD_DIM, blk), F32),) * n_heads

    def body(n, carry):
        prev_j, a_prev, m, l, acc = carry
        j0 = 2 * n
        j1 = jnp.minimum(j0 + 1, last_block)
        qk_into(t_b, j1)
        acc = pv(p_b, prev_j, a_prev, acc)
        st0 = [softmax_into(t_a, p_a, hh, block_row(hh, j0), m[hh], l[hh], False)
               for hh in range(n_heads)]
        a0, m, l = (tuple(x) for x in zip(*st0))
        qk_into(t_a, jnp.minimum(j0 + 2, last_block))
        acc = pv(p_a, j0, a0, acc)
        st1 = [softmax_into(t_b, p_b, hh, block_row(hh, j1), m[hh], l[hh], False)
               for hh in range(n_heads)]
        a1, m, l = (tuple(x) for x in zip(*st1))
        return j1, a1, m, l, acc

    prev_j, a_prev, _, l, acc = lax.fori_loop(
        0, (i + 1) // 2, body, (i, alpha0, m0, l0, acc0))
    acc = pv(p_b, prev_j, a_prev, acc)
    out_t = jnp.concatenate([acc[hh] / l[hh] for hh in range(n_heads)], axis=0)
    o_ref[...] = out_t.T.astype(o_ref.dtype)


def _moba_attention(q, k, v, slopes, n_batch):
    tokens, width = q.shape
    blk = MOBA_BLOCK
    n_blocks = tokens // n_batch // blk
    step_width = HEADS_PER_STEP * HEAD_DIM
    q_spec = pl.BlockSpec((blk, step_width), lambda b, g, i: (b * n_blocks + i, g))
    kv_spec = pl.BlockSpec((n_blocks, blk, step_width), lambda b, g, i: (b, 0, g))
    by_block = lambda a: a.reshape(tokens // blk, blk, width)
    return pl.pallas_call(
        _moba_kernel,
        grid=(n_batch, width // step_width, n_blocks),
        in_specs=[pl.BlockSpec(memory_space=pltpu.SMEM), q_spec, kv_spec, kv_spec],
        out_specs=q_spec,
        out_shape=jax.ShapeDtypeStruct((tokens, width), BF16),
        scratch_shapes=[pltpu.VMEM((n_blocks, step_width), F32),
                        pltpu.VMEM((HEADS_PER_STEP, n_blocks, blk), F32),
                        pltpu.VMEM((HEADS_PER_STEP, blk, blk), F32),
                        pltpu.VMEM((n_blocks, step_width, blk), BF16),
                        pltpu.VMEM((HEADS_PER_STEP, blk, blk), F32),
                        pltpu.VMEM((HEADS_PER_STEP, blk, blk), F32),
                        pltpu.VMEM((HEADS_PER_STEP, blk, blk), BF16),
                        pltpu.VMEM((HEADS_PER_STEP, blk, blk), BF16)],
        compiler_params=pltpu.CompilerParams(
            dimension_semantics=("arbitrary", "arbitrary", "arbitrary"),
            vmem_limit_bytes=VMEM_LIMIT_BYTES),
        name="moba_attn",
    )(slopes, q, by_block(k), by_block(v))


def _ssm_matrices(lam_re, lam_im, log_dt, b_re, b_im, c_re, c_im, d_skip):
    T, G, P, Hc = SSM_CHUNK, SSM_GROUPS, SSM_STATE, SSM_GROUP
    dt = jnp.exp(log_dt)[:, None]
    mag = jnp.exp(lam_re * dt)
    ar = mag * jnp.cos(lam_im * dt)
    ai = mag * jnp.sin(lam_im * dt)
    nr, ni = ar - 1.0, ai
    den = lam_re * lam_re + lam_im * lam_im
    fr = (nr * lam_re + ni * lam_im) / den
    fi = (ni * lam_re - nr * lam_im) / den
    bbr = fr[..., None] * b_re - fi[..., None] * b_im
    bbi = fr[..., None] * b_im + fi[..., None] * b_re
    pr, pi = [jnp.ones_like(ar)], [jnp.zeros_like(ar)]
    for _ in range(T):
        pr, pi = pr + [pr[-1] * ar - pi[-1] * ai], pi + [pr[-1] * ai + pi[-1] * ar]
    pw_r, pw_i = jnp.stack(pr), jnp.stack(pi)

    ca_r = c_re[None] * pw_r[:, :, None, :] - c_im[None] * pw_i[:, :, None, :]
    ca_i = c_re[None] * pw_i[:, :, None, :] + c_im[None] * pw_r[:, :, None, :]
    kern = (jnp.einsum('tgop,gpi->tgoi', ca_r[:T], bbr, precision=HIGHEST)
            - jnp.einsum('tgop,gpi->tgoi', ca_i[:T], bbi, precision=HIGHEST))
    kern = jnp.concatenate([kern, jnp.zeros_like(kern[:1])], axis=0)
    s_idx = np.arange(T)[:, None]
    t_idx = np.arange(T)[None, :]
    tau = np.where(t_idx >= s_idx, t_idx - s_idx, T)
    w = kern[tau]
    skip = d_skip.reshape(G, Hc)[:, :, None] * jnp.eye(Hc, dtype=F32)[None]
    w = w + jnp.eye(T, dtype=F32)[:, :, None, None, None] * skip[None, None]
    w = w.transpose(2, 0, 4, 1, 3).reshape(G, T * Hc, T * Hc)

    rev_r, rev_i = pw_r[T - 1::-1][:T], pw_i[T - 1::-1][:T]
    n_r = rev_r[..., None] * bbr[None] - rev_i[..., None] * bbi[None]
    n_i = rev_r[..., None] * bbi[None] + rev_i[..., None] * bbr[None]
    n_r = n_r.transpose(1, 0, 3, 2).reshape(G, T * Hc, P)
    n_i = n_i.transpose(1, 0, 3, 2).reshape(G, T * Hc, P)

    m_r = ca_r[1:].transpose(1, 3, 0, 2).reshape(G, P, T * Hc)
    m_i = (-ca_i[1:]).transpose(1, 3, 0, 2).reshape(G, P, T * Hc)

    Q = G // SSM_PAIR

    def pair_diag(a):
        a = a.reshape(Q, SSM_PAIR, *a.shape[1:])
        z = jnp.zeros_like(a[:, 0])
        return jnp.concatenate([jnp.concatenate([a[:, 0], z], axis=2),
                                jnp.concatenate([z, a[:, 1]], axis=2)], axis=1)

    w_pair = pair_diag(w)
    n_pair = jnp.concatenate([pair_diag(n_r), pair_diag(n_i)], axis=2)
    m_pair = jnp.concatenate([pair_diag(m_r), pair_diag(m_i)], axis=1)
    a_chunk = jnp.stack([pw_r[T].reshape(1, G * P), pw_i[T].reshape(1, G * P)])
    return w_pair, n_pair, m_pair, a_chunk


def _ssm_state_in_kernel(u_ref, n_ref, er_ref, ei_ref):
    e = jnp.dot(u_ref[...], n_ref[...], precision=HIGHEST, preferred_element_type=F32)
    half = er_ref.shape[-1]
    er_ref[...] = e[:, :half]
    ei_ref[...] = e[:, half:]


def _ssm_scan_kernel(a_ref, er_ref, ei_ref, hr_ref, hi_ref, *, n_seq):
    ar = a_ref[0]
    ai = a_ref[1]
    steps = er_ref.shape[0] // n_seq
    zero = jnp.zeros_like(ar)

    def body(c, carry):
        out = []
        for b in range(n_seq):
            hr, hi = carry[2 * b], carry[2 * b + 1]
            r = b * steps + c
            hr_ref[pl.ds(r, 1), :] = hr
            hi_ref[pl.ds(r, 1), :] = hi
            er = er_ref[pl.ds(r, 1), :]
            ei = ei_ref[pl.ds(r, 1), :]
            out += [ar * hr - ai * hi + er, ar * hi + ai * hr + ei]
        return tuple(out)

    lax.fori_loop(0, steps, body, (zero,) * (2 * n_seq))


def _ssm_out_kernel(u_ref, w_ref, hr_ref, hi_ref, m_ref, y_ref):
    half = hr_ref.shape[-1]
    y = jnp.dot(u_ref[...], w_ref[...], precision=HIGHEST, preferred_element_type=F32)
    y += jnp.dot(hr_ref[...], m_ref[:half, :], precision=HIGHEST, preferred_element_type=F32)
    y += jnp.dot(hi_ref[...], m_ref[half:, :], precision=HIGHEST, preferred_element_type=F32)
    y_ref[...] = y


def _s5_ssm(u, n_batch, mats):
    w_pair, n_pair, m_pair, a_chunk = mats
    tokens = u.shape[0]
    T, Hc, P = SSM_CHUNK, SSM_GROUP, SSM_STATE
    Q = SSM_GROUPS // SSM_PAIR
    rows = tokens // T
    pair_in = SSM_PAIR * T * Hc
    pair_st = SSM_PAIR * P
    n_state = SSM_GROUPS * P
    u2 = u.reshape(rows, T, Q, SSM_PAIR, Hc).transpose(2, 0, 3, 1, 4).reshape(Q, rows, pair_in)

    cparams = pltpu.CompilerParams(dimension_semantics=("arbitrary",),
                                   vmem_limit_bytes=VMEM_LIMIT_BYTES)
    er, ei = pl.pallas_call(
        _ssm_state_in_kernel,
        grid=(Q,),
        in_specs=[pl.BlockSpec((None, rows, pair_in), lambda q: (q, 0, 0)),
                  pl.BlockSpec((None, pair_in, 2 * pair_st), lambda q: (q, 0, 0))],
        out_specs=[pl.BlockSpec((rows, pair_st), lambda q: (0, q))] * 2,
        out_shape=[jax.ShapeDtypeStruct((rows, n_state), F32)] * 2,
        compiler_params=cparams, name="ssm_state_in",
    )(u2, n_pair)

    lane_spec = pl.BlockSpec((rows, SCAN_LANES), lambda q: (0, q))
    hr, hi = pl.pallas_call(
        functools.partial(_ssm_scan_kernel, n_seq=n_batch),
        grid=(n_state // SCAN_LANES,),
        in_specs=[pl.BlockSpec((2, 1, SCAN_LANES), lambda q: (0, 0, q)), lane_spec, lane_spec],
        out_specs=[lane_spec, lane_spec],
        out_shape=[jax.ShapeDtypeStruct((rows, n_state), F32)] * 2,
        compiler_params=cparams, name="ssm_scan",
    )(a_chunk, er, ei)

    st_spec = pl.BlockSpec((rows, pair_st), lambda q: (0, q))
    y2 = pl.pallas_call(
        _ssm_out_kernel,
        grid=(Q,),
        in_specs=[pl.BlockSpec((None, rows, pair_in), lambda q: (q, 0, 0)),
                  pl.BlockSpec((None, pair_in, pair_in), lambda q: (q, 0, 0)),
                  st_spec, st_spec,
                  pl.BlockSpec((None, 2 * pair_st, pair_in), lambda q: (q, 0, 0))],
        out_specs=pl.BlockSpec((None, rows, pair_in), lambda q: (q, 0, 0)),
        out_shape=jax.ShapeDtypeStruct((Q, rows, pair_in), F32),
        compiler_params=cparams, name="ssm_out",
    )(u2, w_pair, hr, hi, m_pair)
    return (y2.reshape(Q, rows, SSM_PAIR, T, Hc).transpose(1, 3, 0, 2, 4)
            .reshape(tokens, SSM_WIDTH))


def _post_kernel(x1_ref, attn_ref, y_ref, ga_ref, gs_ref, wup_ref, wglu_ref, wout_ref,
                 n3_ref, wg_ref, wu_ref, wd_ref, nf_ref, o_ref):
    attn = jnp.dot(attn_ref[...], wup_ref[...], preferred_element_type=F32)
    y = jax.nn.gelu(y_ref[...], approximate=True).astype(BF16)
    glu = jnp.dot(y, wglu_ref[...], preferred_element_type=F32)
    ssm_out = glu[:, :D_MODEL] * jax.nn.sigmoid(glu[:, D_MODEL:])
    merged = jax.nn.sigmoid(ga_ref[...]) * attn + jax.nn.sigmoid(gs_ref[...]) * ssm_out
    x2 = x1_ref[...] + jnp.dot(merged.astype(BF16), wout_ref[...], preferred_element_type=F32)
    x3 = _swiglu_half_step(x2, n3_ref[...], wg_ref, wu_ref, wd_ref)
    o_ref[...] = _rms(x3, nf_ref[...])


def _post(x1, attn, y, ga, gs, wup, wglu, wout, n3, wg, wu, wd, nf):
    tokens = x1.shape[0]
    tm = TOKEN_TILE
    row = lambda w: pl.BlockSpec((tm, w), lambda i: (i, 0))
    consts = (wup, wglu, wout, n3, wg, wu, wd, nf)
    return pl.pallas_call(
        _post_kernel,
        grid=(tokens // tm,),
        in_specs=[row(D_MODEL), row(ATTN_WIDTH), row(SSM_WIDTH), row(D_MODEL), row(D_MODEL)]
        + [_const_spec(c.shape) for c in consts],
        out_specs=row(D_MODEL),
        out_shape=jax.ShapeDtypeStruct((tokens, D_MODEL), F32),
        compiler_params=pltpu.CompilerParams(
            dimension_semantics=("arbitrary",), vmem_limit_bytes=VMEM_LIMIT_BYTES),
        name="post",
    )(x1, attn, y, ga, gs, *consts)


def kernel(x, ffn1_norm, ffn1_w_gate, ffn1_w_up, ffn1_w_down, mix_norm, w_in, w_attn_up, ssm_lambda_re, ssm_lambda_im, ssm_log_dt, ssm_b_re, ssm_b_im, ssm_c_re, ssm_c_im, ssm_d, w_ssm_glu, w_out, ffn2_norm, ffn2_w_gate, ffn2_w_up, ffn2_w_down, final_norm):
    B, S, D = x.shape
    depth = ffn1_norm.shape[0]
    assert depth == 1, "final norm is fused into the single layer's last stage"
    tokens = B * S
    slopes = jnp.asarray(2.0 ** (-8.0 * np.arange(1, N_HEADS + 1) / N_HEADS), dtype=F32)
    bf = lambda w: w.astype(BF16)
    xt = x.reshape(tokens, D)
    for l in range(depth):
        x1, q, k, v, u, ga, gs = _ffn1_proj(
            xt, ffn1_norm[l][None], bf(ffn1_w_gate[l]), bf(ffn1_w_up[l]), bf(ffn1_w_down[l]),
            mix_norm[l][None], bf(w_in[l]))
        attn = _moba_attention(q, k, v, slopes, B)
        mats = _ssm_matrices(ssm_lambda_re[l], ssm_lambda_im[l], ssm_log_dt[l], ssm_b_re[l],
                             ssm_b_im[l], ssm_c_re[l], ssm_c_im[l], ssm_d[l])
        y = _s5_ssm(u, B, mats)
        xt = _post(x1, attn, y, ga, gs, bf(w_attn_up[l]), bf(w_ssm_glu[l]), bf(w_out[l]),
                   ffn2_norm[l][None], bf(ffn2_w_gate[l]), bf(ffn2_w_up[l]), bf(ffn2_w_down[l]),
                   final_norm[None])
    return xt.reshape(B, S, D)
```

```python
import functools
import math

import numpy as np
import jax
import jax.numpy as jnp
from jax import lax
from jax.experimental import pallas as pl
from jax.experimental.pallas import tpu as pltpu

F32 = jnp.float32
BF16 = jnp.bfloat16
HIGHEST = lax.Precision.HIGHEST

D_MODEL = 1024
N_HEADS = 8
HEAD_DIM = 64
ATTN_WIDTH = N_HEADS * HEAD_DIM
MOBA_BLOCK = 256
MOBA_TOPK = 3
SSM_GROUP = 16
SSM_GROUPS = 32
SSM_WIDTH = SSM_GROUP * SSM_GROUPS
SSM_STATE = 64
D_FF = 2816
EPS = 1e-6
LOG2_E = math.log2(math.e)

VMEM_LIMIT_BYTES = 56 * 1024 * 1024
TOKEN_TILE = MOBA_BLOCK
HEADS_PER_STEP = 4
SOFTMAX_ROWS = 32
LANES = 128
SSM_CHUNK = 8
SSM_LANE_GROUPS = LANES // SSM_GROUP
SSM_ROW_TILE = 4096
SCAN_LANES = 512


def _const_spec(shape):
    nd = len(shape)
    return pl.BlockSpec(shape, lambda *_: (0,) * nd, pipeline_mode=pl.Buffered(1))


def _rms(x, g):
    return x * lax.rsqrt(jnp.mean(x * x, axis=-1, keepdims=True) + EPS) * g


def _swiglu_half_step(x, g_norm, wg_ref, wu_ref, wd_ref):
    h = _rms(x, g_norm).astype(BF16)
    gate = jnp.dot(h, wg_ref[...], preferred_element_type=F32)
    up = jnp.dot(h, wu_ref[...], preferred_element_type=F32)
    act = (gate * jax.nn.sigmoid(gate) * up).astype(BF16)
    return x + 0.5 * jnp.dot(act, wd_ref[...], preferred_element_type=F32)


def _ffn1_proj_kernel(x_ref, n1_ref, wg_ref, wu_ref, wd_ref, n2_ref, win_ref,
                      x1_ref, q_ref, k_ref, v_ref, u_ref, ga_ref, gs_ref):
    x1 = _swiglu_half_step(x_ref[...], n1_ref[...], wg_ref, wu_ref, wd_ref)
    x1_ref[...] = x1
    h = _rms(x1, n2_ref[...]).astype(BF16)
    col = 0
    for ref in (q_ref, k_ref, v_ref, u_ref, ga_ref, gs_ref):
        width = ref.shape[-1]
        ref[...] = jnp.dot(h, win_ref[:, col:col + width],
                           preferred_element_type=F32).astype(ref.dtype)
        col += width


def _ffn1_proj(x, n1, wg, wu, wd, n2, win):
    tokens = x.shape[0]
    tm = TOKEN_TILE
    row = lambda w: pl.BlockSpec((tm, w), lambda i: (i, 0))
    widths = (ATTN_WIDTH, ATTN_WIDTH, ATTN_WIDTH, SSM_WIDTH, D_MODEL, D_MODEL)
    dtypes = (F32, F32, BF16, F32, F32, F32)
    return pl.pallas_call(
        _ffn1_proj_kernel,
        grid=(tokens // tm,),
        in_specs=[row(D_MODEL), _const_spec(n1.shape), _const_spec(wg.shape),
                  _const_spec(wu.shape), _const_spec(wd.shape), _const_spec(n2.shape),
                  _const_spec(win.shape)],
        out_specs=[row(D_MODEL)] + [row(w) for w in widths],
        out_shape=[jax.ShapeDtypeStruct((tokens, D_MODEL), F32)]
        + [jax.ShapeDtypeStruct((tokens, w), dt) for w, dt in zip(widths, dtypes)],
        compiler_params=pltpu.CompilerParams(
            dimension_semantics=("arbitrary",), vmem_limit_bytes=VMEM_LIMIT_BYTES),
        name="ffn1_proj",
    )(x, n1, wg, wu, wd, n2, win)


def _moba_kernel(slopes_ref, q_ref, k_ref, v_ref, o_ref, kmean_ref, mask_ref, bias_ref, vt_ref,
                 t_a, t_b, p_a, p_b):
    hg = pl.program_id(1)
    i = pl.program_id(2)
    blk = MOBA_BLOCK
    n_blocks = k_ref.shape[0]
    n_heads = HEADS_PER_STEP
    neg_inf = -jnp.inf
    s_io = lax.broadcasted_iota(jnp.int32, (blk, blk), 0)
    slopes = [slopes_ref[hg * n_heads + hh] * LOG2_E for hh in range(n_heads)]
    rows = lambda hh: slice(hh * HEAD_DIM, (hh + 1) * HEAD_DIM)

    @pl.when(i == 0)
    def _():
        kmean_ref[...] = jnp.mean(k_ref[...], axis=1)
        for hh in range(n_heads):
            bias_ref[hh] = slopes[hh] * s_io.astype(F32)

        def transpose_values(j, _):
            vt_ref[j] = v_ref[j].astype(F32).T.astype(BF16)
            return _
        lax.fori_loop(0, n_blocks, transpose_values, 0)

    qt = q_ref[...].T
    row_head = lax.broadcasted_iota(jnp.int32, qt.shape, 0) // HEAD_DIM
    blk_id = lax.broadcasted_iota(jnp.int32, (n_blocks, blk), 0).astype(F32)
    past = blk_id < i.astype(F32)

    qs = []
    for hh in range(n_heads):
        qm = jnp.where(row_head == hh, qt, 0.0)
        gate = jnp.dot(kmean_ref[...], qm, precision=HIGHEST, preferred_element_type=F32)
        g = jnp.where(past, gate, neg_inf)
        sel = jnp.zeros(g.shape, jnp.bool_)
        for _ in range(MOBA_TOPK):
            top = jnp.max(g, axis=0, keepdims=True)
            first = jnp.min(jnp.where(g == top, blk_id, float(n_blocks)), axis=0, keepdims=True)
            pick = (blk_id == first) & (top > neg_inf)
            sel = sel | pick
            g = jnp.where(pick, neg_inf, g)
        mask_ref[hh] = jnp.where(sel, 0.0, neg_inf).astype(F32)
        qs.append((qm * (HEAD_DIM ** -0.5 * LOG2_E)).astype(BF16))

    last_block = n_blocks - 1
    chunks = [slice(c, c + SOFTMAX_ROWS) for c in range(0, blk, SOFTMAX_ROWS)]

    def qk_into(t_ref, j):
        kj = k_ref[j].astype(BF16)
        for hh in range(n_heads):
            t_ref[hh] = jnp.dot(kj, qs[hh], preferred_element_type=F32) + bias_ref[hh]

    def pv(p_ref, j, alpha, acc):
        vj = vt_ref[j]
        return tuple(alpha[hh] * acc[hh] + jnp.dot(vj[rows(hh)], p_ref[hh],
                                                   preferred_element_type=F32)
                     for hh in range(n_heads))

    def softmax_into(t_ref, p_ref, hh, row, m, l, causal):
        def logits(c):
            t = t_ref[hh, c, :]
            if not causal:
                return t
            key = lax.broadcasted_iota(jnp.int32, t.shape, 0) + c.start
            query = lax.broadcasted_iota(jnp.int32, t.shape, 1)
            return jnp.where(key <= query, t, neg_inf)

        cmax = logits(chunks[0])
        for c in chunks[1:]:
            cmax = jnp.maximum(cmax, logits(c))
        cmax = jnp.max(cmax, axis=0, keepdims=True) + row
        m_new = cmax if m is None else jnp.maximum(m, cmax)
        shift = row - m_new
        psum = None
        for c in chunks:
            p = jnp.exp2(logits(c) + shift)
            p_ref[hh, c, :] = p.astype(BF16)
            psum = p if psum is None else psum + p
        psum = jnp.sum(psum, axis=0, keepdims=True)
        if m is None:
            return jnp.ones_like(psum), m_new, psum
        alpha = jnp.exp2(m - m_new)
        return alpha, m_new, alpha * l + psum

    def block_row(hh, j):
        return mask_ref[hh, pl.ds(j, 1), :] + slopes[hh] * ((j - i) * blk).astype(F32)

    qk_into(t_b, i)
    qk_into(t_a, 0)
    zero_row = jnp.zeros((1, blk), F32)
    own = [softmax_into(t_b, p_b, hh, zero_row, None, None, True) for hh in range(n_heads)]
    alpha0, m0, l0 = (tuple(x) for x in zip(*own))
    acc0 = (jnp.zeros((HEAD_DIM, blk), F32),) * n_heads

    def body(n, carry):
        prev_j, a_prev, m, l, acc = carry
        j0 = 2 * n
        j1 = jnp.minimum(j0 + 1, last_block)
        qk_into(t_b, j1)
        acc = pv(p_b, prev_j, a_prev, acc)
        st0 = [softmax_into(t_a, p_a, hh, block_row(hh, j0), m[hh], l[hh], False)
               for hh in range(n_heads)]
        a0, m, l = (tuple(x) for x in zip(*st0))
        qk_into(t_a, jnp.minimum(j0 + 2, last_block))
        acc = pv(p_a, j0, a0, acc)
        st1 = [softmax_into(t_b, p_b, hh, block_row(hh, j1), m[hh], l[hh], False)
               for hh in range(n_heads)]
        a1, m, l = (tuple(x) for x in zip(*st1))
        return j1, a1, m, l, acc

    prev_j, a_prev, _, l, acc = lax.fori_loop(
        0, (i + 1) // 2, body, (i, alpha0, m0, l0, acc0))
    acc = pv(p_b, prev_j, a_prev, acc)
    out_t = jnp.concatenate([acc[hh] / l[hh] for hh in range(n_heads)], axis=0)
    o_ref[...] = out_t.T.astype(o_ref.dtype)


def _moba_attention(q, k, v, slopes, n_batch):
    tokens, width = q.shape
    blk = MOBA_BLOCK
    n_blocks = tokens // n_batch // blk
    step_width = HEADS_PER_STEP * HEAD_DIM
    q_spec = pl.BlockSpec((blk, step_width), lambda b, g, i: (b * n_blocks + i, g))
    kv_spec = pl.BlockSpec((n_blocks, blk, step_width), lambda b, g, i: (b, 0, g))
    by_block = lambda a: a.reshape(tokens // blk, blk, width)
    return pl.pallas_call(
        _moba_kernel,
        grid=(n_batch, width // step_width, n_blocks),
        in_specs=[pl.BlockSpec(memory_space=pltpu.SMEM), q_spec, kv_spec, kv_spec],
        out_specs=q_spec,
        out_shape=jax.ShapeDtypeStruct((tokens, width), BF16),
        scratch_shapes=[pltpu.VMEM((n_blocks, step_width), F32),
                        pltpu.VMEM((HEADS_PER_STEP, n_blocks, blk), F32),
                        pltpu.VMEM((HEADS_PER_STEP, blk, blk), F32),
                        pltpu.VMEM((n_blocks, step_width, blk), BF16),
                        pltpu.VMEM((HEADS_PER_STEP, blk, blk), F32),
                        pltpu.VMEM((HEADS_PER_STEP, blk, blk), F32),
                        pltpu.VMEM((HEADS_PER_STEP, blk, blk), BF16),
                        pltpu.VMEM((HEADS_PER_STEP, blk, blk), BF16)],
        compiler_params=pltpu.CompilerParams(
            dimension_semantics=("arbitrary", "arbitrary", "arbitrary"),
            vmem_limit_bytes=VMEM_LIMIT_BYTES),
        name="moba_attn",
    )(slopes, q, by_block(k), by_block(v))


def _ssm_matrices(lam_re, lam_im, log_dt, b_re, b_im, c_re, c_im, d_skip):
    T, G, P, Hc = SSM_CHUNK, SSM_GROUPS, SSM_STATE, SSM_GROUP
    dt = jnp.exp(log_dt)[:, None]
    mag = jnp.exp(lam_re * dt)
    ar = mag * jnp.cos(lam_im * dt)
    ai = mag * jnp.sin(lam_im * dt)
    nr, ni = ar - 1.0, ai
    den = lam_re * lam_re + lam_im * lam_im
    fr = (nr * lam_re + ni * lam_im) / den
    fi = (ni * lam_re - nr * lam_im) / den
    bbr = fr[..., None] * b_re - fi[..., None] * b_im
    bbi = fr[..., None] * b_im + fi[..., None] * b_re
    pr, pi = [jnp.ones_like(ar)], [jnp.zeros_like(ar)]
    for _ in range(T):
        pr, pi = pr + [pr[-1] * ar - pi[-1] * ai], pi + [pr[-1] * ai + pi[-1] * ar]
    pw_r, pw_i = jnp.stack(pr), jnp.stack(pi)

    ca_r = c_re[None] * pw_r[:, :, None, :] - c_im[None] * pw_i[:, :, None, :]
    ca_i = c_re[None] * pw_i[:, :, None, :] + c_im[None] * pw_r[:, :, None, :]
    kern = (jnp.einsum('tgop,gpi->tgoi', ca_r[:T], bbr, precision=HIGHEST)
            - jnp.einsum('tgop,gpi->tgoi', ca_i[:T], bbi, precision=HIGHEST))
    skip = d_skip.reshape(G, Hc)[:, :, None] * jnp.eye(Hc, dtype=F32)[None]
    kern = kern.at[0].add(skip)

    GB = SSM_LANE_GROUPS
    CB = G // GB
    k_c = kern.transpose(1, 0, 3, 2).reshape(CB, GB, T, Hc, Hc).transpose(0, 2, 1, 3, 4)
    k_c = jnp.tile(k_c.reshape(CB, T, GB * Hc, Hc), (1, 1, 1, GB))
    rev_r, rev_i = pw_r[T - 1::-1], pw_i[T - 1::-1]
    n_r = rev_r[..., None] * bbr[None] - rev_i[..., None] * bbi[None]
    n_i = rev_r[..., None] * bbi[None] + rev_i[..., None] * bbr[None]
    n_c = jnp.concatenate([n_r, n_i], axis=2).transpose(0, 1, 3, 2)
    n_c = n_c.reshape(T, CB, GB, Hc, 2 * P).transpose(1, 0, 2, 3, 4).reshape(CB, T * LANES, 2 * P)
    m_c = jnp.stack([ca_r[1:], -ca_i[1:]])
    m_c = m_c.reshape(2, T, CB, GB, Hc, P).transpose(2, 0, 3, 5, 1, 4).reshape(
        CB, 2 * GB * P, T * Hc)
    a_chunk = jnp.stack([pw_r[T].reshape(1, G * P), pw_i[T].reshape(1, G * P)])
    return k_c, n_c, m_c, a_chunk


def _div(x, d):
    return x >> (d.bit_length() - 1)


def _mod(x, d):
    return x & (d - 1)


def _spread_block_diag(compact, src_of_col, row_group, col_group):
    n_rows, n_src = compact.shape
    n_cols = n_rows
    src = lax.broadcasted_iota(jnp.int32, (n_src, n_cols), 0)
    col = lax.broadcasted_iota(jnp.int32, (n_src, n_cols), 1)
    spread = jnp.where(src == src_of_col(col), 1.0, 0.0).astype(BF16)
    full = jnp.dot(compact.astype(BF16), spread, preferred_element_type=F32)
    r = lax.broadcasted_iota(jnp.int32, full.shape, 0)
    c = lax.broadcasted_iota(jnp.int32, full.shape, 1)
    return jnp.where(row_group(r) == col_group(c), full, 0.0).astype(BF16)


def _chunk_rows(u_ref):
    n = u_ref.shape[0] // SSM_CHUNK
    return jnp.concatenate([u_ref[pl.ds(s, n, stride=SSM_CHUNK), :] for s in range(SSM_CHUNK)],
                           axis=1).astype(BF16)


def _ssm_state_in_kernel(u_ref, n_c_ref, er_ref, ei_ref, n_s):
    GB, P, Hc = SSM_LANE_GROUPS, SSM_STATE, SSM_GROUP

    @pl.when(pl.program_id(1) == 0)
    def _():
        n_s[...] = _spread_block_diag(
            n_c_ref[...], lambda c: _div(c, GB * P) * P + _mod(c, P),
            lambda r: _mod(_div(r, Hc), GB), lambda c: _mod(_div(c, P), GB))

    e = jnp.dot(_chunk_rows(u_ref), n_s[...], preferred_element_type=F32)
    half = er_ref.shape[-1]
    er_ref[...] = e[:, :half]
    ei_ref[...] = e[:, half:]


def _ssm_scan_kernel(a_ref, er_ref, ei_ref, hr_ref, hi_ref, *, n_seq):
    ar = a_ref[0]
    ai = a_ref[1]
    steps = er_ref.shape[0] // n_seq
    zero = jnp.zeros_like(ar)

    def body(c, carry):
        out = []
        for b in range(n_seq):
            hr, hi = carry[2 * b], carry[2 * b + 1]
            r = b * steps + c
            hr_ref[pl.ds(r, 1), :] = hr
            hi_ref[pl.ds(r, 1), :] = hi
            er = er_ref[pl.ds(r, 1), :]
            ei = ei_ref[pl.ds(r, 1), :]
            out += [ar * hr - ai * hi + er, ar * hi + ai * hr + ei]
        return tuple(out)

    lax.fori_loop(0, steps, body, (zero,) * (2 * n_seq))


def _ssm_out_kernel(u_ref, k_c_ref, m_c_ref, hr_ref, hi_ref, y_ref, w_s, m_s):
    T, GB, P, Hc = SSM_CHUNK, SSM_LANE_GROUPS, SSM_STATE, SSM_GROUP

    @pl.when(pl.program_id(1) == 0)
    def _():
        w_s[...] = jnp.zeros_like(w_s)
        r = lax.broadcasted_iota(jnp.int32, (LANES, LANES), 0)
        c = lax.broadcasted_iota(jnp.int32, (LANES, LANES), 1)
        same_group = _div(r, Hc) == _div(c, Hc)
        for tau in range(T):
            blk = jnp.where(same_group, k_c_ref[tau], 0.0).astype(BF16)
            for s in range(T - tau):
                w_s[s * LANES:(s + 1) * LANES, (s + tau) * LANES:(s + tau + 1) * LANES] = blk
        m_s[...] = _spread_block_diag(
            m_c_ref[...], lambda c: _div(c, LANES) * Hc + _mod(c, Hc),
            lambda r: _mod(_div(r, P), GB), lambda c: _mod(_div(c, Hc), GB))

    h = jnp.concatenate([hr_ref[...], hi_ref[...]], axis=1).astype(BF16)
    y = (jnp.dot(_chunk_rows(u_ref), w_s[...], preferred_element_type=F32)
         + jnp.dot(h, m_s[...], preferred_element_type=F32))
    n = y.shape[0]
    for t in range(T):
        y_ref[pl.ds(t, n, stride=T), :] = y[:, t * LANES:(t + 1) * LANES]


def _s5_ssm(u, n_batch, ops):
    k_c, n_c, m_c, a_chunk = ops
    tokens = u.shape[0]
    T, P = SSM_CHUNK, SSM_STATE
    col_blocks = SSM_WIDTH // LANES
    rt = SSM_ROW_TILE
    chunks = tokens // T
    ct = rt // T
    op_dim = T * LANES
    half = SSM_LANE_GROUPS * P
    n_state = SSM_GROUPS * P
    grid = (col_blocks, tokens // rt)
    u_spec = pl.BlockSpec((rt, LANES), lambda cb, r: (r, cb))
    st_spec = pl.BlockSpec((ct, half), lambda cb, r: (r, cb))
    cparams = pltpu.CompilerParams(dimension_semantics=("arbitrary", "arbitrary"),
                                   vmem_limit_bytes=VMEM_LIMIT_BYTES)

    er, ei = pl.pallas_call(
        _ssm_state_in_kernel,
        grid=grid,
        in_specs=[u_spec, pl.BlockSpec((None, op_dim, LANES), lambda cb, r: (cb, 0, 0))],
        out_specs=[st_spec, st_spec],
        out_shape=[jax.ShapeDtypeStruct((chunks, n_state), F32)] * 2,
        scratch_shapes=[pltpu.VMEM((op_dim, op_dim), BF16)],
        compiler_params=cparams, name="ssm_state_in",
    )(u, n_c)

    lane_spec = pl.BlockSpec((chunks, SCAN_LANES), lambda q: (0, q))
    hr, hi = pl.pallas_call(
        functools.partial(_ssm_scan_kernel, n_seq=n_batch),
        grid=(n_state // SCAN_LANES,),
        in_specs=[pl.BlockSpec((2, 1, SCAN_LANES), lambda q: (0, 0, q)), lane_spec, lane_spec],
        out_specs=[lane_spec, lane_spec],
        out_shape=[jax.ShapeDtypeStruct((chunks, n_state), F32)] * 2,
        compiler_params=pltpu.CompilerParams(dimension_semantics=("arbitrary",),
                                             vmem_limit_bytes=VMEM_LIMIT_BYTES),
        name="ssm_scan",
    )(a_chunk, er, ei)

    return pl.pallas_call(
        _ssm_out_kernel,
        grid=grid,
        in_specs=[u_spec,
                  pl.BlockSpec((None, T, LANES, LANES), lambda cb, r: (cb, 0, 0, 0)),
                  pl.BlockSpec((None, op_dim, LANES), lambda cb, r: (cb, 0, 0)),
                  st_spec, st_spec],
        out_specs=u_spec,
        out_shape=jax.ShapeDtypeStruct((tokens, SSM_WIDTH), F32),
        scratch_shapes=[pltpu.VMEM((op_dim, op_dim), BF16), pltpu.VMEM((op_dim, op_dim), BF16)],
        compiler_params=cparams, name="ssm_out",
    )(u, k_c, m_c, hr, hi)


def _post_kernel(x1_ref, attn_ref, y_ref, ga_ref, gs_ref, wup_ref, wglu_ref, wout_ref,
                 n3_ref, wg_ref, wu_ref, wd_ref, nf_ref, o_ref):
    attn = jnp.dot(attn_ref[...], wup_ref[...], preferred_element_type=F32)
    y = jax.nn.gelu(y_ref[...], approximate=True).astype(BF16)
    glu = jnp.dot(y, wglu_ref[...], preferred_element_type=F32)
    ssm_out = glu[:, :D_MODEL] * jax.nn.sigmoid(glu[:, D_MODEL:])
    merged = jax.nn.sigmoid(ga_ref[...]) * attn + jax.nn.sigmoid(gs_ref[...]) * ssm_out
    x2 = x1_ref[...] + jnp.dot(merged.astype(BF16), wout_ref[...], preferred_element_type=F32)
    x3 = _swiglu_half_step(x2, n3_ref[...], wg_ref, wu_ref, wd_ref)
    o_ref[...] = _rms(x3, nf_ref[...])


def _post(x1, attn, y, ga, gs, wup, wglu, wout, n3, wg, wu, wd, nf):
    tokens = x1.shape[0]
    tm = TOKEN_TILE
    row = lambda w: pl.BlockSpec((tm, w), lambda i: (i, 0))
    consts = (wup, wglu, wout, n3, wg, wu, wd, nf)
    return pl.pallas_call(
        _post_kernel,
        grid=(tokens // tm,),
        in_specs=[row(D_MODEL), row(ATTN_WIDTH), row(SSM_WIDTH), row(D_MODEL), row(D_MODEL)]
        + [_const_spec(c.shape) for c in consts],
        out_specs=row(D_MODEL),
        out_shape=jax.ShapeDtypeStruct((tokens, D_MODEL), F32),
        compiler_params=pltpu.CompilerParams(
            dimension_semantics=("arbitrary",), vmem_limit_bytes=VMEM_LIMIT_BYTES),
        name="post",
    )(x1, attn, y, ga, gs, *consts)


def kernel(x, ffn1_norm, ffn1_w_gate, ffn1_w_up, ffn1_w_down, mix_norm, w_in, w_attn_up, ssm_lambda_re, ssm_lambda_im, ssm_log_dt, ssm_b_re, ssm_b_im, ssm_c_re, ssm_c_im, ssm_d, w_ssm_glu, w_out, ffn2_norm, ffn2_w_gate, ffn2_w_up, ffn2_w_down, final_norm):
    B, S, D = x.shape
    depth = ffn1_norm.shape[0]
    assert depth == 1, "final norm is fused into the single layer's last stage"
    tokens = B * S
    slopes = jnp.asarray(2.0 ** (-8.0 * np.arange(1, N_HEADS + 1) / N_HEADS), dtype=F32)
    bf = lambda w: w.astype(BF16)
    xt = x.reshape(tokens, D)
    for l in range(depth):
        x1, q, k, v, u, ga, gs = _ffn1_proj(
            xt, ffn1_norm[l][None], bf(ffn1_w_gate[l]), bf(ffn1_w_up[l]), bf(ffn1_w_down[l]),
            mix_norm[l][None], bf(w_in[l]))
        attn = _moba_attention(q, k, v, slopes, B)
        mats = _ssm_matrices(ssm_lambda_re[l], ssm_lambda_im[l], ssm_log_dt[l], ssm_b_re[l],
                             ssm_b_im[l], ssm_c_re[l], ssm_c_im[l], ssm_d[l])
        y = _s5_ssm(u, B, mats)
        xt = _post(x1, attn, y, ga, gs, bf(w_attn_up[l]), bf(w_ssm_glu[l]), bf(w_out[l]),
                   ffn2_norm[l][None], bf(ffn2_w_gate[l]), bf(ffn2_w_up[l]), bf(ffn2_w_down[l]),
                   final_norm[None])
    return xt.reshape(B, S, D)
```

```python
import functools
import math

import numpy as np
import jax
import jax.numpy as jnp
from jax import lax
from jax.experimental import pallas as pl
from jax.experimental.pallas import tpu as pltpu

F32 = jnp.float32
BF16 = jnp.bfloat16
HIGHEST = lax.Precision.HIGHEST

D_MODEL = 1024
N_HEADS = 8
HEAD_DIM = 64
ATTN_WIDTH = N_HEADS * HEAD_DIM
MOBA_BLOCK = 256
MOBA_TOPK = 3
SSM_GROUP = 16
SSM_GROUPS = 32
SSM_WIDTH = SSM_GROUP * SSM_GROUPS
SSM_STATE = 64
D_FF = 2816
EPS = 1e-6
LOG2_E = math.log2(math.e)

VMEM_LIMIT_BYTES = 56 * 1024 * 1024
TOKEN_TILE = MOBA_BLOCK
HEADS_PER_STEP = 4
SOFTMAX_ROWS = 32
MXU_DEPTH = 256
BF16_ROWS = 16
V_ROWS = HEAD_DIM + BF16_ROWS
MASKED = -1e30
LANES = 128
SSM_CHUNK = 8
SSM_LANE_GROUPS = LANES // SSM_GROUP
SSM_ROW_TILE = 4096
SCAN_LANES = 512


def _const_spec(shape):
    nd = len(shape)
    return pl.BlockSpec(shape, lambda *_: (0,) * nd, pipeline_mode=pl.Buffered(1))


def _rms(x, g):
    return x * lax.rsqrt(jnp.mean(x * x, axis=-1, keepdims=True) + EPS) * g


def _swiglu_half_step(x, g_norm, wg_ref, wu_ref, wd_ref):
    h = _rms(x, g_norm).astype(BF16)
    gate = jnp.dot(h, wg_ref[...], preferred_element_type=F32)
    up = jnp.dot(h, wu_ref[...], preferred_element_type=F32)
    act = (gate * jax.nn.sigmoid(gate) * up).astype(BF16)
    return x + 0.5 * jnp.dot(act, wd_ref[...], preferred_element_type=F32)


def _ffn1_proj_kernel(x_ref, n1_ref, wg_ref, wu_ref, wd_ref, n2_ref, win_ref,
                      x1_ref, q_ref, k_ref, v_ref, u_ref, ga_ref, gs_ref):
    x1 = _swiglu_half_step(x_ref[...], n1_ref[...], wg_ref, wu_ref, wd_ref)
    x1_ref[...] = x1
    h = _rms(x1, n2_ref[...]).astype(BF16)
    col = 0
    for ref in (q_ref, k_ref, v_ref, u_ref, ga_ref, gs_ref):
        width = ref.shape[-1]
        ref[...] = jnp.dot(h, win_ref[:, col:col + width],
                           preferred_element_type=F32).astype(ref.dtype)
        col += width


def _ffn1_proj(x, n1, wg, wu, wd, n2, win):
    tokens = x.shape[0]
    tm = TOKEN_TILE
    row = lambda w: pl.BlockSpec((tm, w), lambda i: (i, 0))
    widths = (ATTN_WIDTH, ATTN_WIDTH, ATTN_WIDTH, SSM_WIDTH, D_MODEL, D_MODEL)
    dtypes = (F32, F32, BF16, F32, F32, F32)
    return pl.pallas_call(
        _ffn1_proj_kernel,
        grid=(tokens // tm,),
        in_specs=[row(D_MODEL), _const_spec(n1.shape), _const_spec(wg.shape),
                  _const_spec(wu.shape), _const_spec(wd.shape), _const_spec(n2.shape),
                  _const_spec(win.shape)],
        out_specs=[row(D_MODEL)] + [row(w) for w in widths],
        out_shape=[jax.ShapeDtypeStruct((tokens, D_MODEL), F32)]
        + [jax.ShapeDtypeStruct((tokens, w), dt) for w, dt in zip(widths, dtypes)],
        compiler_params=pltpu.CompilerParams(
            dimension_semantics=("arbitrary",), vmem_limit_bytes=VMEM_LIMIT_BYTES),
        name="ffn1_proj",
    )(x, n1, wg, wu, wd, n2, win)


def _split3(x):
    hi = x.astype(BF16).astype(F32)
    mid = (x - hi).astype(BF16).astype(F32)
    return hi, mid, x - hi - mid


def _moba_kernel(slopes_ref, q_ref, k_ref, v_ref, o_ref, kmean_ref, mask_ref, kaug_ref, qa_ref,
                 vt_ref, t_a, t_b, p_a, p_b):
    hg = pl.program_id(1)
    i = pl.program_id(2)
    blk = MOBA_BLOCK
    n_blocks = k_ref.shape[0]
    n_heads = HEADS_PER_STEP
    pair_w = MXU_DEPTH // 2
    neg_inf = -jnp.inf
    slopes = [slopes_ref[hg * n_heads + hh] * LOG2_E for hh in range(n_heads)]
    v_rows = lambda hh: slice(hh * V_ROWS, (hh + 1) * V_ROWS)
    aug_rows = slice(pair_w, pair_w + BF16_ROWS)
    aug_id = lax.broadcasted_iota(jnp.int32, (BF16_ROWS, blk), 0)

    @pl.when(i == 0)
    def _():
        kmean_ref[...] = jnp.mean(k_ref[...], axis=1)
        key = lax.broadcasted_iota(jnp.int32, kaug_ref.shape, 0).astype(F32)
        col = lax.broadcasted_iota(jnp.int32, kaug_ref.shape, 1)
        kaug_ref[...] = jnp.where(col < 3, key, jnp.where(col < 6, 1.0, 0.0)).astype(BF16)
        ones_row = jnp.where(aug_id == 0, 1.0, 0.0).astype(BF16)

        def transpose_values(j, _):
            vt = v_ref[j].astype(F32).T.astype(BF16)
            for hh in range(n_heads):
                vt_ref[j, hh * V_ROWS:hh * V_ROWS + HEAD_DIM, :] = vt[hh * HEAD_DIM:
                                                                      (hh + 1) * HEAD_DIM]
                vt_ref[j, hh * V_ROWS + HEAD_DIM:(hh + 1) * V_ROWS, :] = ones_row
            return _
        lax.fori_loop(0, n_blocks, transpose_values, 0)

    qt = q_ref[...].T
    row_head = lax.broadcasted_iota(jnp.int32, qt.shape, 0) // HEAD_DIM
    blk_id = lax.broadcasted_iota(jnp.int32, (n_blocks, blk), 0).astype(F32)
    past = blk_id < i.astype(F32)

    def aug_tile(hh, row):
        pieces = _split3(jnp.full((1, blk), slopes[hh], F32)) + _split3(row)
        tile = jnp.zeros((BF16_ROWS, blk), F32)
        for n, piece in enumerate(pieces):
            tile = jnp.where(aug_id == n, piece, tile)
        return tile.astype(BF16)

    zero_row = jnp.zeros((1, blk), F32)
    for hh in range(n_heads):
        qm = jnp.where(row_head == hh, qt, 0.0)
        gate = jnp.dot(kmean_ref[...], qm, precision=HIGHEST, preferred_element_type=F32)
        g = jnp.where(past, gate, neg_inf)
        sel = jnp.zeros(g.shape, jnp.bool_)
        for _ in range(MOBA_TOPK):
            top = jnp.max(g, axis=0, keepdims=True)
            first = jnp.min(jnp.where(g == top, blk_id, float(n_blocks)), axis=0, keepdims=True)
            pick = (blk_id == first) & (top > neg_inf)
            sel = sel | pick
            g = jnp.where(pick, neg_inf, g)
        mask_ref[hh] = jnp.where(sel, 0.0, MASKED).astype(F32)
        pair = hh * HEAD_DIM // pair_w
        qa_ref[hh, :pair_w, :] = (qm[pair * pair_w:(pair + 1) * pair_w]
                                  * (HEAD_DIM ** -0.5 * LOG2_E)).astype(BF16)
        qa_ref[hh, pair_w + BF16_ROWS:, :] = jnp.zeros((MXU_DEPTH - pair_w - BF16_ROWS, blk), BF16)

    last_block = n_blocks - 1
    chunks = [slice(c, c + SOFTMAX_ROWS) for c in range(0, blk, SOFTMAX_ROWS)]

    def qk_into(t_ref, j, own):
        kj = k_ref[j].astype(BF16)
        for hh in range(n_heads):
            row = zero_row if own else (mask_ref[hh, pl.ds(j, 1), :]
                                        + slopes[hh] * ((j - i) * blk).astype(F32))
            qa_ref[hh, aug_rows, :] = aug_tile(hh, row)
            pair = hh * HEAD_DIM // pair_w
            keys = jnp.concatenate([kj[:, pair * pair_w:(pair + 1) * pair_w], kaug_ref[...]],
                                   axis=1)
            t_ref[hh] = jnp.dot(keys, qa_ref[hh], preferred_element_type=F32)

    def pv(p_ref, j, alpha, acc):
        vj = vt_ref[j]
        return tuple(alpha[hh] * acc[hh] + jnp.dot(vj[v_rows(hh)], p_ref[hh],
                                                   preferred_element_type=F32)
                     for hh in range(n_heads))

    def softmax_into(t_ref, p_ref, hh, m, causal):
        def logits(c):
            t = t_ref[hh, c, :]
            if not causal:
                return t
            key = lax.broadcasted_iota(jnp.int32, t.shape, 0) + c.start
            query = lax.broadcasted_iota(jnp.int32, t.shape, 1)
            return jnp.where(key <= query, t, neg_inf)

        cmax = logits(chunks[0])
        for c in chunks[1:]:
            cmax = jnp.maximum(cmax, logits(c))
        cmax = jnp.max(cmax, axis=0, keepdims=True)
        m_new = cmax if m is None else jnp.maximum(m, cmax)
        for c in chunks:
            p_ref[hh, c, :] = jnp.exp2(logits(c) - m_new).astype(BF16)
        alpha = jnp.ones_like(m_new) if m is None else jnp.exp2(m - m_new)
        return alpha, m_new

    qk_into(t_b, i, True)
    qk_into(t_a, 0, False)
    own = [softmax_into(t_b, p_b, hh, None, True) for hh in range(n_heads)]
    alpha0, m0 = (tuple(x) for x in zip(*own))
    acc0 = (jnp.zeros((V_ROWS, blk), F32),) * n_heads

    def body(n, carry):
        prev_j, a_prev, m, acc = carry
        j0 = 2 * n
        j1 = jnp.minimum(j0 + 1, last_block)
        qk_into(t_b, j1, False)
        acc = pv(p_b, prev_j, a_prev, acc)
        st0 = [softmax_into(t_a, p_a, hh, m[hh], False) for hh in range(n_heads)]
        a0, m = (tuple(x) for x in zip(*st0))
        qk_into(t_a, jnp.minimum(j0 + 2, last_block), False)
        acc = pv(p_a, j0, a0, acc)
        st1 = [softmax_into(t_b, p_b, hh, m[hh], False) for hh in range(n_heads)]
        a1, m = (tuple(x) for x in zip(*st1))
        return j1, a1, m, acc

    prev_j, a_prev, _, acc = lax.fori_loop(0, (i + 1) // 2, body, (i, alpha0, m0, acc0))
    acc = pv(p_b, prev_j, a_prev, acc)
    out_t = jnp.concatenate([acc[hh][:HEAD_DIM] / acc[hh][HEAD_DIM:HEAD_DIM + 1]
                             for hh in range(n_heads)], axis=0)
    o_ref[...] = out_t.T.astype(o_ref.dtype)


def _moba_attention(q, k, v, slopes, n_batch):
    tokens, width = q.shape
    blk = MOBA_BLOCK
    n_blocks = tokens // n_batch // blk
    step_width = HEADS_PER_STEP * HEAD_DIM
    q_spec = pl.BlockSpec((blk, step_width), lambda b, g, i: (b * n_blocks + i, g))
    kv_spec = pl.BlockSpec((n_blocks, blk, step_width), lambda b, g, i: (b, 0, g))
    by_block = lambda a: a.reshape(tokens // blk, blk, width)
    return pl.pallas_call(
        _moba_kernel,
        grid=(n_batch, width // step_width, n_blocks),
        in_specs=[pl.BlockSpec(memory_space=pltpu.SMEM), q_spec, kv_spec, kv_spec],
        out_specs=q_spec,
        out_shape=jax.ShapeDtypeStruct((tokens, width), BF16),
        scratch_shapes=[pltpu.VMEM((n_blocks, step_width), F32),
                        pltpu.VMEM((HEADS_PER_STEP, n_blocks, blk), F32),
                        pltpu.VMEM((blk, MXU_DEPTH // 2), BF16),
                        pltpu.VMEM((HEADS_PER_STEP, MXU_DEPTH, blk), BF16),
                        pltpu.VMEM((n_blocks, HEADS_PER_STEP * V_ROWS, blk), BF16),
                        pltpu.VMEM((HEADS_PER_STEP, blk, blk), F32),
                        pltpu.VMEM((HEADS_PER_STEP, blk, blk), F32),
                        pltpu.VMEM((HEADS_PER_STEP, blk, blk), BF16),
                        pltpu.VMEM((HEADS_PER_STEP, blk, blk), BF16)],
        compiler_params=pltpu.CompilerParams(
            dimension_semantics=("arbitrary", "arbitrary", "arbitrary"),
            vmem_limit_bytes=VMEM_LIMIT_BYTES),
        name="moba_attn",
    )(slopes, q, by_block(k), by_block(v))


def _ssm_matrices(lam_re, lam_im, log_dt, b_re, b_im, c_re, c_im, d_skip):
    T, G, P, Hc = SSM_CHUNK, SSM_GROUPS, SSM_STATE, SSM_GROUP
    dt = jnp.exp(log_dt)[:, None]
    mag = jnp.exp(lam_re * dt)
    ar = mag * jnp.cos(lam_im * dt)
    ai = mag * jnp.sin(lam_im * dt)
    nr, ni = ar - 1.0, ai
    den = lam_re * lam_re + lam_im * lam_im
    fr = (nr * lam_re + ni * lam_im) / den
    fi = (ni * lam_re - nr * lam_im) / den
    bbr = fr[..., None] * b_re - fi[..., None] * b_im
    bbi = fr[..., None] * b_im + fi[..., None] * b_re
    pr, pi = [jnp.ones_like(ar)], [jnp.zeros_like(ar)]
    for _ in range(T):
        pr, pi = pr + [pr[-1] * ar - pi[-1] * ai], pi + [pr[-1] * ai + pi[-1] * ar]
    pw_r, pw_i = jnp.stack(pr), jnp.stack(pi)

    ca_r = c_re[None] * pw_r[:, :, None, :] - c_im[None] * pw_i[:, :, None, :]
    ca_i = c_re[None] * pw_i[:, :, None, :] + c_im[None] * pw_r[:, :, None, :]
    kern = (jnp.einsum('tgop,gpi->tgoi', ca_r[:T], bbr, precision=HIGHEST)
            - jnp.einsum('tgop,gpi->tgoi', ca_i[:T], bbi, precision=HIGHEST))
    skip = d_skip.reshape(G, Hc)[:, :, None] * jnp.eye(Hc, dtype=F32)[None]
    kern = kern.at[0].add(skip)

    GB = SSM_LANE_GROUPS
    CB = G // GB
    k_c = kern.transpose(1, 0, 3, 2).reshape(CB, GB, T, Hc, Hc).transpose(0, 2, 1, 3, 4)
    k_c = jnp.tile(k_c.reshape(CB, T, GB * Hc, Hc), (1, 1, 1, GB))
    rev_r, rev_i = pw_r[T - 1::-1], pw_i[T - 1::-1]
    n_r = rev_r[..., None] * bbr[None] - rev_i[..., None] * bbi[None]
    n_i = rev_r[..., None] * bbi[None] + rev_i[..., None] * bbr[None]
    n_c = jnp.concatenate([n_r, n_i], axis=2).transpose(0, 1, 3, 2)
    n_c = n_c.reshape(T, CB, GB, Hc, 2 * P).transpose(1, 0, 2, 3, 4).reshape(CB, T * LANES, 2 * P)
    m_c = jnp.stack([ca_r[1:], -ca_i[1:]])
    m_c = m_c.reshape(2, T, CB, GB, Hc, P).transpose(2, 0, 3, 5, 1, 4).reshape(
        CB, 2 * GB * P, T * Hc)
    a_chunk = jnp.stack([pw_r[T].reshape(1, G * P), pw_i[T].reshape(1, G * P)])
    return k_c, n_c, m_c, a_chunk


def _div(x, d):
    return x >> (d.bit_length() - 1)


def _mod(x, d):
    return x & (d - 1)


def _spread_block_diag(compact, src_of_col, row_group, col_group):
    n_rows, n_src = compact.shape
    n_cols = n_rows
    src = lax.broadcasted_iota(jnp.int32, (n_src, n_cols), 0)
    col = lax.broadcasted_iota(jnp.int32, (n_src, n_cols), 1)
    spread = jnp.where(src == src_of_col(col), 1.0, 0.0).astype(BF16)
    full = jnp.dot(compact.astype(BF16), spread, preferred_element_type=F32)
    r = lax.broadcasted_iota(jnp.int32, full.shape, 0)
    c = lax.broadcasted_iota(jnp.int32, full.shape, 1)
    return jnp.where(row_group(r) == col_group(c), full, 0.0).astype(BF16)


def _chunk_rows(u_ref):
    n = u_ref.shape[0] // SSM_CHUNK
    return jnp.concatenate([u_ref[pl.ds(s, n, stride=SSM_CHUNK), :] for s in range(SSM_CHUNK)],
                           axis=1).astype(BF16)


def _ssm_state_in_kernel(u_ref, n_c_ref, er_ref, ei_ref, n_s):
    GB, P, Hc = SSM_LANE_GROUPS, SSM_STATE, SSM_GROUP

    @pl.when(pl.program_id(1) == 0)
    def _():
        n_s[...] = _spread_block_diag(
            n_c_ref[...], lambda c: _div(c, GB * P) * P + _mod(c, P),
            lambda r: _mod(_div(r, Hc), GB), lambda c: _mod(_div(c, P), GB))

    e = jnp.dot(_chunk_rows(u_ref), n_s[...], preferred_element_type=F32)
    half = er_ref.shape[-1]
    er_ref[...] = e[:, :half]
    ei_ref[...] = e[:, half:]


def _ssm_scan_kernel(a_ref, er_ref, ei_ref, hr_ref, hi_ref, *, n_seq):
    ar = a_ref[0]
    ai = a_ref[1]
    steps = er_ref.shape[0] // n_seq
    zero = jnp.zeros_like(ar)

    def body(c, carry):
        out = []
        for b in range(n_seq):
            hr, hi = carry[2 * b], carry[2 * b + 1]
            r = b * steps + c
            hr_ref[pl.ds(r, 1), :] = hr
            hi_ref[pl.ds(r, 1), :] = hi
            er = er_ref[pl.ds(r, 1), :]
            ei = ei_ref[pl.ds(r, 1), :]
            out += [ar * hr - ai * hi + er, ar * hi + ai * hr + ei]
        return tuple(out)

    lax.fori_loop(0, steps, body, (zero,) * (2 * n_seq))


def _ssm_out_kernel(u_ref, k_c_ref, m_c_ref, hr_ref, hi_ref, y_ref, w_s, m_s):
    T, GB, P, Hc = SSM_CHUNK, SSM_LANE_GROUPS, SSM_STATE, SSM_GROUP

    @pl.when(pl.program_id(1) == 0)
    def _():
        w_s[...] = jnp.zeros_like(w_s)
        r = lax.broadcasted_iota(jnp.int32, (LANES, LANES), 0)
        c = lax.broadcasted_iota(jnp.int32, (LANES, LANES), 1)
        same_group = _div(r, Hc) == _div(c, Hc)
        for tau in range(T):
            blk = jnp.where(same_group, k_c_ref[tau], 0.0).astype(BF16)
            for s in range(T - tau):
                w_s[s * LANES:(s + 1) * LANES, (s + tau) * LANES:(s + tau + 1) * LANES] = blk
        m_s[...] = _spread_block_diag(
            m_c_ref[...], lambda c: _div(c, LANES) * Hc + _mod(c, Hc),
            lambda r: _mod(_div(r, P), GB), lambda c: _mod(_div(c, Hc), GB))

    h = jnp.concatenate([hr_ref[...], hi_ref[...]], axis=1).astype(BF16)
    y = (jnp.dot(_chunk_rows(u_ref), w_s[...], preferred_element_type=F32)
         + jnp.dot(h, m_s[...], preferred_element_type=F32))
    n = y.shape[0]
    for t in range(T):
        y_ref[pl.ds(t, n, stride=T), :] = y[:, t * LANES:(t + 1) * LANES]


def _s5_ssm(u, n_batch, ops):
    k_c, n_c, m_c, a_chunk = ops
    tokens = u.shape[0]
    T, P = SSM_CHUNK, SSM_STATE
    col_blocks = SSM_WIDTH // LANES
    rt = SSM_ROW_TILE
    chunks = tokens // T
    ct = rt // T
    op_dim = T * LANES
    half = SSM_LANE_GROUPS * P
    n_state = SSM_GROUPS * P
    grid = (col_blocks, tokens // rt)
    u_spec = pl.BlockSpec((rt, LANES), lambda cb, r: (r, cb))
    st_spec = pl.BlockSpec((ct, half), lambda cb, r: (r, cb))
    cparams = pltpu.CompilerParams(dimension_semantics=("arbitrary", "arbitrary"),
                                   vmem_limit_bytes=VMEM_LIMIT_BYTES)

    er, ei = pl.pallas_call(
        _ssm_state_in_kernel,
        grid=grid,
        in_specs=[u_spec, pl.BlockSpec((None, op_dim, LANES), lambda cb, r: (cb, 0, 0))],
        out_specs=[st_spec, st_spec],
        out_shape=[jax.ShapeDtypeStruct((chunks, n_state), F32)] * 2,
        scratch_shapes=[pltpu.VMEM((op_dim, op_dim), BF16)],
        compiler_params=cparams, name="ssm_state_in",
    )(u, n_c)

    lane_spec = pl.BlockSpec((chunks, SCAN_LANES), lambda q: (0, q))
    hr, hi = pl.pallas_call(
        functools.partial(_ssm_scan_kernel, n_seq=n_batch),
        grid=(n_state // SCAN_LANES,),
        in_specs=[pl.BlockSpec((2, 1, SCAN_LANES), lambda q: (0, 0, q)), lane_spec, lane_spec],
        out_specs=[lane_spec, lane_spec],
        out_shape=[jax.ShapeDtypeStruct((chunks, n_state), F32)] * 2,
        compiler_params=pltpu.CompilerParams(dimension_semantics=("arbitrary",),
                                             vmem_limit_bytes=VMEM_LIMIT_BYTES),
        name="ssm_scan",
    )(a_chunk, er, ei)

    return pl.pallas_call(
        _ssm_out_kernel,
        grid=grid,
        in_specs=[u_spec,
                  pl.BlockSpec((None, T, LANES, LANES), lambda cb, r: (cb, 0, 0, 0)),
                  pl.BlockSpec((None, op_dim, LANES), lambda cb, r: (cb, 0, 0)),
                  st_spec, st_spec],
        out_specs=u_spec,
        out_shape=jax.ShapeDtypeStruct((tokens, SSM_WIDTH), F32),
        scratch_shapes=[pltpu.VMEM((op_dim, op_dim), BF16), pltpu.VMEM((op_dim, op_dim), BF16)],
        compiler_params=cparams, name="ssm_out",
    )(u, k_c, m_c, hr, hi)


def _post_kernel(x1_ref, attn_ref, y_ref, ga_ref, gs_ref, wup_ref, wglu_ref, wout_ref,
                 n3_ref, wg_ref, wu_ref, wd_ref, nf_ref, o_ref):
    attn = jnp.dot(attn_ref[...], wup_ref[...], preferred_element_type=F32)
    y = jax.nn.gelu(y_ref[...], approximate=True).astype(BF16)
    glu = jnp.dot(y, wglu_ref[...], preferred_element_type=F32)
    ssm_out = glu[:, :D_MODEL] * jax.nn.sigmoid(glu[:, D_MODEL:])
    merged = jax.nn.sigmoid(ga_ref[...]) * attn + jax.nn.sigmoid(gs_ref[...]) * ssm_out
    x2 = x1_ref[...] + jnp.dot(merged.astype(BF16), wout_ref[...], preferred_element_type=F32)
    x3 = _swiglu_half_step(x2, n3_ref[...], wg_ref, wu_ref, wd_ref)
    o_ref[...] = _rms(x3, nf_ref[...])


def _post(x1, attn, y, ga, gs, wup, wglu, wout, n3, wg, wu, wd, nf):
    tokens = x1.shape[0]
    tm = TOKEN_TILE
    row = lambda w: pl.BlockSpec((tm, w), lambda i: (i, 0))
    consts = (wup, wglu, wout, n3, wg, wu, wd, nf)
    return pl.pallas_call(
        _post_kernel,
        grid=(tokens // tm,),
        in_specs=[row(D_MODEL), row(ATTN_WIDTH), row(SSM_WIDTH), row(D_MODEL), row(D_MODEL)]
        + [_const_spec(c.shape) for c in consts],
        out_specs=row(D_MODEL),
        out_shape=jax.ShapeDtypeStruct((tokens, D_MODEL), F32),
        compiler_params=pltpu.CompilerParams(
            dimension_semantics=("arbitrary",), vmem_limit_bytes=VMEM_LIMIT_BYTES),
        name="post",
    )(x1, attn, y, ga, gs, *consts)


def kernel(x, ffn1_norm, ffn1_w_gate, ffn1_w_up, ffn1_w_down, mix_norm, w_in, w_attn_up, ssm_lambda_re, ssm_lambda_im, ssm_log_dt, ssm_b_re, ssm_b_im, ssm_c_re, ssm_c_im, ssm_d, w_ssm_glu, w_out, ffn2_norm, ffn2_w_gate, ffn2_w_up, ffn2_w_down, final_norm):
    B, S, D = x.shape
    depth = ffn1_norm.shape[0]
    assert depth == 1, "final norm is fused into the single layer's last stage"
    tokens = B * S
    slopes = jnp.asarray(2.0 ** (-8.0 * np.arange(1, N_HEADS + 1) / N_HEADS), dtype=F32)
    bf = lambda w: w.astype(BF16)
    xt = x.reshape(tokens, D)
    for l in range(depth):
        x1, q, k, v, u, ga, gs = _ffn1_proj(
            xt, ffn1_norm[l][None], bf(ffn1_w_gate[l]), bf(ffn1_w_up[l]), bf(ffn1_w_down[l]),
            mix_norm[l][None], bf(w_in[l]))
        attn = _moba_attention(q, k, v, slopes, B)
        mats = _ssm_matrices(ssm_lambda_re[l], ssm_lambda_im[l], ssm_log_dt[l], ssm_b_re[l],
                             ssm_b_im[l], ssm_c_re[l], ssm_c_im[l], ssm_d[l])
        y = _s5_ssm(u, B, mats)
        xt = _post(x1, attn, y, ga, gs, bf(w_attn_up[l]), bf(w_ssm_glu[l]), bf(w_out[l]),
                   ffn2_norm[l][None], bf(ffn2_w_gate[l]), bf(ffn2_w_up[l]), bf(ffn2_w_down[l]),
                   final_norm[None])
    return xt.reshape(B, S, D)
```

```python
import functools
import math

import numpy as np
import jax
import jax.numpy as jnp
from jax import lax
from jax.experimental import pallas as pl
from jax.experimental.pallas import tpu as pltpu

F32 = jnp.float32
BF16 = jnp.bfloat16
HIGHEST = lax.Precision.HIGHEST

D_MODEL = 1024
N_HEADS = 8
HEAD_DIM = 64
ATTN_WIDTH = N_HEADS * HEAD_DIM
MOBA_BLOCK = 256
MOBA_TOPK = 3
SSM_GROUP = 16
SSM_GROUPS = 32
SSM_WIDTH = SSM_GROUP * SSM_GROUPS
SSM_STATE = 64
D_FF = 2816
EPS = 1e-6
LOG2_E = math.log2(math.e)

VMEM_LIMIT_BYTES = 56 * 1024 * 1024
TOKEN_TILE = MOBA_BLOCK
HEADS_PER_STEP = 8
SOFTMAX_ROWS = 32
MXU_DEPTH = 256
BF16_ROWS = 16
V_ROWS = HEAD_DIM + BF16_ROWS
MASKED = -1e30
LANES = 128
SSM_CHUNK = 8
SSM_LANE_GROUPS = LANES // SSM_GROUP
SSM_ROW_TILE = 4096
SCAN_LANES = 512


def _const_spec(shape):
    nd = len(shape)
    return pl.BlockSpec(shape, lambda *_: (0,) * nd, pipeline_mode=pl.Buffered(1))


def _rms(x, g):
    return x * lax.rsqrt(jnp.mean(x * x, axis=-1, keepdims=True) + EPS) * g


def _swiglu_half_step(x, g_norm, wg_ref, wu_ref, wd_ref):
    h = _rms(x, g_norm).astype(BF16)
    gate = jnp.dot(h, wg_ref[...], preferred_element_type=F32)
    up = jnp.dot(h, wu_ref[...], preferred_element_type=F32)
    act = (gate * jax.nn.sigmoid(gate) * up).astype(BF16)
    return x + 0.5 * jnp.dot(act, wd_ref[...], preferred_element_type=F32)


def _ffn1_proj_kernel(x_ref, n1_ref, wg_ref, wu_ref, wd_ref, n2_ref, win_ref,
                      x1_ref, q_ref, k_ref, v_ref, u_ref, ga_ref, gs_ref):
    x1 = _swiglu_half_step(x_ref[...], n1_ref[...], wg_ref, wu_ref, wd_ref)
    x1_ref[...] = x1
    h = _rms(x1, n2_ref[...]).astype(BF16)
    col = 0
    for ref in (q_ref, k_ref, v_ref, u_ref, ga_ref, gs_ref):
        width = ref.shape[-1]
        ref[...] = jnp.dot(h, win_ref[:, col:col + width],
                           preferred_element_type=F32).astype(ref.dtype)
        col += width


def _ffn1_proj(x, n1, wg, wu, wd, n2, win):
    tokens = x.shape[0]
    tm = TOKEN_TILE
    row = lambda w: pl.BlockSpec((tm, w), lambda i: (i, 0))
    widths = (ATTN_WIDTH, ATTN_WIDTH, ATTN_WIDTH, SSM_WIDTH, D_MODEL, D_MODEL)
    dtypes = (F32, F32, BF16, F32, F32, F32)
    return pl.pallas_call(
        _ffn1_proj_kernel,
        grid=(tokens // tm,),
        in_specs=[row(D_MODEL), _const_spec(n1.shape), _const_spec(wg.shape),
                  _const_spec(wu.shape), _const_spec(wd.shape), _const_spec(n2.shape),
                  _const_spec(win.shape)],
        out_specs=[row(D_MODEL)] + [row(w) for w in widths],
        out_shape=[jax.ShapeDtypeStruct((tokens, D_MODEL), F32)]
        + [jax.ShapeDtypeStruct((tokens, w), dt) for w, dt in zip(widths, dtypes)],
        compiler_params=pltpu.CompilerParams(
            dimension_semantics=("arbitrary",), vmem_limit_bytes=VMEM_LIMIT_BYTES),
        name="ffn1_proj",
    )(x, n1, wg, wu, wd, n2, win)


def _split3(x):
    hi = x.astype(BF16).astype(F32)
    mid = (x - hi).astype(BF16).astype(F32)
    return hi, mid, x - hi - mid


def _moba_kernel(slopes_ref, q_ref, k_ref, v_ref, o_ref, kms_ref, kaug_ref, qa_ref,
                 vt_ref, t_a, t_b, p_a, p_b):
    hg = pl.program_id(1)
    i = pl.program_id(2)
    blk = MOBA_BLOCK
    n_blocks = k_ref.shape[0]
    n_heads = HEADS_PER_STEP
    pair_w = MXU_DEPTH // 2
    pos_col = n_blocks
    neg_inf = -jnp.inf
    slopes = [slopes_ref[hg * n_heads + hh] * LOG2_E for hh in range(n_heads)]
    v_rows = lambda hh: slice(hh * V_ROWS, (hh + 1) * V_ROWS)
    aug_id = lax.broadcasted_iota(jnp.int32, (BF16_ROWS, blk), 0)
    pair_row_head = lax.broadcasted_iota(jnp.int32, (pair_w, blk), 0) // HEAD_DIM
    aug_col = lax.broadcasted_iota(jnp.int32, kaug_ref.shape, 1)

    @pl.when(i == 0)
    def _():
        kmean = jnp.mean(k_ref[...], axis=1)
        lane_head = lax.broadcasted_iota(jnp.int32, kmean.shape, 1) // HEAD_DIM
        km = jnp.concatenate([jnp.where(lane_head == hh, kmean, 0.0) for hh in range(n_heads)],
                             axis=0)
        km_hi = km.astype(BF16)
        kms_ref[...] = jnp.concatenate([km_hi, (km - km_hi.astype(F32)).astype(BF16)], axis=0)
        key = lax.broadcasted_iota(jnp.int32, kaug_ref.shape, 0).astype(F32)
        kaug_ref[...] = jnp.where((aug_col >= pos_col) & (aug_col < pos_col + 3), key,
                                  0.0).astype(BF16)
        ones_row = jnp.where(aug_id == 0, 1.0, 0.0).astype(BF16)

        def transpose_values(j, _):
            vt = v_ref[j].astype(F32).T.astype(BF16)
            for hh in range(n_heads):
                vt_ref[j, hh * V_ROWS:hh * V_ROWS + HEAD_DIM, :] = vt[hh * HEAD_DIM:
                                                                      (hh + 1) * HEAD_DIM]
                vt_ref[j, hh * V_ROWS + HEAD_DIM:(hh + 1) * V_ROWS, :] = ones_row
            return _
        lax.fori_loop(0, n_blocks, transpose_values, 0)

    qt = q_ref[...].T
    blk_id = lax.broadcasted_iota(jnp.int32, (n_blocks, blk), 0).astype(F32)
    past = blk_id < i.astype(F32)

    qt_hi = qt.astype(BF16)
    qt_lo = (qt - qt_hi.astype(F32)).astype(BF16)
    gate_hi = jnp.dot(kms_ref[...], qt_hi, preferred_element_type=F32)
    gates = (gate_hi[:n_heads * n_blocks] + gate_hi[n_heads * n_blocks:]
             + jnp.dot(kms_ref[:n_heads * n_blocks, :], qt_lo, preferred_element_type=F32))

    for hh in range(n_heads):
        g = jnp.where(past, gates[hh * n_blocks:(hh + 1) * n_blocks], neg_inf)
        sel = jnp.zeros(g.shape, jnp.bool_)
        for _ in range(MOBA_TOPK):
            top = jnp.max(g, axis=0, keepdims=True)
            first = jnp.min(jnp.where(g == top, blk_id, float(n_blocks)), axis=0, keepdims=True)
            pick = (blk_id == first) & (top > neg_inf)
            sel = sel | pick
            g = jnp.where(pick, neg_inf, g)
        pair = hh * HEAD_DIM // pair_w
        in_pair = slice(pair * pair_w, (pair + 1) * pair_w)
        qa_ref[hh, :pair_w, :] = (jnp.where(pair_row_head == hh - pair * (pair_w // HEAD_DIM),
                                            qt[in_pair], 0.0)
                                  * (HEAD_DIM ** -0.5 * LOG2_E)).astype(BF16)
        qa_ref[hh, pair_w:pair_w + n_blocks, :] = jnp.where(sel, 0.0, MASKED).astype(BF16)
        pieces = _split3(jnp.full((1, blk), slopes[hh], F32)) * 2
        tile = jnp.zeros((BF16_ROWS, blk), F32)
        for n, piece in enumerate(pieces):
            tile = jnp.where(aug_id == n, piece, tile)
        qa_ref[hh, pair_w + pos_col:pair_w + pos_col + BF16_ROWS, :] = tile.astype(BF16)
        qa_ref[hh, pair_w + pos_col + BF16_ROWS:, :] = jnp.zeros(
            (MXU_DEPTH - pair_w - pos_col - BF16_ROWS, blk), BF16)

    last_block = n_blocks - 1
    chunks = [slice(c, c + SOFTMAX_ROWS) for c in range(0, blk, SOFTMAX_ROWS)]

    def qk_into(t_ref, j, own):
        kj = k_ref[j].astype(BF16)
        extras = kaug_ref[...]
        if not own:
            offset = ((j - i) * blk).astype(F32)
            extras = jnp.where(aug_col == j, 1.0,
                               jnp.where((aug_col >= pos_col + 3) & (aug_col < pos_col + 6),
                                         offset, extras.astype(F32))).astype(BF16)
        for hh in range(n_heads):
            pair = hh * HEAD_DIM // pair_w
            keys = jnp.concatenate([kj[:, pair * pair_w:(pair + 1) * pair_w], extras], axis=1)
            t_ref[hh] = jnp.dot(keys, qa_ref[hh], preferred_element_type=F32)

    def pv(p_ref, j, alpha, acc):
        vj = vt_ref[j]
        return tuple(alpha[hh] * acc[hh] + jnp.dot(vj[v_rows(hh)], p_ref[hh],
                                                   preferred_element_type=F32)
                     for hh in range(n_heads))

    def softmax_into(t_ref, p_ref, hh, m, causal):
        def logits(c):
            t = t_ref[hh, c, :]
            if not causal:
                return t
            key = lax.broadcasted_iota(jnp.int32, t.shape, 0) + c.start
            query = lax.broadcasted_iota(jnp.int32, t.shape, 1)
            return jnp.where(key <= query, t, neg_inf)

        cmax = logits(chunks[0])
        for c in chunks[1:]:
            cmax = jnp.maximum(cmax, logits(c))
        cmax = jnp.max(cmax, axis=0, keepdims=True)
        m_new = cmax if m is None else jnp.maximum(m, cmax)
        for c in chunks:
            p_ref[hh, c, :] = jnp.exp2(logits(c) - m_new).astype(BF16)
        alpha = jnp.ones_like(m_new) if m is None else jnp.exp2(m - m_new)
        return alpha, m_new

    qk_into(t_b, i, True)
    qk_into(t_a, 0, False)
    own = [softmax_into(t_b, p_b, hh, None, True) for hh in range(n_heads)]
    alpha0, m0 = (tuple(x) for x in zip(*own))
    acc0 = (jnp.zeros((V_ROWS, blk), F32),) * n_heads

    def body(n, carry):
        prev_j, a_prev, m, acc = carry
        j0 = 2 * n
        j1 = jnp.minimum(j0 + 1, last_block)
        qk_into(t_b, j1, False)
        acc = pv(p_b, prev_j, a_prev, acc)
        st0 = [softmax_into(t_a, p_a, hh, m[hh], False) for hh in range(n_heads)]
        a0, m = (tuple(x) for x in zip(*st0))
        qk_into(t_a, jnp.minimum(j0 + 2, last_block), False)
        acc = pv(p_a, j0, a0, acc)
        st1 = [softmax_into(t_b, p_b, hh, m[hh], False) for hh in range(n_heads)]
        a1, m = (tuple(x) for x in zip(*st1))
        return j1, a1, m, acc

    prev_j, a_prev, _, acc = lax.fori_loop(0, (i + 1) // 2, body, (i, alpha0, m0, acc0))
    acc = pv(p_b, prev_j, a_prev, acc)
    out_t = jnp.concatenate([acc[hh][:HEAD_DIM] / acc[hh][HEAD_DIM:HEAD_DIM + 1]
                             for hh in range(n_heads)], axis=0)
    o_ref[...] = out_t.T.astype(o_ref.dtype)


def _moba_attention(q, k, v, slopes, n_batch):
    tokens, width = q.shape
    blk = MOBA_BLOCK
    n_blocks = tokens // n_batch // blk
    step_width = HEADS_PER_STEP * HEAD_DIM
    q_spec = pl.BlockSpec((blk, step_width), lambda b, g, i: (b * n_blocks + i, g))
    kv_spec = pl.BlockSpec((n_blocks, blk, step_width), lambda b, g, i: (b, 0, g),
                           pipeline_mode=pl.Buffered(1))
    by_block = lambda a: a.reshape(tokens // blk, blk, width)
    return pl.pallas_call(
        _moba_kernel,
        grid=(n_batch, width // step_width, n_blocks),
        in_specs=[pl.BlockSpec(memory_space=pltpu.SMEM), q_spec, kv_spec, kv_spec],
        out_specs=q_spec,
        out_shape=jax.ShapeDtypeStruct((tokens, width), BF16),
        scratch_shapes=[pltpu.VMEM((2 * HEADS_PER_STEP * n_blocks, step_width), BF16),
                        pltpu.VMEM((blk, MXU_DEPTH // 2), BF16),
                        pltpu.VMEM((HEADS_PER_STEP, MXU_DEPTH, blk), BF16),
                        pltpu.VMEM((n_blocks, HEADS_PER_STEP * V_ROWS, blk), BF16),
                        pltpu.VMEM((HEADS_PER_STEP, blk, blk), F32),
                        pltpu.VMEM((HEADS_PER_STEP, blk, blk), F32),
                        pltpu.VMEM((HEADS_PER_STEP, blk, blk), BF16),
                        pltpu.VMEM((HEADS_PER_STEP, blk, blk), BF16)],
        compiler_params=pltpu.CompilerParams(
            dimension_semantics=("arbitrary", "arbitrary", "arbitrary"),
            vmem_limit_bytes=VMEM_LIMIT_BYTES),
        name="moba_attn",
    )(slopes, q, by_block(k), by_block(v))


def _ssm_matrices(lam_re, lam_im, log_dt, b_re, b_im, c_re, c_im, d_skip):
    T, G, P, Hc = SSM_CHUNK, SSM_GROUPS, SSM_STATE, SSM_GROUP
    dt = jnp.exp(log_dt)[:, None]
    mag = jnp.exp(lam_re * dt)
    ar = mag * jnp.cos(lam_im * dt)
    ai = mag * jnp.sin(lam_im * dt)
    nr, ni = ar - 1.0, ai
    den = lam_re * lam_re + lam_im * lam_im
    fr = (nr * lam_re + ni * lam_im) / den
    fi = (ni * lam_re - nr * lam_im) / den
    bbr = fr[..., None] * b_re - fi[..., None] * b_im
    bbi = fr[..., None] * b_im + fi[..., None] * b_re
    pr, pi = [jnp.ones_like(ar)], [jnp.zeros_like(ar)]
    for _ in range(T):
        pr, pi = pr + [pr[-1] * ar - pi[-1] * ai], pi + [pr[-1] * ai + pi[-1] * ar]
    pw_r, pw_i = jnp.stack(pr), jnp.stack(pi)

    ca_r = c_re[None] * pw_r[:, :, None, :] - c_im[None] * pw_i[:, :, None, :]
    ca_i = c_re[None] * pw_i[:, :, None, :] + c_im[None] * pw_r[:, :, None, :]
    kern = (jnp.einsum('tgop,gpi->tgoi', ca_r[:T], bbr, precision=HIGHEST)
            - jnp.einsum('tgop,gpi->tgoi', ca_i[:T], bbi, precision=HIGHEST))
    skip = d_skip.reshape(G, Hc)[:, :, None] * jnp.eye(Hc, dtype=F32)[None]
    kern = kern.at[0].add(skip)

    GB = SSM_LANE_GROUPS
    CB = G // GB
    k_c = kern.transpose(1, 0, 3, 2).reshape(CB, GB, T, Hc, Hc).transpose(0, 2, 1, 3, 4)
    k_c = jnp.tile(k_c.reshape(CB, T, GB * Hc, Hc), (1, 1, 1, GB))
    rev_r, rev_i = pw_r[T - 1::-1], pw_i[T - 1::-1]
    n_r = rev_r[..., None] * bbr[None] - rev_i[..., None] * bbi[None]
    n_i = rev_r[..., None] * bbi[None] + rev_i[..., None] * bbr[None]
    n_c = jnp.concatenate([n_r, n_i], axis=2).transpose(0, 1, 3, 2)
    n_c = n_c.reshape(T, CB, GB, Hc, 2 * P).transpose(1, 0, 2, 3, 4).reshape(CB, T * LANES, 2 * P)
    m_c = jnp.stack([ca_r[1:], -ca_i[1:]])
    m_c = m_c.reshape(2, T, CB, GB, Hc, P).transpose(2, 0, 3, 5, 1, 4).reshape(
        CB, 2 * GB * P, T * Hc)
    a_chunk = jnp.stack([pw_r[T].reshape(1, G * P), pw_i[T].reshape(1, G * P)])
    return k_c, n_c, m_c, a_chunk


def _div(x, d):
    return x >> (d.bit_length() - 1)


def _mod(x, d):
    return x & (d - 1)


def _spread_block_diag(compact, src_of_col, row_group, col_group):
    n_rows, n_src = compact.shape
    n_cols = n_rows
    src = lax.broadcasted_iota(jnp.int32, (n_src, n_cols), 0)
    col = lax.broadcasted_iota(jnp.int32, (n_src, n_cols), 1)
    spread = jnp.where(src == src_of_col(col), 1.0, 0.0).astype(BF16)
    full = jnp.dot(compact.astype(BF16), spread, preferred_element_type=F32)
    r = lax.broadcasted_iota(jnp.int32, full.shape, 0)
    c = lax.broadcasted_iota(jnp.int32, full.shape, 1)
    return jnp.where(row_group(r) == col_group(c), full, 0.0).astype(BF16)


def _chunk_rows(u_ref):
    n = u_ref.shape[0] // SSM_CHUNK
    return jnp.concatenate([u_ref[pl.ds(s, n, stride=SSM_CHUNK), :] for s in range(SSM_CHUNK)],
                           axis=1).astype(BF16)


def _ssm_state_in_kernel(u_ref, n_c_ref, er_ref, ei_ref, n_s):
    GB, P, Hc = SSM_LANE_GROUPS, SSM_STATE, SSM_GROUP

    @pl.when(pl.program_id(1) == 0)
    def _():
        n_s[...] = _spread_block_diag(
            n_c_ref[...], lambda c: _div(c, GB * P) * P + _mod(c, P),
            lambda r: _mod(_div(r, Hc), GB), lambda c: _mod(_div(c, P), GB))

    e = jnp.dot(_chunk_rows(u_ref), n_s[...], preferred_element_type=F32)
    half = er_ref.shape[-1]
    er_ref[...] = e[:, :half]
    ei_ref[...] = e[:, half:]


def _ssm_scan_kernel(a_ref, er_ref, ei_ref, hr_ref, hi_ref, *, n_seq):
    ar = a_ref[0]
    ai = a_ref[1]
    steps = er_ref.shape[0] // n_seq
    zero = jnp.zeros_like(ar)

    def body(c, carry):
        out = []
        for b in range(n_seq):
            hr, hi = carry[2 * b], carry[2 * b + 1]
            r = b * steps + c
            hr_ref[pl.ds(r, 1), :] = hr
            hi_ref[pl.ds(r, 1), :] = hi
            er = er_ref[pl.ds(r, 1), :]
            ei = ei_ref[pl.ds(r, 1), :]
            out += [ar * hr - ai * hi + er, ar * hi + ai * hr + ei]
        return tuple(out)

    lax.fori_loop(0, steps, body, (zero,) * (2 * n_seq))


def _ssm_out_kernel(u_ref, k_c_ref, m_c_ref, hr_ref, hi_ref, y_ref, w_s, m_s):
    T, GB, P, Hc = SSM_CHUNK, SSM_LANE_GROUPS, SSM_STATE, SSM_GROUP

    @pl.when(pl.program_id(1) == 0)
    def _():
        w_s[...] = jnp.zeros_like(w_s)
        r = lax.broadcasted_iota(jnp.int32, (LANES, LANES), 0)
        c = lax.broadcasted_iota(jnp.int32, (LANES, LANES), 1)
        same_group = _div(r, Hc) == _div(c, Hc)
        for tau in range(T):
            blk = jnp.where(same_group, k_c_ref[tau], 0.0).astype(BF16)
            for s in range(T - tau):
                w_s[s * LANES:(s + 1) * LANES, (s + tau) * LANES:(s + tau + 1) * LANES] = blk
        m_s[...] = _spread_block_diag(
            m_c_ref[...], lambda c: _div(c, LANES) * Hc + _mod(c, Hc),
            lambda r: _mod(_div(r, P), GB), lambda c: _mod(_div(c, Hc), GB))

    h = jnp.concatenate([hr_ref[...], hi_ref[...]], axis=1).astype(BF16)
    y = (jnp.dot(_chunk_rows(u_ref), w_s[...], preferred_element_type=F32)
         + jnp.dot(h, m_s[...], preferred_element_type=F32))
    n = y.shape[0]
    for t in range(T):
        y_ref[pl.ds(t, n, stride=T), :] = y[:, t * LANES:(t + 1) * LANES]


def _s5_ssm(u, n_batch, ops):
    k_c, n_c, m_c, a_chunk = ops
    tokens = u.shape[0]
    T, P = SSM_CHUNK, SSM_STATE
    col_blocks = SSM_WIDTH // LANES
    rt = SSM_ROW_TILE
    chunks = tokens // T
    ct = rt // T
    op_dim = T * LANES
    half = SSM_LANE_GROUPS * P
    n_state = SSM_GROUPS * P
    grid = (col_blocks, tokens // rt)
    u_spec = pl.BlockSpec((rt, LANES), lambda cb, r: (r, cb))
    st_spec = pl.BlockSpec((ct, half), lambda cb, r: (r, cb))
    cparams = pltpu.CompilerParams(dimension_semantics=("arbitrary", "arbitrary"),
                                   vmem_limit_bytes=VMEM_LIMIT_BYTES)

    er, ei = pl.pallas_call(
        _ssm_state_in_kernel,
        grid=grid,
        in_specs=[u_spec, pl.BlockSpec((None, op_dim, LANES), lambda cb, r: (cb, 0, 0))],
        out_specs=[st_spec, st_spec],
        out_shape=[jax.ShapeDtypeStruct((chunks, n_state), F32)] * 2,
        scratch_shapes=[pltpu.VMEM((op_dim, op_dim), BF16)],
        compiler_params=cparams, name="ssm_state_in",
    )(u, n_c)

    lane_spec = pl.BlockSpec((chunks, SCAN_LANES), lambda q: (0, q))
    hr, hi = pl.pallas_call(
        functools.partial(_ssm_scan_kernel, n_seq=n_batch),
        grid=(n_state // SCAN_LANES,),
        in_specs=[pl.BlockSpec((2, 1, SCAN_LANES), lambda q: (0, 0, q)), lane_spec, lane_spec],
        out_specs=[lane_spec, lane_spec],
        out_shape=[jax.ShapeDtypeStruct((chunks, n_state), F32)] * 2,
        compiler_params=pltpu.CompilerParams(dimension_semantics=("arbitrary",),
                                             vmem_limit_bytes=VMEM_LIMIT_BYTES),
        name="ssm_scan",
    )(a_chunk, er, ei)

    return pl.pallas_call(
        _ssm_out_kernel,
        grid=grid,
        in_specs=[u_spec,
                  pl.BlockSpec((None, T, LANES, LANES), lambda cb, r: (cb, 0, 0, 0)),
                  pl.BlockSpec((None, op_dim, LANES), lambda cb, r: (cb, 0, 0)),
                  st_spec, st_spec],
        out_specs=u_spec,
        out_shape=jax.ShapeDtypeStruct((tokens, SSM_WIDTH), F32),
        scratch_shapes=[pltpu.VMEM((op_dim, op_dim), BF16), pltpu.VMEM((op_dim, op_dim), BF16)],
        compiler_params=cparams, name="ssm_out",
    )(u, k_c, m_c, hr, hi)


def _post_kernel(x1_ref, attn_ref, y_ref, ga_ref, gs_ref, wup_ref, wglu_ref, wout_ref,
                 n3_ref, wg_ref, wu_ref, wd_ref, nf_ref, o_ref):
    attn = jnp.dot(attn_ref[...], wup_ref[...], preferred_element_type=F32)
    y = jax.nn.gelu(y_ref[...], approximate=True).astype(BF16)
    glu = jnp.dot(y, wglu_ref[...], preferred_element_type=F32)
    ssm_out = glu[:, :D_MODEL] * jax.nn.sigmoid(glu[:, D_MODEL:])
    merged = jax.nn.sigmoid(ga_ref[...]) * attn + jax.nn.sigmoid(gs_ref[...]) * ssm_out
    x2 = x1_ref[...] + jnp.dot(merged.astype(BF16), wout_ref[...], preferred_element_type=F32)
    x3 = _swiglu_half_step(x2, n3_ref[...], wg_ref, wu_ref, wd_ref)
    o_ref[...] = _rms(x3, nf_ref[...])


def _post(x1, attn, y, ga, gs, wup, wglu, wout, n3, wg, wu, wd, nf):
    tokens = x1.shape[0]
    tm = TOKEN_TILE
    row = lambda w: pl.BlockSpec((tm, w), lambda i: (i, 0))
    consts = (wup, wglu, wout, n3, wg, wu, wd, nf)
    return pl.pallas_call(
        _post_kernel,
        grid=(tokens // tm,),
        in_specs=[row(D_MODEL), row(ATTN_WIDTH), row(SSM_WIDTH), row(D_MODEL), row(D_MODEL)]
        + [_const_spec(c.shape) for c in consts],
        out_specs=row(D_MODEL),
        out_shape=jax.ShapeDtypeStruct((tokens, D_MODEL), F32),
        compiler_params=pltpu.CompilerParams(
            dimension_semantics=("arbitrary",), vmem_limit_bytes=VMEM_LIMIT_BYTES),
        name="post",
    )(x1, attn, y, ga, gs, *consts)


def kernel(x, ffn1_norm, ffn1_w_gate, ffn1_w_up, ffn1_w_down, mix_norm, w_in, w_attn_up, ssm_lambda_re, ssm_lambda_im, ssm_log_dt, ssm_b_re, ssm_b_im, ssm_c_re, ssm_c_im, ssm_d, w_ssm_glu, w_out, ffn2_norm, ffn2_w_gate, ffn2_w_up, ffn2_w_down, final_norm):
    B, S, D = x.shape
    depth = ffn1_norm.shape[0]
    assert depth == 1, "final norm is fused into the single layer's last stage"
    tokens = B * S
    slopes = jnp.asarray(2.0 ** (-8.0 * np.arange(1, N_HEADS + 1) / N_HEADS), dtype=F32)
    bf = lambda w: w.astype(BF16)
    xt = x.reshape(tokens, D)
    for l in range(depth):
        x1, q, k, v, u, ga, gs = _ffn1_proj(
            xt, ffn1_norm[l][None], bf(ffn1_w_gate[l]), bf(ffn1_w_up[l]), bf(ffn1_w_down[l]),
            mix_norm[l][None], bf(w_in[l]))
        attn = _moba_attention(q, k, v, slopes, B)
        mats = _ssm_matrices(ssm_lambda_re[l], ssm_lambda_im[l], ssm_log_dt[l], ssm_b_re[l],
                             ssm_b_im[l], ssm_c_re[l], ssm_c_im[l], ssm_d[l])
        y = _s5_ssm(u, B, mats)
        xt = _post(x1, attn, y, ga, gs, bf(w_attn_up[l]), bf(w_ssm_glu[l]), bf(w_out[l]),
                   ffn2_norm[l][None], bf(ffn2_w_gate[l]), bf(ffn2_w_up[l]), bf(ffn2_w_down[l]),
                   final_norm[None])
    return xt.reshape(B, S, D)
```

```python
import functools
import math

import numpy as np
import jax
import jax.numpy as jnp
from jax import lax
from jax.experimental import pallas as pl
from jax.experimental.pallas import tpu as pltpu

F32 = jnp.float32
BF16 = jnp.bfloat16
HIGHEST = lax.Precision.HIGHEST

D_MODEL = 1024
N_HEADS = 8
HEAD_DIM = 64
ATTN_WIDTH = N_HEADS * HEAD_DIM
MOBA_BLOCK = 256
MOBA_TOPK = 3
SSM_GROUP = 16
SSM_GROUPS = 32
SSM_WIDTH = SSM_GROUP * SSM_GROUPS
SSM_STATE = 64
D_FF = 2816
EPS = 1e-6
LOG2_E = math.log2(math.e)

VMEM_LIMIT_BYTES = 56 * 1024 * 1024
TOKEN_TILE = MOBA_BLOCK
HEADS_PER_STEP = 8
SOFTMAX_ROWS = 32
MXU_DEPTH = 256
BF16_ROWS = 16
V_ROWS = HEAD_DIM + BF16_ROWS
MASKED = -1e30
LANES = 128
SSM_CHUNK = 8
SSM_LANE_GROUPS = LANES // SSM_GROUP
SSM_ROW_TILE = 4096
SCAN_LANES = 512


def _const_spec(shape):
    nd = len(shape)
    return pl.BlockSpec(shape, lambda *_: (0,) * nd, pipeline_mode=pl.Buffered(1))


def _rms(x, g):
    return x * lax.rsqrt(jnp.mean(x * x, axis=-1, keepdims=True) + EPS) * g


def _swiglu_half_step(x, g_norm, wg_ref, wu_ref, wd_ref):
    h = _rms(x, g_norm).astype(BF16)
    gate = jnp.dot(h, wg_ref[...], preferred_element_type=F32)
    up = jnp.dot(h, wu_ref[...], preferred_element_type=F32)
    act = (gate * jax.nn.sigmoid(gate) * up).astype(BF16)
    return x + 0.5 * jnp.dot(act, wd_ref[...], preferred_element_type=F32)


def _ffn1_proj_kernel(x_ref, n1_ref, wg_ref, wu_ref, wd_ref, n2_ref, win_ref,
                      x1_ref, q_ref, k_ref, v_ref, u_ref, ga_ref, gs_ref):
    x1 = _swiglu_half_step(x_ref[...], n1_ref[...], wg_ref, wu_ref, wd_ref)
    x1_ref[...] = x1
    h = _rms(x1, n2_ref[...]).astype(BF16)
    col = 0
    for ref in (q_ref, k_ref, v_ref, u_ref, ga_ref, gs_ref):
        width = ref.shape[-1]
        ref[...] = jnp.dot(h, win_ref[:, col:col + width],
                           preferred_element_type=F32).astype(ref.dtype)
        col += width


def _ffn1_proj(x, n1, wg, wu, wd, n2, win):
    tokens = x.shape[0]
    tm = TOKEN_TILE
    row = lambda w: pl.BlockSpec((tm, w), lambda i: (i, 0))
    widths = (ATTN_WIDTH, ATTN_WIDTH, ATTN_WIDTH, SSM_WIDTH, D_MODEL, D_MODEL)
    dtypes = (F32, F32, BF16, F32, F32, F32)
    return pl.pallas_call(
        _ffn1_proj_kernel,
        grid=(tokens // tm,),
        in_specs=[row(D_MODEL), _const_spec(n1.shape), _const_spec(wg.shape),
                  _const_spec(wu.shape), _const_spec(wd.shape), _const_spec(n2.shape),
                  _const_spec(win.shape)],
        out_specs=[row(D_MODEL)] + [row(w) for w in widths],
        out_shape=[jax.ShapeDtypeStruct((tokens, D_MODEL), F32)]
        + [jax.ShapeDtypeStruct((tokens, w), dt) for w, dt in zip(widths, dtypes)],
        compiler_params=pltpu.CompilerParams(
            dimension_semantics=("arbitrary",), vmem_limit_bytes=VMEM_LIMIT_BYTES),
        name="ffn1_proj",
    )(x, n1, wg, wu, wd, n2, win)


def _split3(x):
    hi = x.astype(BF16).astype(F32)
    mid = (x - hi).astype(BF16).astype(F32)
    return hi, mid, x - hi - mid


def _moba_kernel(slopes_ref, q_ref, k_ref, v_ref, o_ref, kms_ref, kaug_ref, qa_ref,
                 vt_ref, t_a, t_b, p_a, p_b, acc_ref):
    hg = pl.program_id(1)
    i = pl.program_id(2)
    blk = MOBA_BLOCK
    n_blocks = k_ref.shape[0]
    n_heads = HEADS_PER_STEP
    pair_w = MXU_DEPTH // 2
    pos_col = n_blocks
    neg_inf = -jnp.inf
    slopes = [slopes_ref[hg * n_heads + hh] * LOG2_E for hh in range(n_heads)]
    v_rows = lambda hh: slice(hh * V_ROWS, (hh + 1) * V_ROWS)
    aug_id = lax.broadcasted_iota(jnp.int32, (BF16_ROWS, blk), 0)
    pair_row_head = lax.broadcasted_iota(jnp.int32, (pair_w, blk), 0) // HEAD_DIM
    aug_col = lax.broadcasted_iota(jnp.int32, kaug_ref.shape, 1)

    @pl.when(i == 0)
    def _():
        kmean = jnp.mean(k_ref[...], axis=1)
        lane_head = lax.broadcasted_iota(jnp.int32, kmean.shape, 1) // HEAD_DIM
        km = jnp.concatenate([jnp.where(lane_head == hh, kmean, 0.0) for hh in range(n_heads)],
                             axis=0)
        km_hi = km.astype(BF16)
        kms_ref[...] = jnp.concatenate([km_hi, (km - km_hi.astype(F32)).astype(BF16)], axis=0)
        key = lax.broadcasted_iota(jnp.int32, kaug_ref.shape, 0).astype(F32)
        kaug_ref[...] = jnp.where((aug_col >= pos_col) & (aug_col < pos_col + 3), key,
                                  0.0).astype(BF16)
        ones_row = jnp.where(aug_id == 0, 1.0, 0.0).astype(BF16)

        def transpose_values(j, _):
            vt = v_ref[j].astype(F32).T.astype(BF16)
            for hh in range(n_heads):
                vt_ref[j, hh * V_ROWS:hh * V_ROWS + HEAD_DIM, :] = vt[hh * HEAD_DIM:
                                                                      (hh + 1) * HEAD_DIM]
                vt_ref[j, hh * V_ROWS + HEAD_DIM:(hh + 1) * V_ROWS, :] = ones_row
            return _
        lax.fori_loop(0, n_blocks, transpose_values, 0)

    qt = q_ref[...].T
    blk_id = lax.broadcasted_iota(jnp.int32, (n_blocks, blk), 0).astype(F32)
    past = blk_id < i.astype(F32)

    qt_hi = qt.astype(BF16)
    qt_lo = (qt - qt_hi.astype(F32)).astype(BF16)
    gate_hi = jnp.dot(kms_ref[...], qt_hi, preferred_element_type=F32)
    gates = (gate_hi[:n_heads * n_blocks] + gate_hi[n_heads * n_blocks:]
             + jnp.dot(kms_ref[:n_heads * n_blocks, :], qt_lo, preferred_element_type=F32))

    for hh in range(n_heads):
        g = jnp.where(past, gates[hh * n_blocks:(hh + 1) * n_blocks], neg_inf)
        sel = jnp.zeros(g.shape, jnp.bool_)
        for _ in range(MOBA_TOPK):
            top = jnp.max(g, axis=0, keepdims=True)
            first = jnp.min(jnp.where(g == top, blk_id, float(n_blocks)), axis=0, keepdims=True)
            pick = (blk_id == first) & (top > neg_inf)
            sel = sel | pick
            g = jnp.where(pick, neg_inf, g)
        pair = hh * HEAD_DIM // pair_w
        in_pair = slice(pair * pair_w, (pair + 1) * pair_w)
        qa_ref[hh, :pair_w, :] = (jnp.where(pair_row_head == hh - pair * (pair_w // HEAD_DIM),
                                            qt[in_pair], 0.0)
                                  * (HEAD_DIM ** -0.5 * LOG2_E)).astype(BF16)
        qa_ref[hh, pair_w:pair_w + n_blocks, :] = jnp.where(sel, 0.0, MASKED).astype(BF16)
        pieces = _split3(jnp.full((1, blk), slopes[hh], F32)) * 2
        tile = jnp.zeros((BF16_ROWS, blk), F32)
        for n, piece in enumerate(pieces):
            tile = jnp.where(aug_id == n, piece, tile)
        qa_ref[hh, pair_w + pos_col:pair_w + pos_col + BF16_ROWS, :] = tile.astype(BF16)
        qa_ref[hh, pair_w + pos_col + BF16_ROWS:, :] = jnp.zeros(
            (MXU_DEPTH - pair_w - pos_col - BF16_ROWS, blk), BF16)

    last_block = n_blocks - 1
    chunks = [slice(c, c + SOFTMAX_ROWS) for c in range(0, blk, SOFTMAX_ROWS)]

    def qk_into(t_ref, j, own):
        kj = k_ref[j].astype(BF16)
        extras = kaug_ref[...]
        if not own:
            offset = ((j - i) * blk).astype(F32)
            extras = jnp.where(aug_col == j, 1.0,
                               jnp.where((aug_col >= pos_col + 3) & (aug_col < pos_col + 6),
                                         offset, extras.astype(F32))).astype(BF16)
        col_max = []
        for hh in range(n_heads):
            pair = hh * HEAD_DIM // pair_w
            keys = jnp.concatenate([kj[:, pair * pair_w:(pair + 1) * pair_w], extras], axis=1)
            t = jnp.dot(keys, qa_ref[hh], preferred_element_type=F32)
            t_ref[hh] = t
            col_max.append(None if own else jnp.max(t, axis=0, keepdims=True))
        return tuple(col_max)

    def pv(p_ref, j, alpha):
        vj = vt_ref[j]
        for hh in range(n_heads):
            acc_ref[hh] = alpha[hh] * acc_ref[hh] + jnp.dot(vj[v_rows(hh)], p_ref[hh],
                                                            preferred_element_type=F32)

    def softmax_into(t_ref, p_ref, hh, m, cmax):
        causal = cmax is None

        def logits(c):
            t = t_ref[hh, c, :]
            if not causal:
                return t
            key = lax.broadcasted_iota(jnp.int32, t.shape, 0) + c.start
            query = lax.broadcasted_iota(jnp.int32, t.shape, 1)
            return jnp.where(key <= query, t, neg_inf)

        if causal:
            cmax = logits(chunks[0])
            for c in chunks[1:]:
                cmax = jnp.maximum(cmax, logits(c))
            cmax = jnp.max(cmax, axis=0, keepdims=True)
        m_new = cmax if m is None else jnp.maximum(m, cmax)
        for c in chunks:
            p_ref[hh, c, :] = jnp.exp2(logits(c) - m_new).astype(BF16)
        alpha = jnp.ones_like(m_new) if m is None else jnp.exp2(m - m_new)
        return alpha, m_new

    qk_into(t_b, i, True)
    cmax_first = qk_into(t_a, 0, False)
    own = [softmax_into(t_b, p_b, hh, None, None) for hh in range(n_heads)]
    alpha0, m0 = (tuple(x) for x in zip(*own))
    acc_ref[...] = jnp.zeros_like(acc_ref)

    def body(n, carry):
        prev_j, a_prev, m, cmax_a = carry
        j0 = 2 * n
        j1 = jnp.minimum(j0 + 1, last_block)
        cmax_b = qk_into(t_b, j1, False)
        pv(p_b, prev_j, a_prev)
        st0 = [softmax_into(t_a, p_a, hh, m[hh], cmax_a[hh]) for hh in range(n_heads)]
        a0, m = (tuple(x) for x in zip(*st0))
        cmax_a = qk_into(t_a, jnp.minimum(j0 + 2, last_block), False)
        pv(p_a, j0, a0)
        st1 = [softmax_into(t_b, p_b, hh, m[hh], cmax_b[hh]) for hh in range(n_heads)]
        a1, m = (tuple(x) for x in zip(*st1))
        return j1, a1, m, cmax_a

    prev_j, a_prev, _, _ = lax.fori_loop(0, (i + 1) // 2, body, (i, alpha0, m0, cmax_first))
    pv(p_b, prev_j, a_prev)
    out_t = jnp.concatenate([acc_ref[hh, :HEAD_DIM, :] / acc_ref[hh, HEAD_DIM:HEAD_DIM + 1, :]
                             for hh in range(n_heads)], axis=0)
    o_ref[...] = out_t.T.astype(o_ref.dtype)


def _moba_attention(q, k, v, slopes, n_batch):
    tokens, width = q.shape
    blk = MOBA_BLOCK
    n_blocks = tokens // n_batch // blk
    step_width = HEADS_PER_STEP * HEAD_DIM
    q_spec = pl.BlockSpec((blk, step_width), lambda b, g, i: (b * n_blocks + i, g))
    kv_spec = pl.BlockSpec((n_blocks, blk, step_width), lambda b, g, i: (b, 0, g),
                           pipeline_mode=pl.Buffered(1))
    by_block = lambda a: a.reshape(tokens // blk, blk, width)
    return pl.pallas_call(
        _moba_kernel,
        grid=(n_batch, width // step_width, n_blocks),
        in_specs=[pl.BlockSpec(memory_space=pltpu.SMEM), q_spec, kv_spec, kv_spec],
        out_specs=q_spec,
        out_shape=jax.ShapeDtypeStruct((tokens, width), BF16),
        scratch_shapes=[pltpu.VMEM((2 * HEADS_PER_STEP * n_blocks, step_width), BF16),
                        pltpu.VMEM((blk, MXU_DEPTH // 2), BF16),
                        pltpu.VMEM((HEADS_PER_STEP, MXU_DEPTH, blk), BF16),
                        pltpu.VMEM((n_blocks, HEADS_PER_STEP * V_ROWS, blk), BF16),
                        pltpu.VMEM((HEADS_PER_STEP, blk, blk), F32),
                        pltpu.VMEM((HEADS_PER_STEP, blk, blk), F32),
                        pltpu.VMEM((HEADS_PER_STEP, blk, blk), BF16),
                        pltpu.VMEM((HEADS_PER_STEP, blk, blk), BF16),
                        pltpu.VMEM((HEADS_PER_STEP, V_ROWS, blk), F32)],
        compiler_params=pltpu.CompilerParams(
            dimension_semantics=("arbitrary", "arbitrary", "arbitrary"),
            vmem_limit_bytes=VMEM_LIMIT_BYTES),
        name="moba_attn",
    )(slopes, q, by_block(k), by_block(v))


def _ssm_matrices(lam_re, lam_im, log_dt, b_re, b_im, c_re, c_im, d_skip):
    T, G, P, Hc = SSM_CHUNK, SSM_GROUPS, SSM_STATE, SSM_GROUP
    dt = jnp.exp(log_dt)[:, None]
    mag = jnp.exp(lam_re * dt)
    ar = mag * jnp.cos(lam_im * dt)
    ai = mag * jnp.sin(lam_im * dt)
    nr, ni = ar - 1.0, ai
    den = lam_re * lam_re + lam_im * lam_im
    fr = (nr * lam_re + ni * lam_im) / den
    fi = (ni * lam_re - nr * lam_im) / den
    bbr = fr[..., None] * b_re - fi[..., None] * b_im
    bbi = fr[..., None] * b_im + fi[..., None] * b_re
    pr, pi = [jnp.ones_like(ar)], [jnp.zeros_like(ar)]
    for _ in range(T):
        pr, pi = pr + [pr[-1] * ar - pi[-1] * ai], pi + [pr[-1] * ai + pi[-1] * ar]
    pw_r, pw_i = jnp.stack(pr), jnp.stack(pi)

    ca_r = c_re[None] * pw_r[:, :, None, :] - c_im[None] * pw_i[:, :, None, :]
    ca_i = c_re[None] * pw_i[:, :, None, :] + c_im[None] * pw_r[:, :, None, :]
    kern = (jnp.einsum('tgop,gpi->tgoi', ca_r[:T], bbr, precision=HIGHEST)
            - jnp.einsum('tgop,gpi->tgoi', ca_i[:T], bbi, precision=HIGHEST))
    skip = d_skip.reshape(G, Hc)[:, :, None] * jnp.eye(Hc, dtype=F32)[None]
    kern = kern.at[0].add(skip)

    GB = SSM_LANE_GROUPS
    CB = G // GB
    k_c = kern.transpose(1, 0, 3, 2).reshape(CB, GB, T, Hc, Hc).transpose(0, 2, 1, 3, 4)
    k_c = jnp.tile(k_c.reshape(CB, T, GB * Hc, Hc), (1, 1, 1, GB))
    rev_r, rev_i = pw_r[T - 1::-1], pw_i[T - 1::-1]
    n_r = rev_r[..., None] * bbr[None] - rev_i[..., None] * bbi[None]
    n_i = rev_r[..., None] * bbi[None] + rev_i[..., None] * bbr[None]
    n_c = jnp.concatenate([n_r, n_i], axis=2).transpose(0, 1, 3, 2)
    n_c = n_c.reshape(T, CB, GB, Hc, 2 * P).transpose(1, 0, 2, 3, 4).reshape(CB, T * LANES, 2 * P)
    m_c = jnp.stack([ca_r[1:], -ca_i[1:]])
    m_c = m_c.reshape(2, T, CB, GB, Hc, P).transpose(2, 0, 3, 5, 1, 4).reshape(
        CB, 2 * GB * P, T * Hc)
    a_chunk = jnp.stack([pw_r[T].reshape(1, G * P), pw_i[T].reshape(1, G * P)])
    return k_c, n_c, m_c, a_chunk


def _div(x, d):
    return x >> (d.bit_length() - 1)


def _mod(x, d):
    return x & (d - 1)


def _spread_block_diag(compact, src_of_col, row_group, col_group):
    n_rows, n_src = compact.shape
    n_cols = n_rows
    src = lax.broadcasted_iota(jnp.int32, (n_src, n_cols), 0)
    col = lax.broadcasted_iota(jnp.int32, (n_src, n_cols), 1)
    spread = jnp.where(src == src_of_col(col), 1.0, 0.0).astype(BF16)
    full = jnp.dot(compact.astype(BF16), spread, preferred_element_type=F32)
    r = lax.broadcasted_iota(jnp.int32, full.shape, 0)
    c = lax.broadcasted_iota(jnp.int32, full.shape, 1)
    return jnp.where(row_group(r) == col_group(c), full, 0.0).astype(BF16)


def _chunk_rows(u_ref):
    n = u_ref.shape[0] // SSM_CHUNK
    return jnp.concatenate([u_ref[pl.ds(s, n, stride=SSM_CHUNK), :] for s in range(SSM_CHUNK)],
                           axis=1).astype(BF16)


def _ssm_state_in_kernel(u_ref, n_c_ref, er_ref, ei_ref, n_s):
    GB, P, Hc = SSM_LANE_GROUPS, SSM_STATE, SSM_GROUP

    @pl.when(pl.program_id(1) == 0)
    def _():
        n_s[...] = _spread_block_diag(
            n_c_ref[...], lambda c: _div(c, GB * P) * P + _mod(c, P),
            lambda r: _mod(_div(r, Hc), GB), lambda c: _mod(_div(c, P), GB))

    e = jnp.dot(_chunk_rows(u_ref), n_s[...], preferred_element_type=F32)
    half = er_ref.shape[-1]
    er_ref[...] = e[:, :half]
    ei_ref[...] = e[:, half:]


def _ssm_scan_kernel(a_ref, er_ref, ei_ref, hr_ref, hi_ref, *, n_seq):
    ar = a_ref[0]
    ai = a_ref[1]
    steps = er_ref.shape[0] // n_seq
    zero = jnp.zeros_like(ar)

    def body(c, carry):
        out = []
        for b in range(n_seq):
            hr, hi = carry[2 * b], carry[2 * b + 1]
            r = b * steps + c
            hr_ref[pl.ds(r, 1), :] = hr
            hi_ref[pl.ds(r, 1), :] = hi
            er = er_ref[pl.ds(r, 1), :]
            ei = ei_ref[pl.ds(r, 1), :]
            out += [ar * hr - ai * hi + er, ar * hi + ai * hr + ei]
        return tuple(out)

    lax.fori_loop(0, steps, body, (zero,) * (2 * n_seq))


def _ssm_out_kernel(u_ref, k_c_ref, m_c_ref, hr_ref, hi_ref, y_ref, w_s, m_s):
    T, GB, P, Hc = SSM_CHUNK, SSM_LANE_GROUPS, SSM_STATE, SSM_GROUP

    @pl.when(pl.program_id(1) == 0)
    def _():
        w_s[...] = jnp.zeros_like(w_s)
        r = lax.broadcasted_iota(jnp.int32, (LANES, LANES), 0)
        c = lax.broadcasted_iota(jnp.int32, (LANES, LANES), 1)
        same_group = _div(r, Hc) == _div(c, Hc)
        for tau in range(T):
            blk = jnp.where(same_group, k_c_ref[tau], 0.0).astype(BF16)
            for s in range(T - tau):
                w_s[s * LANES:(s + 1) * LANES, (s + tau) * LANES:(s + tau + 1) * LANES] = blk
        m_s[...] = _spread_block_diag(
            m_c_ref[...], lambda c: _div(c, LANES) * Hc + _mod(c, Hc),
            lambda r: _mod(_div(r, P), GB), lambda c: _mod(_div(c, Hc), GB))

    h = jnp.concatenate([hr_ref[...], hi_ref[...]], axis=1).astype(BF16)
    y = (jnp.dot(_chunk_rows(u_ref), w_s[...], preferred_element_type=F32)
         + jnp.dot(h, m_s[...], preferred_element_type=F32))
    n = y.shape[0]
    for t in range(T):
        y_ref[pl.ds(t, n, stride=T), :] = y[:, t * LANES:(t + 1) * LANES]


def _s5_ssm(u, n_batch, ops):
    k_c, n_c, m_c, a_chunk = ops
    tokens = u.shape[0]
    T, P = SSM_CHUNK, SSM_STATE
    col_blocks = SSM_WIDTH // LANES
    rt = SSM_ROW_TILE
    chunks = tokens // T
    ct = rt // T
    op_dim = T * LANES
    half = SSM_LANE_GROUPS * P
    n_state = SSM_GROUPS * P
    grid = (col_blocks, tokens // rt)
    u_spec = pl.BlockSpec((rt, LANES), lambda cb, r: (r, cb))
    st_spec = pl.BlockSpec((ct, half), lambda cb, r: (r, cb))
    cparams = pltpu.CompilerParams(dimension_semantics=("arbitrary", "arbitrary"),
                                   vmem_limit_bytes=VMEM_LIMIT_BYTES)

    er, ei = pl.pallas_call(
        _ssm_state_in_kernel,
        grid=grid,
        in_specs=[u_spec, pl.BlockSpec((None, op_dim, LANES), lambda cb, r: (cb, 0, 0))],
        out_specs=[st_spec, st_spec],
        out_shape=[jax.ShapeDtypeStruct((chunks, n_state), F32)] * 2,
        scratch_shapes=[pltpu.VMEM((op_dim, op_dim), BF16)],
        compiler_params=cparams, name="ssm_state_in",
    )(u, n_c)

    lane_spec = pl.BlockSpec((chunks, SCAN_LANES), lambda q: (0, q))
    hr, hi = pl.pallas_call(
        functools.partial(_ssm_scan_kernel, n_seq=n_batch),
        grid=(n_state // SCAN_LANES,),
        in_specs=[pl.BlockSpec((2, 1, SCAN_LANES), lambda q: (0, 0, q)), lane_spec, lane_spec],
        out_specs=[lane_spec, lane_spec],
        out_shape=[jax.ShapeDtypeStruct((chunks, n_state), F32)] * 2,
        compiler_params=pltpu.CompilerParams(dimension_semantics=("arbitrary",),
                                             vmem_limit_bytes=VMEM_LIMIT_BYTES),
        name="ssm_scan",
    )(a_chunk, er, ei)

    return pl.pallas_call(
        _ssm_out_kernel,
        grid=grid,
        in_specs=[u_spec,
                  pl.BlockSpec((None, T, LANES, LANES), lambda cb, r: (cb, 0, 0, 0)),
                  pl.BlockSpec((None, op_dim, LANES), lambda cb, r: (cb, 0, 0)),
                  st_spec, st_spec],
        out_specs=u_spec,
        out_shape=jax.ShapeDtypeStruct((tokens, SSM_WIDTH), F32),
        scratch_shapes=[pltpu.VMEM((op_dim, op_dim), BF16), pltpu.VMEM((op_dim, op_dim), BF16)],
        compiler_params=cparams, name="ssm_out",
    )(u, k_c, m_c, hr, hi)


def _post_kernel(x1_ref, attn_ref, y_ref, ga_ref, gs_ref, wup_ref, wglu_ref, wout_ref,
                 n3_ref, wg_ref, wu_ref, wd_ref, nf_ref, o_ref):
    attn = jnp.dot(attn_ref[...], wup_ref[...], preferred_element_type=F32)
    y = jax.nn.gelu(y_ref[...], approximate=True).astype(BF16)
    glu = jnp.dot(y, wglu_ref[...], preferred_element_type=F32)
    ssm_out = glu[:, :D_MODEL] * jax.nn.sigmoid(glu[:, D_MODEL:])
    merged = jax.nn.sigmoid(ga_ref[...]) * attn + jax.nn.sigmoid(gs_ref[...]) * ssm_out
    x2 = x1_ref[...] + jnp.dot(merged.astype(BF16), wout_ref[...], preferred_element_type=F32)
    x3 = _swiglu_half_step(x2, n3_ref[...], wg_ref, wu_ref, wd_ref)
    o_ref[...] = _rms(x3, nf_ref[...])


def _post(x1, attn, y, ga, gs, wup, wglu, wout, n3, wg, wu, wd, nf):
    tokens = x1.shape[0]
    tm = TOKEN_TILE
    row = lambda w: pl.BlockSpec((tm, w), lambda i: (i, 0))
    consts = (wup, wglu, wout, n3, wg, wu, wd, nf)
    return pl.pallas_call(
        _post_kernel,
        grid=(tokens // tm,),
        in_specs=[row(D_MODEL), row(ATTN_WIDTH), row(SSM_WIDTH), row(D_MODEL), row(D_MODEL)]
        + [_const_spec(c.shape) for c in consts],
        out_specs=row(D_MODEL),
        out_shape=jax.ShapeDtypeStruct((tokens, D_MODEL), F32),
        compiler_params=pltpu.CompilerParams(
            dimension_semantics=("arbitrary",), vmem_limit_bytes=VMEM_LIMIT_BYTES),
        name="post",
    )(x1, attn, y, ga, gs, *consts)


def kernel(x, ffn1_norm, ffn1_w_gate, ffn1_w_up, ffn1_w_down, mix_norm, w_in, w_attn_up, ssm_lambda_re, ssm_lambda_im, ssm_log_dt, ssm_b_re, ssm_b_im, ssm_c_re, ssm_c_im, ssm_d, w_ssm_glu, w_out, ffn2_norm, ffn2_w_gate, ffn2_w_up, ffn2_w_down, final_norm):
    B, S, D = x.shape
    depth = ffn1_norm.shape[0]
    assert depth == 1, "final norm is fused into the single layer's last stage"
    tokens = B * S
    slopes = jnp.asarray(2.0 ** (-8.0 * np.arange(1, N_HEADS + 1) / N_HEADS), dtype=F32)
    bf = lambda w: w.astype(BF16)
    xt = x.reshape(tokens, D)
    for l in range(depth):
        x1, q, k, v, u, ga, gs = _ffn1_proj(
            xt, ffn1_norm[l][None], bf(ffn1_w_gate[l]), bf(ffn1_w_up[l]), bf(ffn1_w_down[l]),
            mix_norm[l][None], bf(w_in[l]))
        attn = _moba_attention(q, k, v, slopes, B)
        mats = _ssm_matrices(ssm_lambda_re[l], ssm_lambda_im[l], ssm_log_dt[l], ssm_b_re[l],
                             ssm_b_im[l], ssm_c_re[l], ssm_c_im[l], ssm_d[l])
        y = _s5_ssm(u, B, mats)
        xt = _post(x1, attn, y, ga, gs, bf(w_attn_up[l]), bf(w_ssm_glu[l]), bf(w_out[l]),
                   ffn2_norm[l][None], bf(ffn2_w_gate[l]), bf(ffn2_w_up[l]), bf(ffn2_w_down[l]),
                   final_norm[None])
    return xt.reshape(B, S, D)
```

```python
import functools
import math

import numpy as np
import jax
import jax.numpy as jnp
from jax import lax
from jax.experimental import pallas as pl
from jax.experimental.pallas import tpu as pltpu

F32 = jnp.float32
BF16 = jnp.bfloat16
HIGHEST = lax.Precision.HIGHEST

D_MODEL = 1024
N_HEADS = 8
HEAD_DIM = 64
ATTN_WIDTH = N_HEADS * HEAD_DIM
MOBA_BLOCK = 256
MOBA_TOPK = 3
SSM_GROUP = 16
SSM_GROUPS = 32
SSM_WIDTH = SSM_GROUP * SSM_GROUPS
SSM_STATE = 64
D_FF = 2816
EPS = 1e-6
LOG2_E = math.log2(math.e)

VMEM_LIMIT_BYTES = 56 * 1024 * 1024
TOKEN_TILE = MOBA_BLOCK
HEADS_PER_STEP = 8
SOFTMAX_ROWS = 32
MXU_DEPTH = 256
BF16_ROWS = 16
V_ROWS = HEAD_DIM + BF16_ROWS
MASKED = -1e30
LANES = 128
SSM_CHUNK = 8
SSM_LANE_GROUPS = LANES // SSM_GROUP
SSM_ROW_TILE = 4096
SCAN_LANES = 512


def _const_spec(shape):
    nd = len(shape)
    return pl.BlockSpec(shape, lambda *_: (0,) * nd, pipeline_mode=pl.Buffered(1))


def _rms(x, g):
    return x * lax.rsqrt(jnp.mean(x * x, axis=-1, keepdims=True) + EPS) * g


def _swiglu_half_step(x, g_norm, wg_ref, wu_ref, wd_ref):
    h = _rms(x, g_norm).astype(BF16)
    gate = jnp.dot(h, wg_ref[...], preferred_element_type=F32)
    up = jnp.dot(h, wu_ref[...], preferred_element_type=F32)
    act = (gate * jax.nn.sigmoid(gate) * up).astype(BF16)
    return x + 0.5 * jnp.dot(act, wd_ref[...], preferred_element_type=F32)


def _ffn1_proj_kernel(x_ref, n1_ref, wg_ref, wu_ref, wd_ref, n2_ref, win_ref,
                      x1_ref, q_ref, k_ref, v_ref, u_ref, ga_ref, gs_ref):
    x1 = _swiglu_half_step(x_ref[...], n1_ref[...], wg_ref, wu_ref, wd_ref)
    x1_ref[...] = x1
    h = _rms(x1, n2_ref[...]).astype(BF16)
    col = 0
    for ref in (q_ref, k_ref, v_ref, u_ref, ga_ref, gs_ref):
        width = ref.shape[-1]
        ref[...] = jnp.dot(h, win_ref[:, col:col + width],
                           preferred_element_type=F32).astype(ref.dtype)
        col += width


def _ffn1_proj(x, n1, wg, wu, wd, n2, win):
    tokens = x.shape[0]
    tm = TOKEN_TILE
    row = lambda w: pl.BlockSpec((tm, w), lambda i: (i, 0))
    widths = (ATTN_WIDTH, ATTN_WIDTH, ATTN_WIDTH, SSM_WIDTH, D_MODEL, D_MODEL)
    dtypes = (F32, F32, BF16, F32, F32, F32)
    return pl.pallas_call(
        _ffn1_proj_kernel,
        grid=(tokens // tm,),
        in_specs=[row(D_MODEL), _const_spec(n1.shape), _const_spec(wg.shape),
                  _const_spec(wu.shape), _const_spec(wd.shape), _const_spec(n2.shape),
                  _const_spec(win.shape)],
        out_specs=[row(D_MODEL)] + [row(w) for w in widths],
        out_shape=[jax.ShapeDtypeStruct((tokens, D_MODEL), F32)]
        + [jax.ShapeDtypeStruct((tokens, w), dt) for w, dt in zip(widths, dtypes)],
        compiler_params=pltpu.CompilerParams(
            dimension_semantics=("arbitrary",), vmem_limit_bytes=VMEM_LIMIT_BYTES),
        name="ffn1_proj",
    )(x, n1, wg, wu, wd, n2, win)


def _split3(x):
    hi = x.astype(BF16).astype(F32)
    mid = (x - hi).astype(BF16).astype(F32)
    return hi, mid, x - hi - mid


def _moba_kernel(slopes_ref, q_ref, k_ref, v_ref, o_ref, kms_ref, kaug_ref, qa_ref,
                 vt_ref, t_a, t_b, p_a, p_b, acc_ref):
    hg = pl.program_id(1)
    i = pl.program_id(2)
    blk = MOBA_BLOCK
    n_blocks = k_ref.shape[0]
    n_heads = HEADS_PER_STEP
    pair_w = MXU_DEPTH // 2
    pos_col = n_blocks
    neg_inf = -jnp.inf
    slopes = [slopes_ref[hg * n_heads + hh] * LOG2_E for hh in range(n_heads)]
    v_rows = lambda hh: slice(hh * V_ROWS, (hh + 1) * V_ROWS)
    aug_id = lax.broadcasted_iota(jnp.int32, (BF16_ROWS, blk), 0)
    pair_row_head = lax.broadcasted_iota(jnp.int32, (pair_w, blk), 0) // HEAD_DIM
    aug_col = lax.broadcasted_iota(jnp.int32, kaug_ref.shape, 1)

    @pl.when(i == 0)
    def _():
        kmean = jnp.mean(k_ref[...], axis=1)
        lane_head = lax.broadcasted_iota(jnp.int32, kmean.shape, 1) // HEAD_DIM
        km = jnp.concatenate([jnp.where(lane_head == hh, kmean, 0.0) for hh in range(n_heads)],
                             axis=0)
        km_hi = km.astype(BF16)
        kms_ref[...] = jnp.concatenate([km_hi, (km - km_hi.astype(F32)).astype(BF16)], axis=0)
        key = lax.broadcasted_iota(jnp.int32, kaug_ref.shape, 0).astype(F32)
        kaug_ref[...] = jnp.where((aug_col >= pos_col) & (aug_col < pos_col + 3), key,
                                  0.0).astype(BF16)
        ones_row = jnp.where(aug_id == 0, 1.0, 0.0).astype(BF16)

        def transpose_values(j, _):
            vt = v_ref[j].astype(F32).T.astype(BF16)
            for hh in range(n_heads):
                vt_ref[j, hh * V_ROWS:hh * V_ROWS + HEAD_DIM, :] = vt[hh * HEAD_DIM:
                                                                      (hh + 1) * HEAD_DIM]
                vt_ref[j, hh * V_ROWS + HEAD_DIM:(hh + 1) * V_ROWS, :] = ones_row
            return _
        lax.fori_loop(0, n_blocks, transpose_values, 0)

    qt = q_ref[...].T

    for hh in range(n_heads):
        pair = hh * HEAD_DIM // pair_w
        in_pair = slice(pair * pair_w, (pair + 1) * pair_w)
        qa_ref[hh, :pair_w, :] = (jnp.where(pair_row_head == hh - pair * (pair_w // HEAD_DIM),
                                            qt[in_pair], 0.0)
                                  * (HEAD_DIM ** -0.5 * LOG2_E)).astype(BF16)
        qa_ref[hh, pair_w:pair_w + n_blocks, :] = jnp.zeros((n_blocks, blk), BF16)
        pieces = _split3(jnp.full((1, blk), slopes[hh], F32)) * 2
        tile = jnp.zeros((BF16_ROWS, blk), F32)
        for n, piece in enumerate(pieces):
            tile = jnp.where(aug_id == n, piece, tile)
        qa_ref[hh, pair_w + pos_col:pair_w + pos_col + BF16_ROWS, :] = tile.astype(BF16)
        qa_ref[hh, pair_w + pos_col + BF16_ROWS:, :] = jnp.zeros(
            (MXU_DEPTH - pair_w - pos_col - BF16_ROWS, blk), BF16)

    def block_gates():
        qt_hi = qt.astype(BF16)
        qt_lo = (qt - qt_hi.astype(F32)).astype(BF16)
        gate_hi = jnp.dot(kms_ref[...], qt_hi, preferred_element_type=F32)
        return (gate_hi[:n_heads * n_blocks] + gate_hi[n_heads * n_blocks:]
                + jnp.dot(kms_ref[:n_heads * n_blocks, :], qt_lo, preferred_element_type=F32))

    def select_blocks(gates):
        shape = (n_heads, n_blocks, blk)
        blk_id = lax.broadcasted_iota(jnp.int32, shape, 1).astype(F32)
        g = jnp.where(blk_id < i.astype(F32), gates.reshape(shape), neg_inf)
        sel = jnp.zeros(shape, jnp.bool_)
        for _ in range(MOBA_TOPK):
            top = jnp.max(g, axis=1, keepdims=True)
            first = jnp.min(jnp.where(g == top, blk_id, float(n_blocks)), axis=1, keepdims=True)
            pick = (blk_id == first) & (top > neg_inf)
            sel = sel | pick
            g = jnp.where(pick, neg_inf, g)
        mask_rows = jnp.where(sel, 0.0, MASKED).astype(BF16)
        for hh in range(n_heads):
            qa_ref[hh, pair_w:pair_w + n_blocks, :] = mask_rows[hh]

    last_block = n_blocks - 1
    chunks = [slice(c, c + SOFTMAX_ROWS) for c in range(0, blk, SOFTMAX_ROWS)]

    def qk_into(t_ref, j, own):
        kj = k_ref[j].astype(BF16)
        extras = kaug_ref[...]
        if not own:
            offset = ((j - i) * blk).astype(F32)
            extras = jnp.where(aug_col == j, 1.0,
                               jnp.where((aug_col >= pos_col + 3) & (aug_col < pos_col + 6),
                                         offset, extras.astype(F32))).astype(BF16)
        col_max = []
        for hh in range(n_heads):
            pair = hh * HEAD_DIM // pair_w
            keys = jnp.concatenate([kj[:, pair * pair_w:(pair + 1) * pair_w], extras], axis=1)
            t = jnp.dot(keys, qa_ref[hh], preferred_element_type=F32)
            t_ref[hh] = t
            col_max.append(None if own else jnp.max(t, axis=0, keepdims=True))
        return tuple(col_max)

    def pv(p_ref, j, alpha):
        vj = vt_ref[j]
        for hh in range(n_heads):
            acc_ref[hh] = alpha[hh] * acc_ref[hh] + jnp.dot(vj[v_rows(hh)], p_ref[hh],
                                                            preferred_element_type=F32)

    def softmax_into(t_ref, p_ref, hh, m, cmax):
        causal = cmax is None

        def logits(c):
            t = t_ref[hh, c, :]
            if not causal:
                return t
            key = lax.broadcasted_iota(jnp.int32, t.shape, 0) + c.start
            query = lax.broadcasted_iota(jnp.int32, t.shape, 1)
            return jnp.where(key <= query, t, neg_inf)

        if causal:
            cmax = logits(chunks[0])
            for c in chunks[1:]:
                cmax = jnp.maximum(cmax, logits(c))
            cmax = jnp.max(cmax, axis=0, keepdims=True)
        m_new = cmax if m is None else jnp.maximum(m, cmax)
        for c in chunks:
            p_ref[hh, c, :] = jnp.exp2(logits(c) - m_new).astype(BF16)
        alpha = jnp.ones_like(m_new) if m is None else jnp.exp2(m - m_new)
        return alpha, m_new

    gates = block_gates()
    qk_into(t_b, i, True)
    select_blocks(gates)
    cmax_first = qk_into(t_a, 0, False)
    own = [softmax_into(t_b, p_b, hh, None, None) for hh in range(n_heads)]
    alpha0, m0 = (tuple(x) for x in zip(*own))
    acc_ref[...] = jnp.zeros_like(acc_ref)

    def body(n, carry):
        prev_j, a_prev, m, cmax_a = carry
        j0 = 2 * n
        j1 = jnp.minimum(j0 + 1, last_block)
        cmax_b = qk_into(t_b, j1, False)
        pv(p_b, prev_j, a_prev)
        st0 = [softmax_into(t_a, p_a, hh, m[hh], cmax_a[hh]) for hh in range(n_heads)]
        a0, m = (tuple(x) for x in zip(*st0))
        cmax_a = qk_into(t_a, jnp.minimum(j0 + 2, last_block), False)
        pv(p_a, j0, a0)
        st1 = [softmax_into(t_b, p_b, hh, m[hh], cmax_b[hh]) for hh in range(n_heads)]
        a1, m = (tuple(x) for x in zip(*st1))
        return j1, a1, m, cmax_a

    prev_j, a_prev, _, _ = lax.fori_loop(0, (i + 1) // 2, body, (i, alpha0, m0, cmax_first))
    pv(p_b, prev_j, a_prev)
    out_t = jnp.concatenate([acc_ref[hh, :HEAD_DIM, :] / acc_ref[hh, HEAD_DIM:HEAD_DIM + 1, :]
                             for hh in range(n_heads)], axis=0)
    o_ref[...] = out_t.T.astype(o_ref.dtype)


def _moba_attention(q, k, v, slopes, n_batch):
    tokens, width = q.shape
    blk = MOBA_BLOCK
    n_blocks = tokens // n_batch // blk
    step_width = HEADS_PER_STEP * HEAD_DIM
    q_spec = pl.BlockSpec((blk, step_width), lambda b, g, i: (b * n_blocks + i, g))
    kv_spec = pl.BlockSpec((n_blocks, blk, step_width), lambda b, g, i: (b, 0, g),
                           pipeline_mode=pl.Buffered(1))
    by_block = lambda a: a.reshape(tokens // blk, blk, width)
    return pl.pallas_call(
        _moba_kernel,
        grid=(n_batch, width // step_width, n_blocks),
        in_specs=[pl.BlockSpec(memory_space=pltpu.SMEM), q_spec, kv_spec, kv_spec],
        out_specs=q_spec,
        out_shape=jax.ShapeDtypeStruct((tokens, width), BF16),
        scratch_shapes=[pltpu.VMEM((2 * HEADS_PER_STEP * n_blocks, step_width), BF16),
                        pltpu.VMEM((blk, MXU_DEPTH // 2), BF16),
                        pltpu.VMEM((HEADS_PER_STEP, MXU_DEPTH, blk), BF16),
                        pltpu.VMEM((n_blocks, HEADS_PER_STEP * V_ROWS, blk), BF16),
                        pltpu.VMEM((HEADS_PER_STEP, blk, blk), F32),
                        pltpu.VMEM((HEADS_PER_STEP, blk, blk), F32),
                        pltpu.VMEM((HEADS_PER_STEP, blk, blk), BF16),
                        pltpu.VMEM((HEADS_PER_STEP, blk, blk), BF16),
                        pltpu.VMEM((HEADS_PER_STEP, V_ROWS, blk), F32)],
        compiler_params=pltpu.CompilerParams(
            dimension_semantics=("arbitrary", "arbitrary", "arbitrary"),
            vmem_limit_bytes=VMEM_LIMIT_BYTES),
        name="moba_attn",
    )(slopes, q, by_block(k), by_block(v))


def _ssm_matrices(lam_re, lam_im, log_dt, b_re, b_im, c_re, c_im, d_skip):
    T, G, P, Hc = SSM_CHUNK, SSM_GROUPS, SSM_STATE, SSM_GROUP
    dt = jnp.exp(log_dt)[:, None]
    mag = jnp.exp(lam_re * dt)
    ar = mag * jnp.cos(lam_im * dt)
    ai = mag * jnp.sin(lam_im * dt)
    nr, ni = ar - 1.0, ai
    den = lam_re * lam_re + lam_im * lam_im
    fr = (nr * lam_re + ni * lam_im) / den
    fi = (ni * lam_re - nr * lam_im) / den
    bbr = fr[..., None] * b_re - fi[..., None] * b_im
    bbi = fr[..., None] * b_im + fi[..., None] * b_re
    pr, pi = [jnp.ones_like(ar)], [jnp.zeros_like(ar)]
    for _ in range(T):
        pr, pi = pr + [pr[-1] * ar - pi[-1] * ai], pi + [pr[-1] * ai + pi[-1] * ar]
    pw_r, pw_i = jnp.stack(pr), jnp.stack(pi)

    ca_r = c_re[None] * pw_r[:, :, None, :] - c_im[None] * pw_i[:, :, None, :]
    ca_i = c_re[None] * pw_i[:, :, None, :] + c_im[None] * pw_r[:, :, None, :]
    kern = (jnp.einsum('tgop,gpi->tgoi', ca_r[:T], bbr, precision=HIGHEST)
            - jnp.einsum('tgop,gpi->tgoi', ca_i[:T], bbi, precision=HIGHEST))
    skip = d_skip.reshape(G, Hc)[:, :, None] * jnp.eye(Hc, dtype=F32)[None]
    kern = kern.at[0].add(skip)

    GB = SSM_LANE_GROUPS
    CB = G // GB
    k_c = kern.transpose(1, 0, 3, 2).reshape(CB, GB, T, Hc, Hc).transpose(0, 2, 1, 3, 4)
    k_c = jnp.tile(k_c.reshape(CB, T, GB * Hc, Hc), (1, 1, 1, GB))
    rev_r, rev_i = pw_r[T - 1::-1], pw_i[T - 1::-1]
    n_r = rev_r[..., None] * bbr[None] - rev_i[..., None] * bbi[None]
    n_i = rev_r[..., None] * bbi[None] + rev_i[..., None] * bbr[None]
    n_c = jnp.concatenate([n_r, n_i], axis=2).transpose(0, 1, 3, 2)
    n_c = n_c.reshape(T, CB, GB, Hc, 2 * P).transpose(1, 0, 2, 3, 4).reshape(CB, T * LANES, 2 * P)
    m_c = jnp.stack([ca_r[1:], -ca_i[1:]])
    m_c = m_c.reshape(2, T, CB, GB, Hc, P).transpose(2, 0, 3, 5, 1, 4).reshape(
        CB, 2 * GB * P, T * Hc)
    a_chunk = jnp.stack([pw_r[T].reshape(1, G * P), pw_i[T].reshape(1, G * P)])
    return k_c, n_c, m_c, a_chunk


def _div(x, d):
    return x >> (d.bit_length() - 1)


def _mod(x, d):
    return x & (d - 1)


def _spread_block_diag(compact, src_of_col, row_group, col_group):
    n_rows, n_src = compact.shape
    n_cols = n_rows
    src = lax.broadcasted_iota(jnp.int32, (n_src, n_cols), 0)
    col = lax.broadcasted_iota(jnp.int32, (n_src, n_cols), 1)
    spread = jnp.where(src == src_of_col(col), 1.0, 0.0).astype(BF16)
    full = jnp.dot(compact.astype(BF16), spread, preferred_element_type=F32)
    r = lax.broadcasted_iota(jnp.int32, full.shape, 0)
    c = lax.broadcasted_iota(jnp.int32, full.shape, 1)
    return jnp.where(row_group(r) == col_group(c), full, 0.0).astype(BF16)


def _chunk_rows(u_ref):
    n = u_ref.shape[0] // SSM_CHUNK
    return jnp.concatenate([u_ref[pl.ds(s, n, stride=SSM_CHUNK), :] for s in range(SSM_CHUNK)],
                           axis=1).astype(BF16)


def _ssm_state_in_kernel(u_ref, n_c_ref, er_ref, ei_ref, n_s):
    GB, P, Hc = SSM_LANE_GROUPS, SSM_STATE, SSM_GROUP

    @pl.when(pl.program_id(1) == 0)
    def _():
        n_s[...] = _spread_block_diag(
            n_c_ref[...], lambda c: _div(c, GB * P) * P + _mod(c, P),
            lambda r: _mod(_div(r, Hc), GB), lambda c: _mod(_div(c, P), GB))

    e = jnp.dot(_chunk_rows(u_ref), n_s[...], preferred_element_type=F32)
    half = er_ref.shape[-1]
    er_ref[...] = e[:, :half]
    ei_ref[...] = e[:, half:]


def _ssm_scan_kernel(a_ref, er_ref, ei_ref, hr_ref, hi_ref, *, n_seq):
    ar = a_ref[0]
    ai = a_ref[1]
    steps = er_ref.shape[0] // n_seq
    zero = jnp.zeros_like(ar)

    def body(c, carry):
        out = []
        for b in range(n_seq):
            hr, hi = carry[2 * b], carry[2 * b + 1]
            r = b * steps + c
            hr_ref[pl.ds(r, 1), :] = hr
            hi_ref[pl.ds(r, 1), :] = hi
            er = er_ref[pl.ds(r, 1), :]
            ei = ei_ref[pl.ds(r, 1), :]
            out += [ar * hr - ai * hi + er, ar * hi + ai * hr + ei]
        return tuple(out)

    lax.fori_loop(0, steps, body, (zero,) * (2 * n_seq))


def _ssm_out_kernel(u_ref, k_c_ref, m_c_ref, hr_ref, hi_ref, y_ref, w_s, m_s):
    T, GB, P, Hc = SSM_CHUNK, SSM_LANE_GROUPS, SSM_STATE, SSM_GROUP

    @pl.when(pl.program_id(1) == 0)
    def _():
        w_s[...] = jnp.zeros_like(w_s)
        r = lax.broadcasted_iota(jnp.int32, (LANES, LANES), 0)
        c = lax.broadcasted_iota(jnp.int32, (LANES, LANES), 1)
        same_group = _div(r, Hc) == _div(c, Hc)
        for tau in range(T):
            blk = jnp.where(same_group, k_c_ref[tau], 0.0).astype(BF16)
            for s in range(T - tau):
                w_s[s * LANES:(s + 1) * LANES, (s + tau) * LANES:(s + tau + 1) * LANES] = blk
        m_s[...] = _spread_block_diag(
            m_c_ref[...], lambda c: _div(c, LANES) * Hc + _mod(c, Hc),
            lambda r: _mod(_div(r, P), GB), lambda c: _mod(_div(c, Hc), GB))

    h = jnp.concatenate([hr_ref[...], hi_ref[...]], axis=1).astype(BF16)
    y = (jnp.dot(_chunk_rows(u_ref), w_s[...], preferred_element_type=F32)
         + jnp.dot(h, m_s[...], preferred_element_type=F32))
    n = y.shape[0]
    for t in range(T):
        y_ref[pl.ds(t, n, stride=T), :] = y[:, t * LANES:(t + 1) * LANES]


def _s5_ssm(u, n_batch, ops):
    k_c, n_c, m_c, a_chunk = ops
    tokens = u.shape[0]
    T, P = SSM_CHUNK, SSM_STATE
    col_blocks = SSM_WIDTH // LANES
    rt = SSM_ROW_TILE
    chunks = tokens // T
    ct = rt // T
    op_dim = T * LANES
    half = SSM_LANE_GROUPS * P
    n_state = SSM_GROUPS * P
    grid = (col_blocks, tokens // rt)
    u_spec = pl.BlockSpec((rt, LANES), lambda cb, r: (r, cb))
    st_spec = pl.BlockSpec((ct, half), lambda cb, r: (r, cb))
    cparams = pltpu.CompilerParams(dimension_semantics=("arbitrary", "arbitrary"),
                                   vmem_limit_bytes=VMEM_LIMIT_BYTES)

    er, ei = pl.pallas_call(
        _ssm_state_in_kernel,
        grid=grid,
        in_specs=[u_spec, pl.BlockSpec((None, op_dim, LANES), lambda cb, r: (cb, 0, 0))],
        out_specs=[st_spec, st_spec],
        out_shape=[jax.ShapeDtypeStruct((chunks, n_state), F32)] * 2,
        scratch_shapes=[pltpu.VMEM((op_dim, op_dim), BF16)],
        compiler_params=cparams, name="ssm_state_in",
    )(u, n_c)

    lane_spec = pl.BlockSpec((chunks, SCAN_LANES), lambda q: (0, q))
    hr, hi = pl.pallas_call(
        functools.partial(_ssm_scan_kernel, n_seq=n_batch),
        grid=(n_state // SCAN_LANES,),
        in_specs=[pl.BlockSpec((2, 1, SCAN_LANES), lambda q: (0, 0, q)), lane_spec, lane_spec],
        out_specs=[lane_spec, lane_spec],
        out_shape=[jax.ShapeDtypeStruct((chunks, n_state), F32)] * 2,
        compiler_params=pltpu.CompilerParams(dimension_semantics=("arbitrary",),
                                             vmem_limit_bytes=VMEM_LIMIT_BYTES),
        name="ssm_scan",
    )(a_chunk, er, ei)

    return pl.pallas_call(
        _ssm_out_kernel,
        grid=grid,
        in_specs=[u_spec,
                  pl.BlockSpec((None, T, LANES, LANES), lambda cb, r: (cb, 0, 0, 0)),
                  pl.BlockSpec((None, op_dim, LANES), lambda cb, r: (cb, 0, 0)),
                  st_spec, st_spec],
        out_specs=u_spec,
        out_shape=jax.ShapeDtypeStruct((tokens, SSM_WIDTH), F32),
        scratch_shapes=[pltpu.VMEM((op_dim, op_dim), BF16), pltpu.VMEM((op_dim, op_dim), BF16)],
        compiler_params=cparams, name="ssm_out",
    )(u, k_c, m_c, hr, hi)


def _post_kernel(x1_ref, attn_ref, y_ref, ga_ref, gs_ref, wup_ref, wglu_ref, wout_ref,
                 n3_ref, wg_ref, wu_ref, wd_ref, nf_ref, o_ref):
    attn = jnp.dot(attn_ref[...], wup_ref[...], preferred_element_type=F32)
    y = jax.nn.gelu(y_ref[...], approximate=True).astype(BF16)
    glu = jnp.dot(y, wglu_ref[...], preferred_element_type=F32)
    ssm_out = glu[:, :D_MODEL] * jax.nn.sigmoid(glu[:, D_MODEL:])
    merged = jax.nn.sigmoid(ga_ref[...]) * attn + jax.nn.sigmoid(gs_ref[...]) * ssm_out
    x2 = x1_ref[...] + jnp.dot(merged.astype(BF16), wout_ref[...], preferred_element_type=F32)
    x3 = _swiglu_half_step(x2, n3_ref[...], wg_ref, wu_ref, wd_ref)
    o_ref[...] = _rms(x3, nf_ref[...])


def _post(x1, attn, y, ga, gs, wup, wglu, wout, n3, wg, wu, wd, nf):
    tokens = x1.shape[0]
    tm = TOKEN_TILE
    row = lambda w: pl.BlockSpec((tm, w), lambda i: (i, 0))
    consts = (wup, wglu, wout, n3, wg, wu, wd, nf)
    return pl.pallas_call(
        _post_kernel,
        grid=(tokens // tm,),
        in_specs=[row(D_MODEL), row(ATTN_WIDTH), row(SSM_WIDTH), row(D_MODEL), row(D_MODEL)]
        + [_const_spec(c.shape) for c in consts],
        out_specs=row(D_MODEL),
        out_shape=jax.ShapeDtypeStruct((tokens, D_MODEL), F32),
        compiler_params=pltpu.CompilerParams(
            dimension_semantics=("arbitrary",), vmem_limit_bytes=VMEM_LIMIT_BYTES),
        name="post",
    )(x1, attn, y, ga, gs, *consts)


def kernel(x, ffn1_norm, ffn1_w_gate, ffn1_w_up, ffn1_w_down, mix_norm, w_in, w_attn_up, ssm_lambda_re, ssm_lambda_im, ssm_log_dt, ssm_b_re, ssm_b_im, ssm_c_re, ssm_c_im, ssm_d, w_ssm_glu, w_out, ffn2_norm, ffn2_w_gate, ffn2_w_up, ffn2_w_down, final_norm):
    B, S, D = x.shape
    depth = ffn1_norm.shape[0]
    assert depth == 1, "final norm is fused into the single layer's last stage"
    tokens = B * S
    slopes = jnp.asarray(2.0 ** (-8.0 * np.arange(1, N_HEADS + 1) / N_HEADS), dtype=F32)
    bf = lambda w: w.astype(BF16)
    xt = x.reshape(tokens, D)
    for l in range(depth):
        x1, q, k, v, u, ga, gs = _ffn1_proj(
            xt, ffn1_norm[l][None], bf(ffn1_w_gate[l]), bf(ffn1_w_up[l]), bf(ffn1_w_down[l]),
            mix_norm[l][None], bf(w_in[l]))
        attn = _moba_attention(q, k, v, slopes, B)
        mats = _ssm_matrices(ssm_lambda_re[l], ssm_lambda_im[l], ssm_log_dt[l], ssm_b_re[l],
                             ssm_b_im[l], ssm_c_re[l], ssm_c_im[l], ssm_d[l])
        y = _s5_ssm(u, B, mats)
        xt = _post(x1, attn, y, ga, gs, bf(w_attn_up[l]), bf(w_ssm_glu[l]), bf(w_out[l]),
                   ffn2_norm[l][None], bf(ffn2_w_gate[l]), bf(ffn2_w_up[l]), bf(ffn2_w_down[l]),
                   final_norm[None])
    return xt.reshape(B, S, D)
```

```python
import functools
import math

import numpy as np
import jax
import jax.numpy as jnp
from jax import lax
from jax.experimental import pallas as pl
from jax.experimental.pallas import tpu as pltpu

F32 = jnp.float32
BF16 = jnp.bfloat16
HIGHEST = lax.Precision.HIGHEST

D_MODEL = 1024
N_HEADS = 8
HEAD_DIM = 64
ATTN_WIDTH = N_HEADS * HEAD_DIM
MOBA_BLOCK = 256
MOBA_TOPK = 3
SSM_GROUP = 16
SSM_GROUPS = 32
SSM_WIDTH = SSM_GROUP * SSM_GROUPS
SSM_STATE = 64
D_FF = 2816
EPS = 1e-6
LOG2_E = math.log2(math.e)

VMEM_LIMIT_BYTES = 56 * 1024 * 1024
TOKEN_TILE = MOBA_BLOCK
HEADS_PER_STEP = 8
SOFTMAX_ROWS = 32
MXU_DEPTH = 256
BF16_ROWS = 16
V_ROWS = HEAD_DIM + BF16_ROWS
MASKED = -1e30
LANES = 128
SSM_CHUNK = 8
SSM_LANE_GROUPS = LANES // SSM_GROUP
SSM_ROW_TILE = 4096
SCAN_LANES = 512


def _const_spec(shape):
    nd = len(shape)
    return pl.BlockSpec(shape, lambda *_: (0,) * nd, pipeline_mode=pl.Buffered(1))


def _rms(x, g):
    return x * lax.rsqrt(jnp.mean(x * x, axis=-1, keepdims=True) + EPS) * g


def _swiglu_half_step(x, g_norm, wg_ref, wu_ref, wd_ref):
    h = _rms(x, g_norm).astype(BF16)
    gate = jnp.dot(h, wg_ref[...], preferred_element_type=F32)
    up = jnp.dot(h, wu_ref[...], preferred_element_type=F32)
    act = (gate * jax.nn.sigmoid(gate) * up).astype(BF16)
    return x + 0.5 * jnp.dot(act, wd_ref[...], preferred_element_type=F32)


def _ffn1_proj_kernel(x_ref, n1_ref, wg_ref, wu_ref, wd_ref, n2_ref, win_ref,
                      x1_ref, q_ref, k_ref, v_ref, u_ref, ga_ref, gs_ref):
    x1 = _swiglu_half_step(x_ref[...], n1_ref[...], wg_ref, wu_ref, wd_ref)
    x1_ref[...] = x1
    h = _rms(x1, n2_ref[...]).astype(BF16)
    col = 0
    for ref in (q_ref, k_ref, v_ref, u_ref, ga_ref, gs_ref):
        width = ref.shape[-1]
        ref[...] = jnp.dot(h, win_ref[:, col:col + width],
                           preferred_element_type=F32).astype(ref.dtype)
        col += width


def _ffn1_proj(x, n1, wg, wu, wd, n2, win):
    tokens = x.shape[0]
    tm = TOKEN_TILE
    row = lambda w: pl.BlockSpec((tm, w), lambda i: (i, 0))
    widths = (ATTN_WIDTH, ATTN_WIDTH, ATTN_WIDTH, SSM_WIDTH, D_MODEL, D_MODEL)
    dtypes = (F32, F32, BF16, F32, F32, F32)
    return pl.pallas_call(
        _ffn1_proj_kernel,
        grid=(tokens // tm,),
        in_specs=[row(D_MODEL), _const_spec(n1.shape), _const_spec(wg.shape),
                  _const_spec(wu.shape), _const_spec(wd.shape), _const_spec(n2.shape),
                  _const_spec(win.shape)],
        out_specs=[row(D_MODEL)] + [row(w) for w in widths],
        out_shape=[jax.ShapeDtypeStruct((tokens, D_MODEL), F32)]
        + [jax.ShapeDtypeStruct((tokens, w), dt) for w, dt in zip(widths, dtypes)],
        compiler_params=pltpu.CompilerParams(
            dimension_semantics=("arbitrary",), vmem_limit_bytes=VMEM_LIMIT_BYTES),
        name="ffn1_proj",
    )(x, n1, wg, wu, wd, n2, win)


def _split3(x):
    hi = x.astype(BF16).astype(F32)
    mid = (x - hi).astype(BF16).astype(F32)
    return hi, mid, x - hi - mid


def _moba_kernel(slopes_ref, q_ref, k_ref, v_ref, o_ref, kms_ref, kaug_ref, qa_ref,
                 vt_ref, t_a, t_b, p_a, p_b, acc_ref):
    hg = pl.program_id(1)
    i = pl.program_id(2)
    blk = MOBA_BLOCK
    n_blocks = k_ref.shape[0]
    n_heads = HEADS_PER_STEP
    pair_w = MXU_DEPTH // 2
    pos_col = n_blocks
    neg_inf = -jnp.inf
    slopes = [slopes_ref[hg * n_heads + hh] * LOG2_E for hh in range(n_heads)]
    v_rows = lambda hh: slice(hh * V_ROWS, (hh + 1) * V_ROWS)
    aug_id = lax.broadcasted_iota(jnp.int32, (BF16_ROWS, blk), 0)
    pair_row_head = lax.broadcasted_iota(jnp.int32, (pair_w, blk), 0) // HEAD_DIM
    aug_col = lax.broadcasted_iota(jnp.int32, kaug_ref.shape, 1)

    @pl.when(i == 0)
    def _():
        kmean = jnp.mean(k_ref[...], axis=1)
        lane_head = lax.broadcasted_iota(jnp.int32, kmean.shape, 1) // HEAD_DIM
        km = jnp.concatenate([jnp.where(lane_head == hh, kmean, 0.0) for hh in range(n_heads)],
                             axis=0)
        km_hi = km.astype(BF16)
        kms_ref[...] = jnp.concatenate([km_hi, (km - km_hi.astype(F32)).astype(BF16)], axis=0)
        key = lax.broadcasted_iota(jnp.int32, kaug_ref.shape, 0).astype(F32)
        kaug_ref[...] = jnp.where((aug_col >= pos_col) & (aug_col < pos_col + 3), key,
                                  0.0).astype(BF16)
        ones_row = jnp.where(aug_id == 0, 1.0, 0.0).astype(BF16)

        def transpose_values(j, _):
            vt = v_ref[j].astype(F32).T.astype(BF16)
            for hh in range(n_heads):
                vt_ref[j, hh * V_ROWS:hh * V_ROWS + HEAD_DIM, :] = vt[hh * HEAD_DIM:
                                                                      (hh + 1) * HEAD_DIM]
                vt_ref[j, hh * V_ROWS + HEAD_DIM:(hh + 1) * V_ROWS, :] = ones_row
            return _
        lax.fori_loop(0, n_blocks, transpose_values, 0)

    qt = q_ref[...].T

    for hh in range(n_heads):
        pair = hh * HEAD_DIM // pair_w
        in_pair = slice(pair * pair_w, (pair + 1) * pair_w)
        qa_ref[hh, :pair_w, :] = (jnp.where(pair_row_head == hh - pair * (pair_w // HEAD_DIM),
                                            qt[in_pair], 0.0)
                                  * (HEAD_DIM ** -0.5 * LOG2_E)).astype(BF16)
        qa_ref[hh, pair_w:pair_w + n_blocks, :] = jnp.zeros((n_blocks, blk), BF16)
        pieces = _split3(jnp.full((1, blk), slopes[hh], F32)) * 2
        tile = jnp.zeros((BF16_ROWS, blk), F32)
        for n, piece in enumerate(pieces):
            tile = jnp.where(aug_id == n, piece, tile)
        qa_ref[hh, pair_w + pos_col:pair_w + pos_col + BF16_ROWS, :] = tile.astype(BF16)
        qa_ref[hh, pair_w + pos_col + BF16_ROWS:, :] = jnp.zeros(
            (MXU_DEPTH - pair_w - pos_col - BF16_ROWS, blk), BF16)

    def block_gates():
        qt_hi = qt.astype(BF16)
        qt_lo = (qt - qt_hi.astype(F32)).astype(BF16)
        gate_hi = jnp.dot(kms_ref[...], qt_hi, preferred_element_type=F32)
        return (gate_hi[:n_heads * n_blocks] + gate_hi[n_heads * n_blocks:]
                + jnp.dot(kms_ref[:n_heads * n_blocks, :], qt_lo, preferred_element_type=F32))

    def select_blocks(gates):
        shape = (n_heads, n_blocks, blk)
        blk_id = lax.broadcasted_iota(jnp.int32, shape, 1).astype(F32)
        g = jnp.where(blk_id < i.astype(F32), gates.reshape(shape), neg_inf)
        sel = jnp.zeros(shape, jnp.bool_)
        for _ in range(MOBA_TOPK):
            top = jnp.max(g, axis=1, keepdims=True)
            first = jnp.min(jnp.where(g == top, blk_id, float(n_blocks)), axis=1, keepdims=True)
            pick = (blk_id == first) & (top > neg_inf)
            sel = sel | pick
            g = jnp.where(pick, neg_inf, g)
        mask_rows = jnp.where(sel, 0.0, MASKED).astype(BF16)
        for hh in range(n_heads):
            qa_ref[hh, pair_w:pair_w + n_blocks, :] = mask_rows[hh]

    last_block = n_blocks - 1
    chunks = [slice(c, c + SOFTMAX_ROWS) for c in range(0, blk, SOFTMAX_ROWS)]

    def qk_into(t_ref, j, own):
        kj = k_ref[j].astype(BF16)
        extras = kaug_ref[...]
        if not own:
            offset = ((j - i) * blk).astype(F32)
            extras = jnp.where(aug_col == j, 1.0,
                               jnp.where((aug_col >= pos_col + 3) & (aug_col < pos_col + 6),
                                         offset, extras.astype(F32))).astype(BF16)
        col_max = []
        for hh in range(n_heads):
            pair = hh * HEAD_DIM // pair_w
            keys = jnp.concatenate([kj[:, pair * pair_w:(pair + 1) * pair_w], extras], axis=1)
            t = jnp.dot(keys, qa_ref[hh], preferred_element_type=F32)
            t_ref[hh] = t
            col_max.append(None if own else jnp.max(t, axis=0, keepdims=True))
        return tuple(col_max)

    def pv(p_ref, j, alpha):
        vj = vt_ref[j]
        for hh in range(n_heads):
            acc_ref[hh] = alpha[hh] * acc_ref[hh] + jnp.dot(vj[v_rows(hh)], p_ref[hh],
                                                            preferred_element_type=F32)

    def softmax_into(t_ref, p_ref, hh, m, cmax):
        causal = cmax is None

        def logits(c):
            t = t_ref[hh, c, :]
            if not causal:
                return t
            key = lax.broadcasted_iota(jnp.int32, t.shape, 0) + c.start
            query = lax.broadcasted_iota(jnp.int32, t.shape, 1)
            return jnp.where(key <= query, t, neg_inf)

        if causal:
            cmax = logits(chunks[0])
            for c in chunks[1:]:
                cmax = jnp.maximum(cmax, logits(c))
            cmax = jnp.max(cmax, axis=0, keepdims=True)
        m_new = cmax if m is None else jnp.maximum(m, cmax)
        for c in chunks:
            p_ref[hh, c, :] = jnp.exp2(logits(c) - m_new).astype(BF16)
        alpha = jnp.ones_like(m_new) if m is None else jnp.exp2(m - m_new)
        return alpha, m_new

    gates = block_gates()
    qk_into(t_b, i, True)
    select_blocks(gates)
    cmax_first = qk_into(t_a, 0, False)
    own = [softmax_into(t_b, p_b, hh, None, None) for hh in range(n_heads)]
    alpha0, m0 = (tuple(x) for x in zip(*own))
    acc_ref[...] = jnp.zeros_like(acc_ref)

    def body(n, carry):
        prev_j, a_prev, m, cmax_a = carry
        j0 = 2 * n
        j1 = jnp.minimum(j0 + 1, last_block)
        cmax_b = qk_into(t_b, j1, False)
        pv(p_b, prev_j, a_prev)
        st0 = [softmax_into(t_a, p_a, hh, m[hh], cmax_a[hh]) for hh in range(n_heads)]
        a0, m = (tuple(x) for x in zip(*st0))
        cmax_a = qk_into(t_a, jnp.minimum(j0 + 2, last_block), False)
        pv(p_a, j0, a0)
        st1 = [softmax_into(t_b, p_b, hh, m[hh], cmax_b[hh]) for hh in range(n_heads)]
        a1, m = (tuple(x) for x in zip(*st1))
        return j1, a1, m, cmax_a

    def double_body(n, carry):
        return body(2 * n + 1, body(2 * n, carry))

    n_pairs = (i + 1) // 2
    carry = lax.fori_loop(0, n_pairs // 2, double_body, (i, alpha0, m0, cmax_first))
    prev_j, a_prev, _, _ = lax.fori_loop(n_pairs // 2 * 2, n_pairs, body, carry)
    pv(p_b, prev_j, a_prev)
    out_t = jnp.concatenate([acc_ref[hh, :HEAD_DIM, :] / acc_ref[hh, HEAD_DIM:HEAD_DIM + 1, :]
                             for hh in range(n_heads)], axis=0)
    o_ref[...] = out_t.T.astype(o_ref.dtype)


def _moba_attention(q, k, v, slopes, n_batch):
    tokens, width = q.shape
    blk = MOBA_BLOCK
    n_blocks = tokens // n_batch // blk
    step_width = HEADS_PER_STEP * HEAD_DIM
    q_spec = pl.BlockSpec((blk, step_width), lambda b, g, i: (b * n_blocks + i, g))
    kv_spec = pl.BlockSpec((n_blocks, blk, step_width), lambda b, g, i: (b, 0, g),
                           pipeline_mode=pl.Buffered(1))
    by_block = lambda a: a.reshape(tokens // blk, blk, width)
    return pl.pallas_call(
        _moba_kernel,
        grid=(n_batch, width // step_width, n_blocks),
        in_specs=[pl.BlockSpec(memory_space=pltpu.SMEM), q_spec, kv_spec, kv_spec],
        out_specs=q_spec,
        out_shape=jax.ShapeDtypeStruct((tokens, width), BF16),
        scratch_shapes=[pltpu.VMEM((2 * HEADS_PER_STEP * n_blocks, step_width), BF16),
                        pltpu.VMEM((blk, MXU_DEPTH // 2), BF16),
                        pltpu.VMEM((HEADS_PER_STEP, MXU_DEPTH, blk), BF16),
                        pltpu.VMEM((n_blocks, HEADS_PER_STEP * V_ROWS, blk), BF16),
                        pltpu.VMEM((HEADS_PER_STEP, blk, blk), F32),
                        pltpu.VMEM((HEADS_PER_STEP, blk, blk), F32),
                        pltpu.VMEM((HEADS_PER_STEP, blk, blk), BF16),
                        pltpu.VMEM((HEADS_PER_STEP, blk, blk), BF16),
                        pltpu.VMEM((HEADS_PER_STEP, V_ROWS, blk), F32)],
        compiler_params=pltpu.CompilerParams(
            dimension_semantics=("arbitrary", "arbitrary", "arbitrary"),
            vmem_limit_bytes=VMEM_LIMIT_BYTES),
        name="moba_attn",
    )(slopes, q, by_block(k), by_block(v))


def _ssm_matrices(lam_re, lam_im, log_dt, b_re, b_im, c_re, c_im, d_skip):
    T, G, P, Hc = SSM_CHUNK, SSM_GROUPS, SSM_STATE, SSM_GROUP
    dt = jnp.exp(log_dt)[:, None]
    mag = jnp.exp(lam_re * dt)
    ar = mag * jnp.cos(lam_im * dt)
    ai = mag * jnp.sin(lam_im * dt)
    nr, ni = ar - 1.0, ai
    den = lam_re * lam_re + lam_im * lam_im
    fr = (nr * lam_re + ni * lam_im) / den
    fi = (ni * lam_re - nr * lam_im) / den
    bbr = fr[..., None] * b_re - fi[..., None] * b_im
    bbi = fr[..., None] * b_im + fi[..., None] * b_re
    pr, pi = [jnp.ones_like(ar)], [jnp.zeros_like(ar)]
    for _ in range(T):
        pr, pi = pr + [pr[-1] * ar - pi[-1] * ai], pi + [pr[-1] * ai + pi[-1] * ar]
    pw_r, pw_i = jnp.stack(pr), jnp.stack(pi)

    ca_r = c_re[None] * pw_r[:, :, None, :] - c_im[None] * pw_i[:, :, None, :]
    ca_i = c_re[None] * pw_i[:, :, None, :] + c_im[None] * pw_r[:, :, None, :]
    kern = (jnp.einsum('tgop,gpi->tgoi', ca_r[:T], bbr, precision=HIGHEST)
            - jnp.einsum('tgop,gpi->tgoi', ca_i[:T], bbi, precision=HIGHEST))
    skip = d_skip.reshape(G, Hc)[:, :, None] * jnp.eye(Hc, dtype=F32)[None]
    kern = kern.at[0].add(skip)

    GB = SSM_LANE_GROUPS
    CB = G // GB
    k_c = kern.transpose(1, 0, 3, 2).reshape(CB, GB, T, Hc, Hc).transpose(0, 2, 1, 3, 4)
    k_c = jnp.tile(k_c.reshape(CB, T, GB * Hc, Hc), (1, 1, 1, GB))
    rev_r, rev_i = pw_r[T - 1::-1], pw_i[T - 1::-1]
    n_r = rev_r[..., None] * bbr[None] - rev_i[..., None] * bbi[None]
    n_i = rev_r[..., None] * bbi[None] + rev_i[..., None] * bbr[None]
    n_c = jnp.concatenate([n_r, n_i], axis=2).transpose(0, 1, 3, 2)
    n_c = n_c.reshape(T, CB, GB, Hc, 2 * P).transpose(1, 0, 2, 3, 4).reshape(CB, T * LANES, 2 * P)
    m_c = jnp.stack([ca_r[1:], -ca_i[1:]])
    m_c = m_c.reshape(2, T, CB, GB, Hc, P).transpose(2, 0, 3, 5, 1, 4).reshape(
        CB, 2 * GB * P, T * Hc)
    a_chunk = jnp.stack([pw_r[T].reshape(1, G * P), pw_i[T].reshape(1, G * P)])
    return k_c, n_c, m_c, a_chunk


def _div(x, d):
    return x >> (d.bit_length() - 1)


def _mod(x, d):
    return x & (d - 1)


def _spread_block_diag(compact, src_of_col, row_group, col_group):
    n_rows, n_src = compact.shape
    n_cols = n_rows
    src = lax.broadcasted_iota(jnp.int32, (n_src, n_cols), 0)
    col = lax.broadcasted_iota(jnp.int32, (n_src, n_cols), 1)
    spread = jnp.where(src == src_of_col(col), 1.0, 0.0).astype(BF16)
    full = jnp.dot(compact.astype(BF16), spread, preferred_element_type=F32)
    r = lax.broadcasted_iota(jnp.int32, full.shape, 0)
    c = lax.broadcasted_iota(jnp.int32, full.shape, 1)
    return jnp.where(row_group(r) == col_group(c), full, 0.0).astype(BF16)


def _chunk_rows(u_ref):
    n = u_ref.shape[0] // SSM_CHUNK
    return jnp.concatenate([u_ref[pl.ds(s, n, stride=SSM_CHUNK), :] for s in range(SSM_CHUNK)],
                           axis=1).astype(BF16)


def _ssm_state_in_kernel(u_ref, n_c_ref, er_ref, ei_ref, n_s):
    GB, P, Hc = SSM_LANE_GROUPS, SSM_STATE, SSM_GROUP

    @pl.when(pl.program_id(1) == 0)
    def _():
        n_s[...] = _spread_block_diag(
            n_c_ref[...], lambda c: _div(c, GB * P) * P + _mod(c, P),
            lambda r: _mod(_div(r, Hc), GB), lambda c: _mod(_div(c, P), GB))

    e = jnp.dot(_chunk_rows(u_ref), n_s[...], preferred_element_type=F32)
    half = er_ref.shape[-1]
    er_ref[...] = e[:, :half]
    ei_ref[...] = e[:, half:]


def _ssm_scan_kernel(a_ref, er_ref, ei_ref, hr_ref, hi_ref, *, n_seq):
    ar = a_ref[0]
    ai = a_ref[1]
    steps = er_ref.shape[0] // n_seq
    zero = jnp.zeros_like(ar)

    def body(c, carry):
        out = []
        for b in range(n_seq):
            hr, hi = carry[2 * b], carry[2 * b + 1]
            r = b * steps + c
            hr_ref[pl.ds(r, 1), :] = hr
            hi_ref[pl.ds(r, 1), :] = hi
            er = er_ref[pl.ds(r, 1), :]
            ei = ei_ref[pl.ds(r, 1), :]
            out += [ar * hr - ai * hi + er, ar * hi + ai * hr + ei]
        return tuple(out)

    lax.fori_loop(0, steps, body, (zero,) * (2 * n_seq))


def _ssm_out_kernel(u_ref, k_c_ref, m_c_ref, hr_ref, hi_ref, y_ref, w_s, m_s):
    T, GB, P, Hc = SSM_CHUNK, SSM_LANE_GROUPS, SSM_STATE, SSM_GROUP

    @pl.when(pl.program_id(1) == 0)
    def _():
        w_s[...] = jnp.zeros_like(w_s)
        r = lax.broadcasted_iota(jnp.int32, (LANES, LANES), 0)
        c = lax.broadcasted_iota(jnp.int32, (LANES, LANES), 1)
        same_group = _div(r, Hc) == _div(c, Hc)
        for tau in range(T):
            blk = jnp.where(same_group, k_c_ref[tau], 0.0).astype(BF16)
            for s in range(T - tau):
                w_s[s * LANES:(s + 1) * LANES, (s + tau) * LANES:(s + tau + 1) * LANES] = blk
        m_s[...] = _spread_block_diag(
            m_c_ref[...], lambda c: _div(c, LANES) * Hc + _mod(c, Hc),
            lambda r: _mod(_div(r, P), GB), lambda c: _mod(_div(c, Hc), GB))

    h = jnp.concatenate([hr_ref[...], hi_ref[...]], axis=1).astype(BF16)
    y = (jnp.dot(_chunk_rows(u_ref), w_s[...], preferred_element_type=F32)
         + jnp.dot(h, m_s[...], preferred_element_type=F32))
    n = y.shape[0]
    for t in range(T):
        y_ref[pl.ds(t, n, stride=T), :] = y[:, t * LANES:(t + 1) * LANES]


def _s5_ssm(u, n_batch, ops):
    k_c, n_c, m_c, a_chunk = ops
    tokens = u.shape[0]
    T, P = SSM_CHUNK, SSM_STATE
    col_blocks = SSM_WIDTH // LANES
    rt = SSM_ROW_TILE
    chunks = tokens // T
    ct = rt // T
    op_dim = T * LANES
    half = SSM_LANE_GROUPS * P
    n_state = SSM_GROUPS * P
    grid = (col_blocks, tokens // rt)
    u_spec = pl.BlockSpec((rt, LANES), lambda cb, r: (r, cb))
    st_spec = pl.BlockSpec((ct, half), lambda cb, r: (r, cb))
    cparams = pltpu.CompilerParams(dimension_semantics=("arbitrary", "arbitrary"),
                                   vmem_limit_bytes=VMEM_LIMIT_BYTES)

    er, ei = pl.pallas_call(
        _ssm_state_in_kernel,
        grid=grid,
        in_specs=[u_spec, pl.BlockSpec((None, op_dim, LANES), lambda cb, r: (cb, 0, 0))],
        out_specs=[st_spec, st_spec],
        out_shape=[jax.ShapeDtypeStruct((chunks, n_state), F32)] * 2,
        scratch_shapes=[pltpu.VMEM((op_dim, op_dim), BF16)],
        compiler_params=cparams, name="ssm_state_in",
    )(u, n_c)

    lane_spec = pl.BlockSpec((chunks, SCAN_LANES), lambda q: (0, q))
    hr, hi = pl.pallas_call(
        functools.partial(_ssm_scan_kernel, n_seq=n_batch),
        grid=(n_state // SCAN_LANES,),
        in_specs=[pl.BlockSpec((2, 1, SCAN_LANES), lambda q: (0, 0, q)), lane_spec, lane_spec],
        out_specs=[lane_spec, lane_spec],
        out_shape=[jax.ShapeDtypeStruct((chunks, n_state), F32)] * 2,
        compiler_params=pltpu.CompilerParams(dimension_semantics=("arbitrary",),
                                             vmem_limit_bytes=VMEM_LIMIT_BYTES),
        name="ssm_scan",
    )(a_chunk, er, ei)

    return pl.pallas_call(
        _ssm_out_kernel,
        grid=grid,
        in_specs=[u_spec,
                  pl.BlockSpec((None, T, LANES, LANES), lambda cb, r: (cb, 0, 0, 0)),
                  pl.BlockSpec((None, op_dim, LANES), lambda cb, r: (cb, 0, 0)),
                  st_spec, st_spec],
        out_specs=u_spec,
        out_shape=jax.ShapeDtypeStruct((tokens, SSM_WIDTH), F32),
        scratch_shapes=[pltpu.VMEM((op_dim, op_dim), BF16), pltpu.VMEM((op_dim, op_dim), BF16)],
        compiler_params=cparams, name="ssm_out",
    )(u, k_c, m_c, hr, hi)


def _post_kernel(x1_ref, attn_ref, y_ref, ga_ref, gs_ref, wup_ref, wglu_ref, wout_ref,
                 n3_ref, wg_ref, wu_ref, wd_ref, nf_ref, o_ref):
    attn = jnp.dot(attn_ref[...], wup_ref[...], preferred_element_type=F32)
    y = jax.nn.gelu(y_ref[...], approximate=True).astype(BF16)
    glu = jnp.dot(y, wglu_ref[...], preferred_element_type=F32)
    ssm_out = glu[:, :D_MODEL] * jax.nn.sigmoid(glu[:, D_MODEL:])
    merged = jax.nn.sigmoid(ga_ref[...]) * attn + jax.nn.sigmoid(gs_ref[...]) * ssm_out
    x2 = x1_ref[...] + jnp.dot(merged.astype(BF16), wout_ref[...], preferred_element_type=F32)
    x3 = _swiglu_half_step(x2, n3_ref[...], wg_ref, wu_ref, wd_ref)
    o_ref[...] = _rms(x3, nf_ref[...])


def _post(x1, attn, y, ga, gs, wup, wglu, wout, n3, wg, wu, wd, nf):
    tokens = x1.shape[0]
    tm = TOKEN_TILE
    row = lambda w: pl.BlockSpec((tm, w), lambda i: (i, 0))
    consts = (wup, wglu, wout, n3, wg, wu, wd, nf)
    return pl.pallas_call(
        _post_kernel,
        grid=(tokens // tm,),
        in_specs=[row(D_MODEL), row(ATTN_WIDTH), row(SSM_WIDTH), row(D_MODEL), row(D_MODEL)]
        + [_const_spec(c.shape) for c in consts],
        out_specs=row(D_MODEL),
        out_shape=jax.ShapeDtypeStruct((tokens, D_MODEL), F32),
        compiler_params=pltpu.CompilerParams(
            dimension_semantics=("arbitrary",), vmem_limit_bytes=VMEM_LIMIT_BYTES),
        name="post",
    )(x1, attn, y, ga, gs, *consts)


def kernel(x, ffn1_norm, ffn1_w_gate, ffn1_w_up, ffn1_w_down, mix_norm, w_in, w_attn_up, ssm_lambda_re, ssm_lambda_im, ssm_log_dt, ssm_b_re, ssm_b_im, ssm_c_re, ssm_c_im, ssm_d, w_ssm_glu, w_out, ffn2_norm, ffn2_w_gate, ffn2_w_up, ffn2_w_down, final_norm):
    B, S, D = x.shape
    depth = ffn1_norm.shape[0]
    assert depth == 1, "final norm is fused into the single layer's last stage"
    tokens = B * S
    slopes = jnp.asarray(2.0 ** (-8.0 * np.arange(1, N_HEADS + 1) / N_HEADS), dtype=F32)
    bf = lambda w: w.astype(BF16)
    xt = x.reshape(tokens, D)
    for l in range(depth):
        x1, q, k, v, u, ga, gs = _ffn1_proj(
            xt, ffn1_norm[l][None], bf(ffn1_w_gate[l]), bf(ffn1_w_up[l]), bf(ffn1_w_down[l]),
            mix_norm[l][None], bf(w_in[l]))
        attn = _moba_attention(q, k, v, slopes, B)
        mats = _ssm_matrices(ssm_lambda_re[l], ssm_lambda_im[l], ssm_log_dt[l], ssm_b_re[l],
                             ssm_b_im[l], ssm_c_re[l], ssm_c_im[l], ssm_d[l])
        y = _s5_ssm(u, B, mats)
        xt = _post(x1, attn, y, ga, gs, bf(w_attn_up[l]), bf(w_ssm_glu[l]), bf(w_out[l]),
                   ffn2_norm[l][None], bf(ffn2_w_gate[l]), bf(ffn2_w_up[l]), bf(ffn2_w_down[l]),
                   final_norm[None])
    return xt.reshape(B, S, D)
```

```python
import functools
import math

import numpy as np
import jax
import jax.numpy as jnp
from jax import lax
from jax.experimental import pallas as pl
from jax.experimental.pallas import tpu as pltpu

F32 = jnp.float32
BF16 = jnp.bfloat16
HIGHEST = lax.Precision.HIGHEST

D_MODEL = 1024
N_HEADS = 8
HEAD_DIM = 64
ATTN_WIDTH = N_HEADS * HEAD_DIM
MOBA_BLOCK = 256
MOBA_TOPK = 3
SSM_GROUP = 16
SSM_GROUPS = 32
SSM_WIDTH = SSM_GROUP * SSM_GROUPS
SSM_STATE = 64
D_FF = 2816
EPS = 1e-6
LOG2_E = math.log2(math.e)

VMEM_LIMIT_BYTES = 56 * 1024 * 1024
TOKEN_TILE = MOBA_BLOCK
HEADS_PER_STEP = 8
SOFTMAX_ROWS = 32
MXU_DEPTH = 256
BF16_ROWS = 16
V_ROWS = HEAD_DIM + BF16_ROWS
MASKED = -1e30
LANES = 128
SSM_CHUNK = 8
SSM_LANE_GROUPS = LANES // SSM_GROUP
SSM_ROW_TILE = 4096
SCAN_LANES = 512


def _const_spec(shape):
    nd = len(shape)
    return pl.BlockSpec(shape, lambda *_: (0,) * nd, pipeline_mode=pl.Buffered(1))


def _rms(x, g):
    return x * lax.rsqrt(jnp.mean(x * x, axis=-1, keepdims=True) + EPS) * g


def _swiglu_half_step(x, g_norm, wg_ref, wu_ref, wd_ref):
    h = _rms(x, g_norm).astype(BF16)
    gate = jnp.dot(h, wg_ref[...], preferred_element_type=F32)
    up = jnp.dot(h, wu_ref[...], preferred_element_type=F32)
    act = (gate * jax.nn.sigmoid(gate) * up).astype(BF16)
    return x + 0.5 * jnp.dot(act, wd_ref[...], preferred_element_type=F32)


def _ffn1_proj_kernel(x_ref, n1_ref, wg_ref, wu_ref, wd_ref, n2_ref, win_ref,
                      x1_ref, q_ref, k_ref, v_ref, u_ref, ga_ref, gs_ref, kmean_ref):
    x1 = _swiglu_half_step(x_ref[...], n1_ref[...], wg_ref, wu_ref, wd_ref)
    x1_ref[...] = x1
    h = _rms(x1, n2_ref[...]).astype(BF16)
    col = 0
    for ref in (q_ref, k_ref, v_ref, u_ref, ga_ref, gs_ref):
        width = ref.shape[-1]
        y = jnp.dot(h, win_ref[:, col:col + width], preferred_element_type=F32)
        ref[...] = y.astype(ref.dtype)
        if ref is k_ref:
            kmean_ref[...] = jnp.mean(y, axis=0, keepdims=True)
        col += width


def _ffn1_proj(x, n1, wg, wu, wd, n2, win):
    tokens = x.shape[0]
    tm = TOKEN_TILE
    assert tm == MOBA_BLOCK
    row = lambda w: pl.BlockSpec((tm, w), lambda i: (i, 0))
    widths = (ATTN_WIDTH, ATTN_WIDTH, ATTN_WIDTH, SSM_WIDTH, D_MODEL, D_MODEL)
    dtypes = (F32, BF16, BF16, F32, F32, F32)
    return pl.pallas_call(
        _ffn1_proj_kernel,
        grid=(tokens // tm,),
        in_specs=[row(D_MODEL), _const_spec(n1.shape), _const_spec(wg.shape),
                  _const_spec(wu.shape), _const_spec(wd.shape), _const_spec(n2.shape),
                  _const_spec(win.shape)],
        out_specs=[row(D_MODEL)] + [row(w) for w in widths]
        + [pl.BlockSpec((None, 1, ATTN_WIDTH), lambda i: (i, 0, 0))],
        out_shape=[jax.ShapeDtypeStruct((tokens, D_MODEL), F32)]
        + [jax.ShapeDtypeStruct((tokens, w), dt) for w, dt in zip(widths, dtypes)]
        + [jax.ShapeDtypeStruct((tokens // tm, 1, ATTN_WIDTH), F32)],
        compiler_params=pltpu.CompilerParams(
            dimension_semantics=("arbitrary",), vmem_limit_bytes=VMEM_LIMIT_BYTES),
        name="ffn1_proj",
    )(x, n1, wg, wu, wd, n2, win)


def _split3(x):
    hi = x.astype(BF16).astype(F32)
    mid = (x - hi).astype(BF16).astype(F32)
    return hi, mid, x - hi - mid


def _moba_kernel(slopes_ref, q_ref, k_ref, v_ref, kmean_ref, o_ref, kms_ref, kaug_ref, qa_ref,
                 vt_ref, t_a, t_b, p_a, p_b, acc_ref):
    hg = pl.program_id(1)
    i = pl.program_id(2)
    blk = MOBA_BLOCK
    n_blocks = k_ref.shape[0]
    n_heads = HEADS_PER_STEP
    pair_w = MXU_DEPTH // 2
    pos_col = n_blocks
    neg_inf = -jnp.inf
    slopes = [slopes_ref[hg * n_heads + hh] * LOG2_E for hh in range(n_heads)]
    v_rows = lambda hh: slice(hh * V_ROWS, (hh + 1) * V_ROWS)
    aug_id = lax.broadcasted_iota(jnp.int32, (BF16_ROWS, blk), 0)
    pair_row_head = lax.broadcasted_iota(jnp.int32, (pair_w, blk), 0) // HEAD_DIM
    aug_col = lax.broadcasted_iota(jnp.int32, kaug_ref.shape, 1)

    @pl.when(i == 0)
    def _():
        kmean = kmean_ref[...]
        lane_head = lax.broadcasted_iota(jnp.int32, kmean.shape, 1) // HEAD_DIM
        km = jnp.concatenate([jnp.where(lane_head == hh, kmean, 0.0) for hh in range(n_heads)],
                             axis=0)
        km_hi = km.astype(BF16)
        kms_ref[...] = jnp.concatenate([km_hi, (km - km_hi.astype(F32)).astype(BF16)], axis=0)
        key = lax.broadcasted_iota(jnp.int32, kaug_ref.shape, 0).astype(F32)
        kaug_ref[...] = jnp.where((aug_col >= pos_col) & (aug_col < pos_col + 3), key,
                                  0.0).astype(BF16)
        ones_row = jnp.where(aug_id == 0, 1.0, 0.0).astype(BF16)

        def transpose_values(j, _):
            vt = v_ref[j].astype(F32).T.astype(BF16)
            for hh in range(n_heads):
                vt_ref[j, hh * V_ROWS:hh * V_ROWS + HEAD_DIM, :] = vt[hh * HEAD_DIM:
                                                                      (hh + 1) * HEAD_DIM]
                vt_ref[j, hh * V_ROWS + HEAD_DIM:(hh + 1) * V_ROWS, :] = ones_row
            return _
        lax.fori_loop(0, n_blocks, transpose_values, 0)

    qt = q_ref[...].T

    for hh in range(n_heads):
        pair = hh * HEAD_DIM // pair_w
        in_pair = slice(pair * pair_w, (pair + 1) * pair_w)
        qa_ref[hh, :pair_w, :] = (jnp.where(pair_row_head == hh - pair * (pair_w // HEAD_DIM),
                                            qt[in_pair], 0.0)
                                  * (HEAD_DIM ** -0.5 * LOG2_E)).astype(BF16)
        qa_ref[hh, pair_w:pair_w + n_blocks, :] = jnp.zeros((n_blocks, blk), BF16)
        pieces = _split3(jnp.full((1, blk), slopes[hh], F32)) * 2
        tile = jnp.zeros((BF16_ROWS, blk), F32)
        for n, piece in enumerate(pieces):
            tile = jnp.where(aug_id == n, piece, tile)
        qa_ref[hh, pair_w + pos_col:pair_w + pos_col + BF16_ROWS, :] = tile.astype(BF16)
        qa_ref[hh, pair_w + pos_col + BF16_ROWS:, :] = jnp.zeros(
            (MXU_DEPTH - pair_w - pos_col - BF16_ROWS, blk), BF16)

    def block_gates():
        qt_hi = qt.astype(BF16)
        qt_lo = (qt - qt_hi.astype(F32)).astype(BF16)
        gate_hi = jnp.dot(kms_ref[...], qt_hi, preferred_element_type=F32)
        return (gate_hi[:n_heads * n_blocks] + gate_hi[n_heads * n_blocks:]
                + jnp.dot(kms_ref[:n_heads * n_blocks, :], qt_lo, preferred_element_type=F32))

    def select_blocks(gates):
        shape = (n_heads, n_blocks, blk)
        blk_id = lax.broadcasted_iota(jnp.int32, shape, 1).astype(F32)
        g = jnp.where(blk_id < i.astype(F32), gates.reshape(shape), neg_inf)
        sel = jnp.zeros(shape, jnp.bool_)
        for _ in range(MOBA_TOPK):
            top = jnp.max(g, axis=1, keepdims=True)
            first = jnp.min(jnp.where(g == top, blk_id, float(n_blocks)), axis=1, keepdims=True)
            pick = (blk_id == first) & (top > neg_inf)
            sel = sel | pick
            g = jnp.where(pick, neg_inf, g)
        mask_rows = jnp.where(sel, 0.0, MASKED).astype(BF16)
        for hh in range(n_heads):
            qa_ref[hh, pair_w:pair_w + n_blocks, :] = mask_rows[hh]

    last_block = n_blocks - 1
    chunks = [slice(c, c + SOFTMAX_ROWS) for c in range(0, blk, SOFTMAX_ROWS)]

    def qk_into(t_ref, j, own):
        kj = k_ref[j]
        extras = kaug_ref[...]
        if not own:
            offset = ((j - i) * blk).astype(F32)
            extras = jnp.where(aug_col == j, 1.0,
                               jnp.where((aug_col >= pos_col + 3) & (aug_col < pos_col + 6),
                                         offset, extras.astype(F32))).astype(BF16)
        col_max = []
        for hh in range(n_heads):
            pair = hh * HEAD_DIM // pair_w
            keys = jnp.concatenate([kj[:, pair * pair_w:(pair + 1) * pair_w], extras], axis=1)
            t = jnp.dot(keys, qa_ref[hh], preferred_element_type=F32)
            t_ref[hh] = t
            col_max.append(None if own else jnp.max(t, axis=0, keepdims=True))
        return tuple(col_max)

    def pv(p_ref, j, alpha):
        vj = vt_ref[j]
        for hh in range(n_heads):
            acc_ref[hh] = alpha[hh] * acc_ref[hh] + jnp.dot(vj[v_rows(hh)], p_ref[hh],
                                                            preferred_element_type=F32)

    def softmax_into(t_ref, p_ref, hh, m, cmax):
        causal = cmax is None

        def logits(c):
            t = t_ref[hh, c, :]
            if not causal:
                return t
            key = lax.broadcasted_iota(jnp.int32, t.shape, 0) + c.start
            query = lax.broadcasted_iota(jnp.int32, t.shape, 1)
            return jnp.where(key <= query, t, neg_inf)

        if causal:
            cmax = logits(chunks[0])
            for c in chunks[1:]:
                cmax = jnp.maximum(cmax, logits(c))
            cmax = jnp.max(cmax, axis=0, keepdims=True)
        m_new = cmax if m is None else jnp.maximum(m, cmax)
        for c in chunks:
            p_ref[hh, c, :] = jnp.exp2(logits(c) - m_new).astype(BF16)
        alpha = jnp.ones_like(m_new) if m is None else jnp.exp2(m - m_new)
        return alpha, m_new

    gates = block_gates()
    qk_into(t_b, i, True)
    select_blocks(gates)
    cmax_first = qk_into(t_a, 0, False)
    own = [softmax_into(t_b, p_b, hh, None, None) for hh in range(n_heads)]
    alpha0, m0 = (tuple(x) for x in zip(*own))
    acc_ref[...] = jnp.zeros_like(acc_ref)

    def body(n, carry):
        prev_j, a_prev, m, cmax_a = carry
        j0 = 2 * n
        j1 = jnp.minimum(j0 + 1, last_block)
        cmax_b = qk_into(t_b, j1, False)
        pv(p_b, prev_j, a_prev)
        st0 = [softmax_into(t_a, p_a, hh, m[hh], cmax_a[hh]) for hh in range(n_heads)]
        a0, m = (tuple(x) for x in zip(*st0))
        cmax_a = qk_into(t_a, jnp.minimum(j0 + 2, last_block), False)
        pv(p_a, j0, a0)
        st1 = [softmax_into(t_b, p_b, hh, m[hh], cmax_b[hh]) for hh in range(n_heads)]
        a1, m = (tuple(x) for x in zip(*st1))
        return j1, a1, m, cmax_a

    def double_body(n, carry):
        return body(2 * n + 1, body(2 * n, carry))

    n_pairs = (i + 1) // 2
    carry = lax.fori_loop(0, n_pairs // 2, double_body, (i, alpha0, m0, cmax_first))
    prev_j, a_prev, _, _ = lax.fori_loop(n_pairs // 2 * 2, n_pairs, body, carry)
    pv(p_b, prev_j, a_prev)
    out_t = jnp.concatenate([acc_ref[hh, :HEAD_DIM, :] / acc_ref[hh, HEAD_DIM:HEAD_DIM + 1, :]
                             for hh in range(n_heads)], axis=0)
    o_ref[...] = out_t.T.astype(o_ref.dtype)


def _moba_attention(q, k, v, kmean, slopes, n_batch):
    tokens, width = q.shape
    blk = MOBA_BLOCK
    n_blocks = tokens // n_batch // blk
    step_width = HEADS_PER_STEP * HEAD_DIM
    q_spec = pl.BlockSpec((blk, step_width), lambda b, g, i: (b * n_blocks + i, g))
    kv_spec = pl.BlockSpec((n_blocks, blk, step_width), lambda b, g, i: (b, 0, g),
                           pipeline_mode=pl.Buffered(1))
    by_block = lambda a: a.reshape(tokens // blk, blk, width)
    return pl.pallas_call(
        _moba_kernel,
        grid=(n_batch, width // step_width, n_blocks),
        in_specs=[pl.BlockSpec(memory_space=pltpu.SMEM), q_spec, kv_spec, kv_spec,
                  pl.BlockSpec((n_blocks, step_width), lambda b, g, i: (b, g))],
        out_specs=q_spec,
        out_shape=jax.ShapeDtypeStruct((tokens, width), BF16),
        scratch_shapes=[pltpu.VMEM((2 * HEADS_PER_STEP * n_blocks, step_width), BF16),
                        pltpu.VMEM((blk, MXU_DEPTH // 2), BF16),
                        pltpu.VMEM((HEADS_PER_STEP, MXU_DEPTH, blk), BF16),
                        pltpu.VMEM((n_blocks, HEADS_PER_STEP * V_ROWS, blk), BF16),
                        pltpu.VMEM((HEADS_PER_STEP, blk, blk), F32),
                        pltpu.VMEM((HEADS_PER_STEP, blk, blk), F32),
                        pltpu.VMEM((HEADS_PER_STEP, blk, blk), BF16),
                        pltpu.VMEM((HEADS_PER_STEP, blk, blk), BF16),
                        pltpu.VMEM((HEADS_PER_STEP, V_ROWS, blk), F32)],
        compiler_params=pltpu.CompilerParams(
            dimension_semantics=("arbitrary", "arbitrary", "arbitrary"),
            vmem_limit_bytes=VMEM_LIMIT_BYTES),
        name="moba_attn",
    )(slopes, q, by_block(k), by_block(v), kmean)


def _ssm_matrices(lam_re, lam_im, log_dt, b_re, b_im, c_re, c_im, d_skip):
    T, G, P, Hc = SSM_CHUNK, SSM_GROUPS, SSM_STATE, SSM_GROUP
    dt = jnp.exp(log_dt)[:, None]
    mag = jnp.exp(lam_re * dt)
    ar = mag * jnp.cos(lam_im * dt)
    ai = mag * jnp.sin(lam_im * dt)
    nr, ni = ar - 1.0, ai
    den = lam_re * lam_re + lam_im * lam_im
    fr = (nr * lam_re + ni * lam_im) / den
    fi = (ni * lam_re - nr * lam_im) / den
    bbr = fr[..., None] * b_re - fi[..., None] * b_im
    bbi = fr[..., None] * b_im + fi[..., None] * b_re
    pr, pi = [jnp.ones_like(ar)], [jnp.zeros_like(ar)]
    for _ in range(T):
        pr, pi = pr + [pr[-1] * ar - pi[-1] * ai], pi + [pr[-1] * ai + pi[-1] * ar]
    pw_r, pw_i = jnp.stack(pr), jnp.stack(pi)

    ca_r = c_re[None] * pw_r[:, :, None, :] - c_im[None] * pw_i[:, :, None, :]
    ca_i = c_re[None] * pw_i[:, :, None, :] + c_im[None] * pw_r[:, :, None, :]
    kern = (jnp.einsum('tgop,gpi->tgoi', ca_r[:T], bbr, precision=HIGHEST)
            - jnp.einsum('tgop,gpi->tgoi', ca_i[:T], bbi, precision=HIGHEST))
    skip = d_skip.reshape(G, Hc)[:, :, None] * jnp.eye(Hc, dtype=F32)[None]
    kern = kern.at[0].add(skip)

    GB = SSM_LANE_GROUPS
    CB = G // GB
    k_c = kern.transpose(1, 0, 3, 2).reshape(CB, GB, T, Hc, Hc).transpose(0, 2, 1, 3, 4)
    k_c = jnp.tile(k_c.reshape(CB, T, GB * Hc, Hc), (1, 1, 1, GB))
    rev_r, rev_i = pw_r[T - 1::-1], pw_i[T - 1::-1]
    n_r = rev_r[..., None] * bbr[None] - rev_i[..., None] * bbi[None]
    n_i = rev_r[..., None] * bbi[None] + rev_i[..., None] * bbr[None]
    n_c = jnp.concatenate([n_r, n_i], axis=2).transpose(0, 1, 3, 2)
    n_c = n_c.reshape(T, CB, GB, Hc, 2 * P).transpose(1, 0, 2, 3, 4).reshape(CB, T * LANES, 2 * P)
    m_c = jnp.stack([ca_r[1:], -ca_i[1:]])
    m_c = m_c.reshape(2, T, CB, GB, Hc, P).transpose(2, 0, 3, 5, 1, 4).reshape(
        CB, 2 * GB * P, T * Hc)
    a_chunk = jnp.stack([pw_r[T].reshape(1, G * P), pw_i[T].reshape(1, G * P)])
    return k_c, n_c, m_c, a_chunk


def _div(x, d):
    return x >> (d.bit_length() - 1)


def _mod(x, d):
    return x & (d - 1)


def _spread_block_diag(compact, src_of_col, row_group, col_group):
    n_rows, n_src = compact.shape
    n_cols = n_rows
    src = lax.broadcasted_iota(jnp.int32, (n_src, n_cols), 0)
    col = lax.broadcasted_iota(jnp.int32, (n_src, n_cols), 1)
    spread = jnp.where(src == src_of_col(col), 1.0, 0.0).astype(BF16)
    full = jnp.dot(compact.astype(BF16), spread, preferred_element_type=F32)
    r = lax.broadcasted_iota(jnp.int32, full.shape, 0)
    c = lax.broadcasted_iota(jnp.int32, full.shape, 1)
    return jnp.where(row_group(r) == col_group(c), full, 0.0).astype(BF16)


def _chunk_rows(u_ref):
    n = u_ref.shape[0] // SSM_CHUNK
    return jnp.concatenate([u_ref[pl.ds(s, n, stride=SSM_CHUNK), :] for s in range(SSM_CHUNK)],
                           axis=1).astype(BF16)


def _ssm_state_in_kernel(u_ref, n_c_ref, er_ref, ei_ref, n_s):
    GB, P, Hc = SSM_LANE_GROUPS, SSM_STATE, SSM_GROUP

    @pl.when(pl.program_id(1) == 0)
    def _():
        n_s[...] = _spread_block_diag(
            n_c_ref[...], lambda c: _div(c, GB * P) * P + _mod(c, P),
            lambda r: _mod(_div(r, Hc), GB), lambda c: _mod(_div(c, P), GB))

    e = jnp.dot(_chunk_rows(u_ref), n_s[...], preferred_element_type=F32)
    half = er_ref.shape[-1]
    er_ref[...] = e[:, :half].astype(er_ref.dtype)
    ei_ref[...] = e[:, half:].astype(ei_ref.dtype)


def _ssm_scan_kernel(a_ref, er_ref, ei_ref, hr_ref, hi_ref, er_s, ei_s, hr_s, hi_s, *, n_seq):
    er_s[...] = er_ref[...].astype(F32)
    ei_s[...] = ei_ref[...].astype(F32)
    ar = a_ref[0]
    ai = a_ref[1]
    steps = er_ref.shape[0] // n_seq
    zero = jnp.zeros_like(ar)

    def body(c, carry):
        out = []
        for b in range(n_seq):
            hr, hi = carry[2 * b], carry[2 * b + 1]
            r = b * steps + c
            hr_s[pl.ds(r, 1), :] = hr
            hi_s[pl.ds(r, 1), :] = hi
            er = er_s[pl.ds(r, 1), :]
            ei = ei_s[pl.ds(r, 1), :]
            out += [ar * hr - ai * hi + er, ar * hi + ai * hr + ei]
        return tuple(out)

    lax.fori_loop(0, steps, body, (zero,) * (2 * n_seq))
    hr_ref[...] = hr_s[...].astype(hr_ref.dtype)
    hi_ref[...] = hi_s[...].astype(hi_ref.dtype)


def _ssm_out_kernel(u_ref, k_c_ref, m_c_ref, hr_ref, hi_ref, y_ref, w_s, m_s):
    T, GB, P, Hc = SSM_CHUNK, SSM_LANE_GROUPS, SSM_STATE, SSM_GROUP

    @pl.when(pl.program_id(1) == 0)
    def _():
        w_s[...] = jnp.zeros_like(w_s)
        r = lax.broadcasted_iota(jnp.int32, (LANES, LANES), 0)
        c = lax.broadcasted_iota(jnp.int32, (LANES, LANES), 1)
        same_group = _div(r, Hc) == _div(c, Hc)
        for tau in range(T):
            blk = jnp.where(same_group, k_c_ref[tau], 0.0).astype(BF16)
            for s in range(T - tau):
                w_s[s * LANES:(s + 1) * LANES, (s + tau) * LANES:(s + tau + 1) * LANES] = blk
        m_s[...] = _spread_block_diag(
            m_c_ref[...], lambda c: _div(c, LANES) * Hc + _mod(c, Hc),
            lambda r: _mod(_div(r, P), GB), lambda c: _mod(_div(c, Hc), GB))

    h = jnp.concatenate([hr_ref[...], hi_ref[...]], axis=1).astype(BF16)
    y = (jnp.dot(_chunk_rows(u_ref), w_s[...], preferred_element_type=F32)
         + jnp.dot(h, m_s[...], preferred_element_type=F32))
    n = y.shape[0]
    for t in range(T):
        y_ref[pl.ds(t, n, stride=T), :] = y[:, t * LANES:(t + 1) * LANES]


def _s5_ssm(u, n_batch, ops):
    k_c, n_c, m_c, a_chunk = ops
    tokens = u.shape[0]
    T, P = SSM_CHUNK, SSM_STATE
    col_blocks = SSM_WIDTH // LANES
    rt = SSM_ROW_TILE
    chunks = tokens // T
    ct = rt // T
    op_dim = T * LANES
    half = SSM_LANE_GROUPS * P
    n_state = SSM_GROUPS * P
    grid = (col_blocks, tokens // rt)
    u_spec = pl.BlockSpec((rt, LANES), lambda cb, r: (r, cb))
    st_spec = pl.BlockSpec((ct, half), lambda cb, r: (r, cb))
    cparams = pltpu.CompilerParams(dimension_semantics=("arbitrary", "arbitrary"),
                                   vmem_limit_bytes=VMEM_LIMIT_BYTES)

    er, ei = pl.pallas_call(
        _ssm_state_in_kernel,
        grid=grid,
        in_specs=[u_spec, pl.BlockSpec((None, op_dim, LANES), lambda cb, r: (cb, 0, 0))],
        out_specs=[st_spec, st_spec],
        out_shape=[jax.ShapeDtypeStruct((chunks, n_state), BF16)] * 2,
        scratch_shapes=[pltpu.VMEM((op_dim, op_dim), BF16)],
        compiler_params=cparams, name="ssm_state_in",
    )(u, n_c)

    lane_spec = pl.BlockSpec((chunks, SCAN_LANES), lambda q: (0, q))
    hr, hi = pl.pallas_call(
        functools.partial(_ssm_scan_kernel, n_seq=n_batch),
        grid=(n_state // SCAN_LANES,),
        in_specs=[pl.BlockSpec((2, 1, SCAN_LANES), lambda q: (0, 0, q)), lane_spec, lane_spec],
        out_specs=[lane_spec, lane_spec],
        out_shape=[jax.ShapeDtypeStruct((chunks, n_state), BF16)] * 2,
        scratch_shapes=[pltpu.VMEM((chunks, SCAN_LANES), F32)] * 4,
        compiler_params=pltpu.CompilerParams(dimension_semantics=("arbitrary",),
                                             vmem_limit_bytes=VMEM_LIMIT_BYTES),
        name="ssm_scan",
    )(a_chunk, er, ei)

    return pl.pallas_call(
        _ssm_out_kernel,
        grid=grid,
        in_specs=[u_spec,
                  pl.BlockSpec((None, T, LANES, LANES), lambda cb, r: (cb, 0, 0, 0)),
                  pl.BlockSpec((None, op_dim, LANES), lambda cb, r: (cb, 0, 0)),
                  st_spec, st_spec],
        out_specs=u_spec,
        out_shape=jax.ShapeDtypeStruct((tokens, SSM_WIDTH), F32),
        scratch_shapes=[pltpu.VMEM((op_dim, op_dim), BF16), pltpu.VMEM((op_dim, op_dim), BF16)],
        compiler_params=cparams, name="ssm_out",
    )(u, k_c, m_c, hr, hi)


def _post_kernel(x1_ref, attn_ref, y_ref, ga_ref, gs_ref, wup_ref, wglu_ref, wout_ref,
                 n3_ref, wg_ref, wu_ref, wd_ref, nf_ref, o_ref):
    attn = jnp.dot(attn_ref[...], wup_ref[...], preferred_element_type=F32)
    y = jax.nn.gelu(y_ref[...], approximate=True).astype(BF16)
    glu = jnp.dot(y, wglu_ref[...], preferred_element_type=F32)
    ssm_out = glu[:, :D_MODEL] * jax.nn.sigmoid(glu[:, D_MODEL:])
    merged = jax.nn.sigmoid(ga_ref[...]) * attn + jax.nn.sigmoid(gs_ref[...]) * ssm_out
    x2 = x1_ref[...] + jnp.dot(merged.astype(BF16), wout_ref[...], preferred_element_type=F32)
    x3 = _swiglu_half_step(x2, n3_ref[...], wg_ref, wu_ref, wd_ref)
    o_ref[...] = _rms(x3, nf_ref[...])


def _post(x1, attn, y, ga, gs, wup, wglu, wout, n3, wg, wu, wd, nf):
    tokens = x1.shape[0]
    tm = TOKEN_TILE
    row = lambda w: pl.BlockSpec((tm, w), lambda i: (i, 0))
    consts = (wup, wglu, wout, n3, wg, wu, wd, nf)
    return pl.pallas_call(
        _post_kernel,
        grid=(tokens // tm,),
        in_specs=[row(D_MODEL), row(ATTN_WIDTH), row(SSM_WIDTH), row(D_MODEL), row(D_MODEL)]
        + [_const_spec(c.shape) for c in consts],
        out_specs=row(D_MODEL),
        out_shape=jax.ShapeDtypeStruct((tokens, D_MODEL), F32),
        compiler_params=pltpu.CompilerParams(
            dimension_semantics=("arbitrary",), vmem_limit_bytes=VMEM_LIMIT_BYTES),
        name="post",
    )(x1, attn, y, ga, gs, *consts)


def kernel(x, ffn1_norm, ffn1_w_gate, ffn1_w_up, ffn1_w_down, mix_norm, w_in, w_attn_up, ssm_lambda_re, ssm_lambda_im, ssm_log_dt, ssm_b_re, ssm_b_im, ssm_c_re, ssm_c_im, ssm_d, w_ssm_glu, w_out, ffn2_norm, ffn2_w_gate, ffn2_w_up, ffn2_w_down, final_norm):
    B, S, D = x.shape
    depth = ffn1_norm.shape[0]
    assert depth == 1, "final norm is fused into the single layer's last stage"
    tokens = B * S
    slopes = jnp.asarray(2.0 ** (-8.0 * np.arange(1, N_HEADS + 1) / N_HEADS), dtype=F32)
    bf = lambda w: w.astype(BF16)
    xt = x.reshape(tokens, D)
    for l in range(depth):
        x1, q, k, v, u, ga, gs, kmean = _ffn1_proj(
            xt, ffn1_norm[l][None], bf(ffn1_w_gate[l]), bf(ffn1_w_up[l]), bf(ffn1_w_down[l]),
            mix_norm[l][None], bf(w_in[l]))
        attn = _moba_attention(q, k, v, kmean.reshape(-1, ATTN_WIDTH), slopes, B)
        mats = _ssm_matrices(ssm_lambda_re[l], ssm_lambda_im[l], ssm_log_dt[l], ssm_b_re[l],
                             ssm_b_im[l], ssm_c_re[l], ssm_c_im[l], ssm_d[l])
        y = _s5_ssm(u, B, mats)
        xt = _post(x1, attn, y, ga, gs, bf(w_attn_up[l]), bf(w_ssm_glu[l]), bf(w_out[l]),
                   ffn2_norm[l][None], bf(ffn2_w_gate[l]), bf(ffn2_w_up[l]), bf(ffn2_w_down[l]),
                   final_norm[None])
    return xt.reshape(B, S, D)
```

```python
import functools
import math

import numpy as np
import jax
import jax.numpy as jnp
from jax import lax
from jax.experimental import pallas as pl
from jax.experimental.pallas import tpu as pltpu

F32 = jnp.float32
BF16 = jnp.bfloat16
HIGHEST = lax.Precision.HIGHEST

D_MODEL = 1024
N_HEADS = 8
HEAD_DIM = 64
ATTN_WIDTH = N_HEADS * HEAD_DIM
MOBA_BLOCK = 256
MOBA_TOPK = 3
SSM_GROUP = 16
SSM_GROUPS = 32
SSM_WIDTH = SSM_GROUP * SSM_GROUPS
SSM_STATE = 64
D_FF = 2816
EPS = 1e-6
LOG2_E = math.log2(math.e)

VMEM_LIMIT_BYTES = 56 * 1024 * 1024
TOKEN_TILE = MOBA_BLOCK
HEADS_PER_STEP = 8
SOFTMAX_ROWS = 32
MXU_DEPTH = 256
BF16_ROWS = 16
V_ROWS = HEAD_DIM + BF16_ROWS
MASKED = -1e30
LANES = 128
SSM_CHUNK = 8
SSM_LANE_GROUPS = LANES // SSM_GROUP
SSM_ROW_TILE = 2048


def _const_spec(shape):
    nd = len(shape)
    return pl.BlockSpec(shape, lambda *_: (0,) * nd, pipeline_mode=pl.Buffered(1))


def _rms(x, g):
    return x * lax.rsqrt(jnp.mean(x * x, axis=-1, keepdims=True) + EPS) * g


def _swiglu_half_step(x, g_norm, wg_ref, wu_ref, wd_ref):
    h = _rms(x, g_norm).astype(BF16)
    gate = jnp.dot(h, wg_ref[...], preferred_element_type=F32)
    up = jnp.dot(h, wu_ref[...], preferred_element_type=F32)
    act = (gate * jax.nn.sigmoid(gate) * up).astype(BF16)
    return x + 0.5 * jnp.dot(act, wd_ref[...], preferred_element_type=F32)


def _ffn1_proj_kernel(x_ref, n1_ref, wg_ref, wu_ref, wd_ref, n2_ref, win_ref,
                      x1_ref, q_ref, k_ref, v_ref, u_ref, ga_ref, gs_ref, kmean_ref):
    x1 = _swiglu_half_step(x_ref[...], n1_ref[...], wg_ref, wu_ref, wd_ref)
    x1_ref[...] = x1
    h = _rms(x1, n2_ref[...]).astype(BF16)
    col = 0
    for ref in (q_ref, k_ref, v_ref, u_ref, ga_ref, gs_ref):
        width = ref.shape[-1]
        y = jnp.dot(h, win_ref[:, col:col + width], preferred_element_type=F32)
        ref[...] = y.astype(ref.dtype)
        if ref is k_ref:
            kmean_ref[...] = jnp.mean(y, axis=0, keepdims=True)
        col += width


def _ffn1_proj(x, n1, wg, wu, wd, n2, win):
    tokens = x.shape[0]
    tm = TOKEN_TILE
    assert tm == MOBA_BLOCK
    row = lambda w: pl.BlockSpec((tm, w), lambda i: (i, 0))
    widths = (ATTN_WIDTH, ATTN_WIDTH, ATTN_WIDTH, SSM_WIDTH, D_MODEL, D_MODEL)
    dtypes = (F32, BF16, BF16, F32, F32, F32)
    return pl.pallas_call(
        _ffn1_proj_kernel,
        grid=(tokens // tm,),
        in_specs=[row(D_MODEL), _const_spec(n1.shape), _const_spec(wg.shape),
                  _const_spec(wu.shape), _const_spec(wd.shape), _const_spec(n2.shape),
                  _const_spec(win.shape)],
        out_specs=[row(D_MODEL)] + [row(w) for w in widths]
        + [pl.BlockSpec((None, 1, ATTN_WIDTH), lambda i: (i, 0, 0))],
        out_shape=[jax.ShapeDtypeStruct((tokens, D_MODEL), F32)]
        + [jax.ShapeDtypeStruct((tokens, w), dt) for w, dt in zip(widths, dtypes)]
        + [jax.ShapeDtypeStruct((tokens // tm, 1, ATTN_WIDTH), F32)],
        compiler_params=pltpu.CompilerParams(
            dimension_semantics=("arbitrary",), vmem_limit_bytes=VMEM_LIMIT_BYTES),
        name="ffn1_proj",
    )(x, n1, wg, wu, wd, n2, win)


def _split3(x):
    hi = x.astype(BF16).astype(F32)
    mid = (x - hi).astype(BF16).astype(F32)
    return hi, mid, x - hi - mid


def _moba_kernel(slopes_ref, q_ref, k_ref, v_ref, kmean_ref, o_ref, kms_ref, kaug_ref, qa_ref,
                 vt_ref, t_a, t_b, p_a, p_b, acc_ref):
    hg = pl.program_id(1)
    i = pl.program_id(2)
    blk = MOBA_BLOCK
    n_blocks = k_ref.shape[0]
    n_heads = HEADS_PER_STEP
    pair_w = MXU_DEPTH // 2
    pos_col = n_blocks
    neg_inf = -jnp.inf
    slopes = [slopes_ref[hg * n_heads + hh] * LOG2_E for hh in range(n_heads)]
    v_rows = lambda hh: slice(hh * V_ROWS, (hh + 1) * V_ROWS)
    aug_id = lax.broadcasted_iota(jnp.int32, (BF16_ROWS, blk), 0)
    pair_row_head = lax.broadcasted_iota(jnp.int32, (pair_w, blk), 0) // HEAD_DIM
    aug_col = lax.broadcasted_iota(jnp.int32, kaug_ref.shape, 1)

    @pl.when(i == 0)
    def _():
        kmean = kmean_ref[...]
        lane_head = lax.broadcasted_iota(jnp.int32, kmean.shape, 1) // HEAD_DIM
        km = jnp.concatenate([jnp.where(lane_head == hh, kmean, 0.0) for hh in range(n_heads)],
                             axis=0)
        km_hi = km.astype(BF16)
        kms_ref[...] = jnp.concatenate([km_hi, (km - km_hi.astype(F32)).astype(BF16)], axis=0)
        key = lax.broadcasted_iota(jnp.int32, kaug_ref.shape, 0).astype(F32)
        kaug_ref[...] = jnp.where((aug_col >= pos_col) & (aug_col < pos_col + 3), key,
                                  0.0).astype(BF16)
        ones_row = jnp.where(aug_id == 0, 1.0, 0.0).astype(BF16)

        def transpose_values(j, _):
            vt = v_ref[j].astype(F32).T.astype(BF16)
            for hh in range(n_heads):
                vt_ref[j, hh * V_ROWS:hh * V_ROWS + HEAD_DIM, :] = vt[hh * HEAD_DIM:
                                                                      (hh + 1) * HEAD_DIM]
                vt_ref[j, hh * V_ROWS + HEAD_DIM:(hh + 1) * V_ROWS, :] = ones_row
            return _
        lax.fori_loop(0, n_blocks, transpose_values, 0)

    qt = q_ref[...].T

    for hh in range(n_heads):
        pair = hh * HEAD_DIM // pair_w
        in_pair = slice(pair * pair_w, (pair + 1) * pair_w)
        qa_ref[hh, :pair_w, :] = (jnp.where(pair_row_head == hh - pair * (pair_w // HEAD_DIM),
                                            qt[in_pair], 0.0)
                                  * (HEAD_DIM ** -0.5 * LOG2_E)).astype(BF16)
        qa_ref[hh, pair_w:pair_w + n_blocks, :] = jnp.zeros((n_blocks, blk), BF16)
        pieces = _split3(jnp.full((1, blk), slopes[hh], F32)) * 2
        tile = jnp.zeros((BF16_ROWS, blk), F32)
        for n, piece in enumerate(pieces):
            tile = jnp.where(aug_id == n, piece, tile)
        qa_ref[hh, pair_w + pos_col:pair_w + pos_col + BF16_ROWS, :] = tile.astype(BF16)
        qa_ref[hh, pair_w + pos_col + BF16_ROWS:, :] = jnp.zeros(
            (MXU_DEPTH - pair_w - pos_col - BF16_ROWS, blk), BF16)

    def block_gates():
        qt_hi = qt.astype(BF16)
        qt_lo = (qt - qt_hi.astype(F32)).astype(BF16)
        gate_hi = jnp.dot(kms_ref[...], qt_hi, preferred_element_type=F32)
        return (gate_hi[:n_heads * n_blocks] + gate_hi[n_heads * n_blocks:]
                + jnp.dot(kms_ref[:n_heads * n_blocks, :], qt_lo, preferred_element_type=F32))

    def select_blocks(gates):
        shape = (n_heads, n_blocks, blk)
        blk_id = lax.broadcasted_iota(jnp.int32, shape, 1).astype(F32)
        g = jnp.where(blk_id < i.astype(F32), gates.reshape(shape), neg_inf)
        sel = jnp.zeros(shape, jnp.bool_)
        for _ in range(MOBA_TOPK):
            top = jnp.max(g, axis=1, keepdims=True)
            first = jnp.min(jnp.where(g == top, blk_id, float(n_blocks)), axis=1, keepdims=True)
            pick = (blk_id == first) & (top > neg_inf)
            sel = sel | pick
            g = jnp.where(pick, neg_inf, g)
        mask_rows = jnp.where(sel, 0.0, MASKED).astype(BF16)
        for hh in range(n_heads):
            qa_ref[hh, pair_w:pair_w + n_blocks, :] = mask_rows[hh]

    last_block = n_blocks - 1
    chunks = [slice(c, c + SOFTMAX_ROWS) for c in range(0, blk, SOFTMAX_ROWS)]

    def qk_into(t_ref, j, own):
        kj = k_ref[j]
        extras = kaug_ref[...]
        if not own:
            offset = ((j - i) * blk).astype(F32)
            extras = jnp.where(aug_col == j, 1.0,
                               jnp.where((aug_col >= pos_col + 3) & (aug_col < pos_col + 6),
                                         offset, extras.astype(F32))).astype(BF16)
        col_max = []
        for hh in range(n_heads):
            pair = hh * HEAD_DIM // pair_w
            keys = jnp.concatenate([kj[:, pair * pair_w:(pair + 1) * pair_w], extras], axis=1)
            t = jnp.dot(keys, qa_ref[hh], preferred_element_type=F32)
            t_ref[hh] = t
            col_max.append(None if own else jnp.max(t, axis=0, keepdims=True))
        return tuple(col_max)

    def pv(p_ref, j, alpha):
        vj = vt_ref[j]
        for hh in range(n_heads):
            acc_ref[hh] = alpha[hh] * acc_ref[hh] + jnp.dot(vj[v_rows(hh)], p_ref[hh],
                                                            preferred_element_type=F32)

    def softmax_into(t_ref, p_ref, hh, m, cmax):
        causal = cmax is None

        def logits(c):
            t = t_ref[hh, c, :]
            if not causal:
                return t
            key = lax.broadcasted_iota(jnp.int32, t.shape, 0) + c.start
            query = lax.broadcasted_iota(jnp.int32, t.shape, 1)
            return jnp.where(key <= query, t, neg_inf)

        if causal:
            cmax = logits(chunks[0])
            for c in chunks[1:]:
                cmax = jnp.maximum(cmax, logits(c))
            cmax = jnp.max(cmax, axis=0, keepdims=True)
        m_new = cmax if m is None else jnp.maximum(m, cmax)
        for c in chunks:
            p_ref[hh, c, :] = jnp.exp2(logits(c) - m_new).astype(BF16)
        alpha = jnp.ones_like(m_new) if m is None else jnp.exp2(m - m_new)
        return alpha, m_new

    gates = block_gates()
    qk_into(t_b, i, True)
    select_blocks(gates)
    cmax_first = qk_into(t_a, 0, False)
    own = [softmax_into(t_b, p_b, hh, None, None) for hh in range(n_heads)]
    alpha0, m0 = (tuple(x) for x in zip(*own))
    acc_ref[...] = jnp.zeros_like(acc_ref)

    def body(n, carry):
        prev_j, a_prev, m, cmax_a = carry
        j0 = 2 * n
        j1 = jnp.minimum(j0 + 1, last_block)
        cmax_b = qk_into(t_b, j1, False)
        pv(p_b, prev_j, a_prev)
        st0 = [softmax_into(t_a, p_a, hh, m[hh], cmax_a[hh]) for hh in range(n_heads)]
        a0, m = (tuple(x) for x in zip(*st0))
        cmax_a = qk_into(t_a, jnp.minimum(j0 + 2, last_block), False)
        pv(p_a, j0, a0)
        st1 = [softmax_into(t_b, p_b, hh, m[hh], cmax_b[hh]) for hh in range(n_heads)]
        a1, m = (tuple(x) for x in zip(*st1))
        return j1, a1, m, cmax_a

    def double_body(n, carry):
        return body(2 * n + 1, body(2 * n, carry))

    n_pairs = (i + 1) // 2
    carry = lax.fori_loop(0, n_pairs // 2, double_body, (i, alpha0, m0, cmax_first))
    prev_j, a_prev, _, _ = lax.fori_loop(n_pairs // 2 * 2, n_pairs, body, carry)
    pv(p_b, prev_j, a_prev)
    out_t = jnp.concatenate([acc_ref[hh, :HEAD_DIM, :] / acc_ref[hh, HEAD_DIM:HEAD_DIM + 1, :]
                             for hh in range(n_heads)], axis=0)
    o_ref[...] = out_t.T.astype(o_ref.dtype)


def _moba_attention(q, k, v, kmean, slopes, n_batch):
    tokens, width = q.shape
    blk = MOBA_BLOCK
    n_blocks = tokens // n_batch // blk
    step_width = HEADS_PER_STEP * HEAD_DIM
    q_spec = pl.BlockSpec((blk, step_width), lambda b, g, i: (b * n_blocks + i, g))
    kv_spec = pl.BlockSpec((n_blocks, blk, step_width), lambda b, g, i: (b, 0, g),
                           pipeline_mode=pl.Buffered(1))
    by_block = lambda a: a.reshape(tokens // blk, blk, width)
    return pl.pallas_call(
        _moba_kernel,
        grid=(n_batch, width // step_width, n_blocks),
        in_specs=[pl.BlockSpec(memory_space=pltpu.SMEM), q_spec, kv_spec, kv_spec,
                  pl.BlockSpec((n_blocks, step_width), lambda b, g, i: (b, g))],
        out_specs=q_spec,
        out_shape=jax.ShapeDtypeStruct((tokens, width), BF16),
        scratch_shapes=[pltpu.VMEM((2 * HEADS_PER_STEP * n_blocks, step_width), BF16),
                        pltpu.VMEM((blk, MXU_DEPTH // 2), BF16),
                        pltpu.VMEM((HEADS_PER_STEP, MXU_DEPTH, blk), BF16),
                        pltpu.VMEM((n_blocks, HEADS_PER_STEP * V_ROWS, blk), BF16),
                        pltpu.VMEM((HEADS_PER_STEP, blk, blk), F32),
                        pltpu.VMEM((HEADS_PER_STEP, blk, blk), F32),
                        pltpu.VMEM((HEADS_PER_STEP, blk, blk), BF16),
                        pltpu.VMEM((HEADS_PER_STEP, blk, blk), BF16),
                        pltpu.VMEM((HEADS_PER_STEP, V_ROWS, blk), F32)],
        compiler_params=pltpu.CompilerParams(
            dimension_semantics=("arbitrary", "arbitrary", "arbitrary"),
            vmem_limit_bytes=VMEM_LIMIT_BYTES),
        name="moba_attn",
    )(slopes, q, by_block(k), by_block(v), kmean)


def _ssm_matrices(lam_re, lam_im, log_dt, b_re, b_im, c_re, c_im, d_skip):
    T, G, P, Hc = SSM_CHUNK, SSM_GROUPS, SSM_STATE, SSM_GROUP
    dt = jnp.exp(log_dt)[:, None]
    mag = jnp.exp(lam_re * dt)
    ar = mag * jnp.cos(lam_im * dt)
    ai = mag * jnp.sin(lam_im * dt)
    nr, ni = ar - 1.0, ai
    den = lam_re * lam_re + lam_im * lam_im
    fr = (nr * lam_re + ni * lam_im) / den
    fi = (ni * lam_re - nr * lam_im) / den
    bbr = fr[..., None] * b_re - fi[..., None] * b_im
    bbi = fr[..., None] * b_im + fi[..., None] * b_re
    pr, pi = [jnp.ones_like(ar)], [jnp.zeros_like(ar)]
    for _ in range(T):
        pr, pi = pr + [pr[-1] * ar - pi[-1] * ai], pi + [pr[-1] * ai + pi[-1] * ar]
    pw_r, pw_i = jnp.stack(pr), jnp.stack(pi)

    ca_r = c_re[None] * pw_r[:, :, None, :] - c_im[None] * pw_i[:, :, None, :]
    ca_i = c_re[None] * pw_i[:, :, None, :] + c_im[None] * pw_r[:, :, None, :]
    kern = (jnp.einsum('tgop,gpi->tgoi', ca_r[:T], bbr, precision=HIGHEST)
            - jnp.einsum('tgop,gpi->tgoi', ca_i[:T], bbi, precision=HIGHEST))
    skip = d_skip.reshape(G, Hc)[:, :, None] * jnp.eye(Hc, dtype=F32)[None]
    kern = kern.at[0].add(skip)

    GB = SSM_LANE_GROUPS
    CB = G // GB
    k_c = kern.transpose(1, 0, 3, 2).reshape(CB, GB, T, Hc, Hc).transpose(0, 2, 1, 3, 4)
    k_c = jnp.tile(k_c.reshape(CB, T, GB * Hc, Hc), (1, 1, 1, GB))
    rev_r, rev_i = pw_r[T - 1::-1], pw_i[T - 1::-1]
    n_r = rev_r[..., None] * bbr[None] - rev_i[..., None] * bbi[None]
    n_i = rev_r[..., None] * bbi[None] + rev_i[..., None] * bbr[None]
    n_c = jnp.concatenate([n_r, n_i], axis=2).transpose(0, 1, 3, 2)
    n_c = n_c.reshape(T, CB, GB, Hc, 2 * P).transpose(1, 0, 2, 3, 4).reshape(CB, T * LANES, 2 * P)
    m_c = jnp.stack([ca_r[1:], -ca_i[1:]])
    m_c = m_c.reshape(2, T, CB, GB, Hc, P).transpose(2, 0, 3, 5, 1, 4).reshape(
        CB, 2 * GB * P, T * Hc)
    a_chunk = jnp.stack([pw_r[T].reshape(1, G * P), pw_i[T].reshape(1, G * P)])
    return k_c.astype(BF16), n_c.astype(BF16), m_c.astype(BF16), a_chunk


def _div(x, d):
    return x >> (d.bit_length() - 1)


def _mod(x, d):
    return x & (d - 1)


def _spread_block_diag(compact, src_of_col, row_group, col_group):
    n_rows, n_src = compact.shape
    n_cols = n_rows
    src = lax.broadcasted_iota(jnp.int32, (n_src, n_cols), 0)
    col = lax.broadcasted_iota(jnp.int32, (n_src, n_cols), 1)
    spread = jnp.where(src == src_of_col(col), 1.0, 0.0).astype(BF16)
    full = jnp.dot(compact.astype(BF16), spread, preferred_element_type=F32)
    r = lax.broadcasted_iota(jnp.int32, full.shape, 0)
    c = lax.broadcasted_iota(jnp.int32, full.shape, 1)
    return jnp.where(row_group(r) == col_group(c), full, 0.0).astype(BF16)


def _chunk_rows(u_ref):
    n = u_ref.shape[0] // SSM_CHUNK
    return jnp.concatenate([u_ref[pl.ds(s, n, stride=SSM_CHUNK), :] for s in range(SSM_CHUNK)],
                           axis=1).astype(BF16)


def _ssm_kernel(*refs, tiles_per_seq):
    T, GB, P, Hc = SSM_CHUNK, SSM_LANE_GROUPS, SSM_STATE, SSM_GROUP
    nb = SSM_WIDTH // LANES
    u_refs, (k_c_ref, n_c_ref, m_c_ref, a_ref) = refs[:nb], refs[nb:nb + 4]
    y_refs = refs[nb + 4:2 * nb + 4]
    n_s, w_s, m_s, er_s, ei_s, hr_s, hi_s, cr_s, ci_s = refs[2 * nb + 4:]
    half = GB * P
    step = pl.program_id(0)

    @pl.when(step == 0)
    def _():
        r = lax.broadcasted_iota(jnp.int32, (LANES, LANES), 0)
        c = lax.broadcasted_iota(jnp.int32, (LANES, LANES), 1)
        same_group = _div(r, Hc) == _div(c, Hc)
        for cb in range(nb):
            n_s[cb] = _spread_block_diag(
                n_c_ref[cb], lambda c: _div(c, GB * P) * P + _mod(c, P),
                lambda r: _mod(_div(r, Hc), GB), lambda c: _mod(_div(c, P), GB))
            w_s[cb] = jnp.zeros(w_s.shape[1:], BF16)
            for tau in range(T):
                blk = jnp.where(same_group, k_c_ref[cb, tau], 0.0).astype(BF16)
                for s in range(T - tau):
                    w_s[cb, s * LANES:(s + 1) * LANES,
                        (s + tau) * LANES:(s + tau + 1) * LANES] = blk
            m_s[cb] = _spread_block_diag(
                m_c_ref[cb], lambda c: _div(c, LANES) * Hc + _mod(c, Hc),
                lambda r: _mod(_div(r, P), GB), lambda c: _mod(_div(c, Hc), GB))

    @pl.when(step % tiles_per_seq == 0)
    def _():
        cr_s[...] = jnp.zeros_like(cr_s)
        ci_s[...] = jnp.zeros_like(ci_s)

    u_rows = [_chunk_rows(u_ref) for u_ref in u_refs]
    for cb in range(nb):
        e = jnp.dot(u_rows[cb], n_s[cb], preferred_element_type=F32)
        er_s[:, cb * half:(cb + 1) * half] = e[:, :half]
        ei_s[:, cb * half:(cb + 1) * half] = e[:, half:]

    ar = a_ref[0]
    ai = a_ref[1]

    def scan_row(c, carry):
        hr, hi = carry
        hr_s[pl.ds(c, 1), :] = hr
        hi_s[pl.ds(c, 1), :] = hi
        er = er_s[pl.ds(c, 1), :]
        ei = ei_s[pl.ds(c, 1), :]
        return ar * hr - ai * hi + er, ar * hi + ai * hr + ei

    hr, hi = lax.fori_loop(0, er_s.shape[0], scan_row, (cr_s[...], ci_s[...]))
    cr_s[...] = hr
    ci_s[...] = hi

    n = er_s.shape[0]
    for cb in range(nb):
        cols = slice(cb * half, (cb + 1) * half)
        h = jnp.concatenate([hr_s[:, cols], hi_s[:, cols]], axis=1).astype(BF16)
        y = (jnp.dot(u_rows[cb], w_s[cb], preferred_element_type=F32)
             + jnp.dot(h, m_s[cb], preferred_element_type=F32))
        for t in range(T):
            y_refs[cb][pl.ds(t, n, stride=T), :] = y[:, t * LANES:(t + 1) * LANES]


def _s5_ssm(u, n_batch, ops):
    k_c, n_c, m_c, a_chunk = ops
    tokens = u.shape[0]
    T = SSM_CHUNK
    nb = SSM_WIDTH // LANES
    rt = SSM_ROW_TILE
    ct = rt // T
    op_dim = T * LANES
    n_state = SSM_GROUPS * SSM_STATE
    assert (tokens // n_batch) % rt == 0
    u_specs = [pl.BlockSpec((rt, LANES), functools.partial(lambda cb, r: (r, cb), cb))
               for cb in range(nb)]
    y_spec = pl.BlockSpec((rt, LANES), lambda r: (r, 0))
    return pl.pallas_call(
        functools.partial(_ssm_kernel, tiles_per_seq=tokens // n_batch // rt),
        grid=(tokens // rt,),
        in_specs=u_specs + [_const_spec(k_c.shape), _const_spec(n_c.shape),
                            _const_spec(m_c.shape), _const_spec(a_chunk.shape)],
        out_specs=[y_spec] * nb,
        out_shape=[jax.ShapeDtypeStruct((tokens, LANES), F32)] * nb,
        scratch_shapes=[pltpu.VMEM((nb, op_dim, op_dim), BF16)] * 3
        + [pltpu.VMEM((ct, n_state), F32)] * 4 + [pltpu.VMEM((1, n_state), F32)] * 2,
        compiler_params=pltpu.CompilerParams(dimension_semantics=("arbitrary",),
                                             vmem_limit_bytes=VMEM_LIMIT_BYTES),
        name="ssm",
    )(*([u] * nb), k_c, n_c, m_c, a_chunk)


def _post_kernel(x1_ref, attn_ref, *refs):
    n_y = SSM_WIDTH // LANES
    y_refs = refs[:n_y]
    (ga_ref, gs_ref, wup_ref, wglu_ref, wout_ref, n3_ref, wg_ref, wu_ref, wd_ref, nf_ref,
     o_ref) = refs[n_y:]
    attn = jnp.dot(attn_ref[...], wup_ref[...], preferred_element_type=F32)
    y = jnp.concatenate([y_ref[...] for y_ref in y_refs], axis=1)
    y = jax.nn.gelu(y, approximate=True).astype(BF16)
    glu = jnp.dot(y, wglu_ref[...], preferred_element_type=F32)
    ssm_out = glu[:, :D_MODEL] * jax.nn.sigmoid(glu[:, D_MODEL:])
    merged = jax.nn.sigmoid(ga_ref[...]) * attn + jax.nn.sigmoid(gs_ref[...]) * ssm_out
    x2 = x1_ref[...] + jnp.dot(merged.astype(BF16), wout_ref[...], preferred_element_type=F32)
    x3 = _swiglu_half_step(x2, n3_ref[...], wg_ref, wu_ref, wd_ref)
    o_ref[...] = _rms(x3, nf_ref[...])


def _post(x1, attn, ys, ga, gs, wup, wglu, wout, n3, wg, wu, wd, nf):
    tokens = x1.shape[0]
    tm = TOKEN_TILE
    row = lambda w: pl.BlockSpec((tm, w), lambda i: (i, 0))
    consts = (wup, wglu, wout, n3, wg, wu, wd, nf)
    return pl.pallas_call(
        _post_kernel,
        grid=(tokens // tm,),
        in_specs=[row(D_MODEL), row(ATTN_WIDTH)] + [row(LANES)] * len(ys)
        + [row(D_MODEL), row(D_MODEL)] + [_const_spec(c.shape) for c in consts],
        out_specs=row(D_MODEL),
        out_shape=jax.ShapeDtypeStruct((tokens, D_MODEL), F32),
        compiler_params=pltpu.CompilerParams(
            dimension_semantics=("arbitrary",), vmem_limit_bytes=VMEM_LIMIT_BYTES),
        name="post",
    )(x1, attn, *ys, ga, gs, *consts)


def kernel(x, ffn1_norm, ffn1_w_gate, ffn1_w_up, ffn1_w_down, mix_norm, w_in, w_attn_up, ssm_lambda_re, ssm_lambda_im, ssm_log_dt, ssm_b_re, ssm_b_im, ssm_c_re, ssm_c_im, ssm_d, w_ssm_glu, w_out, ffn2_norm, ffn2_w_gate, ffn2_w_up, ffn2_w_down, final_norm):
    B, S, D = x.shape
    depth = ffn1_norm.shape[0]
    assert depth == 1, "final norm is fused into the single layer's last stage"
    tokens = B * S
    slopes = jnp.asarray(2.0 ** (-8.0 * np.arange(1, N_HEADS + 1) / N_HEADS), dtype=F32)
    bf = lambda w: w.astype(BF16)
    xt = x.reshape(tokens, D)
    for l in range(depth):
        x1, q, k, v, u, ga, gs, kmean = _ffn1_proj(
            xt, ffn1_norm[l][None], bf(ffn1_w_gate[l]), bf(ffn1_w_up[l]), bf(ffn1_w_down[l]),
            mix_norm[l][None], bf(w_in[l]))
        attn = _moba_attention(q, k, v, kmean.reshape(-1, ATTN_WIDTH), slopes, B)
        mats = _ssm_matrices(ssm_lambda_re[l], ssm_lambda_im[l], ssm_log_dt[l], ssm_b_re[l],
                             ssm_b_im[l], ssm_c_re[l], ssm_c_im[l], ssm_d[l])
        ys = _s5_ssm(u, B, mats)
        xt = _post(x1, attn, ys, ga, gs, bf(w_attn_up[l]), bf(w_ssm_glu[l]), bf(w_out[l]),
                   ffn2_norm[l][None], bf(ffn2_w_gate[l]), bf(ffn2_w_up[l]), bf(ffn2_w_down[l]),
                   final_norm[None])
    return xt.reshape(B, S, D)
```

```python
import functools
import math

import numpy as np
import jax
import jax.numpy as jnp
from jax import lax
from jax.experimental import pallas as pl
from jax.experimental.pallas import tpu as pltpu

F32 = jnp.float32
BF16 = jnp.bfloat16
HIGHEST = lax.Precision.HIGHEST

D_MODEL = 1024
N_HEADS = 8
HEAD_DIM = 64
ATTN_WIDTH = N_HEADS * HEAD_DIM
MOBA_BLOCK = 256
MOBA_TOPK = 3
SSM_GROUP = 16
SSM_GROUPS = 32
SSM_WIDTH = SSM_GROUP * SSM_GROUPS
SSM_STATE = 64
D_FF = 2816
EPS = 1e-6
LOG2_E = math.log2(math.e)

VMEM_LIMIT_BYTES = 56 * 1024 * 1024
TOKEN_TILE = MOBA_BLOCK
ROW_SPLITS = 2
HEADS_PER_STEP = 8
SOFTMAX_ROWS = 32
MXU_DEPTH = 256
BF16_ROWS = 16
V_ROWS = HEAD_DIM + BF16_ROWS
MASKED = -1e30
LANES = 128
SSM_CHUNK = 8
SSM_LANE_GROUPS = LANES // SSM_GROUP
SSM_ROW_TILE = 2048


def _const_spec(shape):
    nd = len(shape)
    return pl.BlockSpec(shape, lambda *_: (0,) * nd, pipeline_mode=pl.Buffered(1))


def _rms(x, g):
    return x * lax.rsqrt(jnp.mean(x * x, axis=-1, keepdims=True) + EPS) * g


def _row_groups(n_rows):
    rows = n_rows // ROW_SPLITS
    return [slice(g * rows, (g + 1) * rows) for g in range(ROW_SPLITS)]


def _swiglu_half_step(xs, g_norm, wg_ref, wu_ref, wd_ref):
    gate, up = [], []
    for x in xs:
        h = _rms(x, g_norm).astype(BF16)
        gate.append(jnp.dot(h, wg_ref[...], preferred_element_type=F32))
        up.append(jnp.dot(h, wu_ref[...], preferred_element_type=F32))
    out = []
    for x, g, u in zip(xs, gate, up):
        act = (g * jax.nn.sigmoid(g) * u).astype(BF16)
        out.append(x + 0.5 * jnp.dot(act, wd_ref[...], preferred_element_type=F32))
    return out


def _ffn1_proj_kernel(x_ref, n1_ref, wg_ref, wu_ref, wd_ref, n2_ref, win_ref,
                      x1_ref, q_ref, k_ref, v_ref, u_ref, ga_ref, gs_ref, kmean_ref):
    groups = _row_groups(x_ref.shape[0])
    x1 = _swiglu_half_step([x_ref[r, :] for r in groups], n1_ref[...], wg_ref, wu_ref, wd_ref)
    hs = []
    for r, x1_g in zip(groups, x1):
        x1_ref[r, :] = x1_g
        hs.append(_rms(x1_g, n2_ref[...]).astype(BF16))
    k_sum = None
    col = 0
    for ref in (q_ref, k_ref, v_ref, u_ref, ga_ref, gs_ref):
        width = ref.shape[-1]
        for r, h in zip(groups, hs):
            y = jnp.dot(h, win_ref[:, col:col + width], preferred_element_type=F32)
            ref[r, :] = y.astype(ref.dtype)
            if ref is k_ref:
                part = jnp.sum(y, axis=0, keepdims=True)
                k_sum = part if k_sum is None else k_sum + part
        col += width
    kmean_ref[...] = k_sum / x_ref.shape[0]


def _ffn1_proj(x, n1, wg, wu, wd, n2, win):
    tokens = x.shape[0]
    tm = TOKEN_TILE
    assert tm == MOBA_BLOCK
    row = lambda w: pl.BlockSpec((tm, w), lambda i: (i, 0))
    widths = (ATTN_WIDTH, ATTN_WIDTH, ATTN_WIDTH, SSM_WIDTH, D_MODEL, D_MODEL)
    dtypes = (F32, BF16, BF16, F32, F32, F32)
    return pl.pallas_call(
        _ffn1_proj_kernel,
        grid=(tokens // tm,),
        in_specs=[row(D_MODEL), _const_spec(n1.shape), _const_spec(wg.shape),
                  _const_spec(wu.shape), _const_spec(wd.shape), _const_spec(n2.shape),
                  _const_spec(win.shape)],
        out_specs=[row(D_MODEL)] + [row(w) for w in widths]
        + [pl.BlockSpec((None, 1, ATTN_WIDTH), lambda i: (i, 0, 0))],
        out_shape=[jax.ShapeDtypeStruct((tokens, D_MODEL), F32)]
        + [jax.ShapeDtypeStruct((tokens, w), dt) for w, dt in zip(widths, dtypes)]
        + [jax.ShapeDtypeStruct((tokens // tm, 1, ATTN_WIDTH), F32)],
        compiler_params=pltpu.CompilerParams(
            dimension_semantics=("arbitrary",), vmem_limit_bytes=VMEM_LIMIT_BYTES),
        name="ffn1_proj",
    )(x, n1, wg, wu, wd, n2, win)


def _split3(x):
    hi = x.astype(BF16).astype(F32)
    mid = (x - hi).astype(BF16).astype(F32)
    return hi, mid, x - hi - mid


def _moba_kernel(slopes_ref, q_ref, k_ref, v_ref, kmean_ref, o_ref, kms_ref, kaug_ref, qa_ref,
                 vt_ref, t_a, t_b, p_a, p_b, acc_ref):
    hg = pl.program_id(1)
    i = pl.program_id(2)
    blk = MOBA_BLOCK
    n_blocks = k_ref.shape[0]
    n_heads = HEADS_PER_STEP
    pair_w = MXU_DEPTH // 2
    pos_col = n_blocks
    neg_inf = -jnp.inf
    slopes = [slopes_ref[hg * n_heads + hh] * LOG2_E for hh in range(n_heads)]
    v_rows = lambda hh: slice(hh * V_ROWS, (hh + 1) * V_ROWS)
    aug_id = lax.broadcasted_iota(jnp.int32, (BF16_ROWS, blk), 0)
    pair_row_head = lax.broadcasted_iota(jnp.int32, (pair_w, blk), 0) // HEAD_DIM
    aug_col = lax.broadcasted_iota(jnp.int32, kaug_ref.shape, 1)

    @pl.when(i == 0)
    def _():
        kmean = kmean_ref[...]
        lane_head = lax.broadcasted_iota(jnp.int32, kmean.shape, 1) // HEAD_DIM
        km = jnp.concatenate([jnp.where(lane_head == hh, kmean, 0.0) for hh in range(n_heads)],
                             axis=0)
        km_hi = km.astype(BF16)
        kms_ref[...] = jnp.concatenate([km_hi, (km - km_hi.astype(F32)).astype(BF16)], axis=0)
        key = lax.broadcasted_iota(jnp.int32, kaug_ref.shape, 0).astype(F32)
        kaug_ref[...] = jnp.where((aug_col >= pos_col) & (aug_col < pos_col + 3), key,
                                  0.0).astype(BF16)
        ones_row = jnp.where(aug_id == 0, 1.0, 0.0).astype(BF16)

        def transpose_values(j, _):
            vt = v_ref[j].astype(F32).T.astype(BF16)
            for hh in range(n_heads):
                vt_ref[j, hh * V_ROWS:hh * V_ROWS + HEAD_DIM, :] = vt[hh * HEAD_DIM:
                                                                      (hh + 1) * HEAD_DIM]
                vt_ref[j, hh * V_ROWS + HEAD_DIM:(hh + 1) * V_ROWS, :] = ones_row
            return _
        lax.fori_loop(0, n_blocks, transpose_values, 0)

    qt = q_ref[...].T

    for hh in range(n_heads):
        pair = hh * HEAD_DIM // pair_w
        in_pair = slice(pair * pair_w, (pair + 1) * pair_w)
        qa_ref[hh, :pair_w, :] = (jnp.where(pair_row_head == hh - pair * (pair_w // HEAD_DIM),
                                            qt[in_pair], 0.0)
                                  * (HEAD_DIM ** -0.5 * LOG2_E)).astype(BF16)
        qa_ref[hh, pair_w:pair_w + n_blocks, :] = jnp.zeros((n_blocks, blk), BF16)
        pieces = _split3(jnp.full((1, blk), slopes[hh], F32)) * 2
        tile = jnp.zeros((BF16_ROWS, blk), F32)
        for n, piece in enumerate(pieces):
            tile = jnp.where(aug_id == n, piece, tile)
        qa_ref[hh, pair_w + pos_col:pair_w + pos_col + BF16_ROWS, :] = tile.astype(BF16)
        qa_ref[hh, pair_w + pos_col + BF16_ROWS:, :] = jnp.zeros(
            (MXU_DEPTH - pair_w - pos_col - BF16_ROWS, blk), BF16)

    def block_gates():
        qt_hi = qt.astype(BF16)
        qt_lo = (qt - qt_hi.astype(F32)).astype(BF16)
        gate_hi = jnp.dot(kms_ref[...], qt_hi, preferred_element_type=F32)
        return (gate_hi[:n_heads * n_blocks] + gate_hi[n_heads * n_blocks:]
                + jnp.dot(kms_ref[:n_heads * n_blocks, :], qt_lo, preferred_element_type=F32))

    def select_blocks(gates):
        shape = (n_heads, n_blocks, blk)
        blk_id = lax.broadcasted_iota(jnp.int32, shape, 1).astype(F32)
        g = jnp.where(blk_id < i.astype(F32), gates.reshape(shape), neg_inf)
        sel = jnp.zeros(shape, jnp.bool_)
        for _ in range(MOBA_TOPK):
            top = jnp.max(g, axis=1, keepdims=True)
            first = jnp.min(jnp.where(g == top, blk_id, float(n_blocks)), axis=1, keepdims=True)
            pick = (blk_id == first) & (top > neg_inf)
            sel = sel | pick
            g = jnp.where(pick, neg_inf, g)
        mask_rows = jnp.where(sel, 0.0, MASKED).astype(BF16)
        for hh in range(n_heads):
            qa_ref[hh, pair_w:pair_w + n_blocks, :] = mask_rows[hh]

    last_block = n_blocks - 1
    chunks = [slice(c, c + SOFTMAX_ROWS) for c in range(0, blk, SOFTMAX_ROWS)]

    def qk_into(t_ref, j, own):
        kj = k_ref[j]
        extras = kaug_ref[...]
        if not own:
            offset = ((j - i) * blk).astype(F32)
            extras = jnp.where(aug_col == j, 1.0,
                               jnp.where((aug_col >= pos_col + 3) & (aug_col < pos_col + 6),
                                         offset, extras.astype(F32))).astype(BF16)
        col_max = []
        for hh in range(n_heads):
            pair = hh * HEAD_DIM // pair_w
            keys = jnp.concatenate([kj[:, pair * pair_w:(pair + 1) * pair_w], extras], axis=1)
            t = jnp.dot(keys, qa_ref[hh], preferred_element_type=F32)
            t_ref[hh] = t
            col_max.append(None if own else jnp.max(t, axis=0, keepdims=True))
        return tuple(col_max)

    def pv(p_ref, j, alpha):
        vj = vt_ref[j]
        for hh in range(n_heads):
            acc_ref[hh] = alpha[hh] * acc_ref[hh] + jnp.dot(vj[v_rows(hh)], p_ref[hh],
                                                            preferred_element_type=F32)

    def softmax_into(t_ref, p_ref, hh, m, cmax):
        causal = cmax is None

        def logits(c):
            t = t_ref[hh, c, :]
            if not causal:
                return t
            key = lax.broadcasted_iota(jnp.int32, t.shape, 0) + c.start
            query = lax.broadcasted_iota(jnp.int32, t.shape, 1)
            return jnp.where(key <= query, t, neg_inf)

        if causal:
            cmax = logits(chunks[0])
            for c in chunks[1:]:
                cmax = jnp.maximum(cmax, logits(c))
            cmax = jnp.max(cmax, axis=0, keepdims=True)
        m_new = cmax if m is None else jnp.maximum(m, cmax)
        for c in chunks:
            p_ref[hh, c, :] = jnp.exp2(logits(c) - m_new).astype(BF16)
        alpha = jnp.ones_like(m_new) if m is None else jnp.exp2(m - m_new)
        return alpha, m_new

    gates = block_gates()
    qk_into(t_b, i, True)
    select_blocks(gates)
    cmax_first = qk_into(t_a, 0, False)
    own = [softmax_into(t_b, p_b, hh, None, None) for hh in range(n_heads)]
    alpha0, m0 = (tuple(x) for x in zip(*own))
    acc_ref[...] = jnp.zeros_like(acc_ref)

    def body(n, carry):
        prev_j, a_prev, m, cmax_a = carry
        j0 = 2 * n
        j1 = jnp.minimum(j0 + 1, last_block)
        cmax_b = qk_into(t_b, j1, False)
        pv(p_b, prev_j, a_prev)
        st0 = [softmax_into(t_a, p_a, hh, m[hh], cmax_a[hh]) for hh in range(n_heads)]
        a0, m = (tuple(x) for x in zip(*st0))
        cmax_a = qk_into(t_a, jnp.minimum(j0 + 2, last_block), False)
        pv(p_a, j0, a0)
        st1 = [softmax_into(t_b, p_b, hh, m[hh], cmax_b[hh]) for hh in range(n_heads)]
        a1, m = (tuple(x) for x in zip(*st1))
        return j1, a1, m, cmax_a

    def double_body(n, carry):
        return body(2 * n + 1, body(2 * n, carry))

    n_pairs = (i + 1) // 2
    carry = lax.fori_loop(0, n_pairs // 2, double_body, (i, alpha0, m0, cmax_first))
    prev_j, a_prev, _, _ = lax.fori_loop(n_pairs // 2 * 2, n_pairs, body, carry)
    pv(p_b, prev_j, a_prev)
    out_t = jnp.concatenate([acc_ref[hh, :HEAD_DIM, :] / acc_ref[hh, HEAD_DIM:HEAD_DIM + 1, :]
                             for hh in range(n_heads)], axis=0)
    o_ref[...] = out_t.T.astype(o_ref.dtype)


def _moba_attention(q, k, v, kmean, slopes, n_batch):
    tokens, width = q.shape
    blk = MOBA_BLOCK
    n_blocks = tokens // n_batch // blk
    step_width = HEADS_PER_STEP * HEAD_DIM
    q_spec = pl.BlockSpec((blk, step_width), lambda b, g, i: (b * n_blocks + i, g))
    kv_spec = pl.BlockSpec((n_blocks, blk, step_width), lambda b, g, i: (b, 0, g),
                           pipeline_mode=pl.Buffered(1))
    by_block = lambda a: a.reshape(tokens // blk, blk, width)
    return pl.pallas_call(
        _moba_kernel,
        grid=(n_batch, width // step_width, n_blocks),
        in_specs=[pl.BlockSpec(memory_space=pltpu.SMEM), q_spec, kv_spec, kv_spec,
                  pl.BlockSpec((n_blocks, step_width), lambda b, g, i: (b, g))],
        out_specs=q_spec,
        out_shape=jax.ShapeDtypeStruct((tokens, width), BF16),
        scratch_shapes=[pltpu.VMEM((2 * HEADS_PER_STEP * n_blocks, step_width), BF16),
                        pltpu.VMEM((blk, MXU_DEPTH // 2), BF16),
                        pltpu.VMEM((HEADS_PER_STEP, MXU_DEPTH, blk), BF16),
                        pltpu.VMEM((n_blocks, HEADS_PER_STEP * V_ROWS, blk), BF16),
                        pltpu.VMEM((HEADS_PER_STEP, blk, blk), F32),
                        pltpu.VMEM((HEADS_PER_STEP, blk, blk), F32),
                        pltpu.VMEM((HEADS_PER_STEP, blk, blk), BF16),
                        pltpu.VMEM((HEADS_PER_STEP, blk, blk), BF16),
                        pltpu.VMEM((HEADS_PER_STEP, V_ROWS, blk), F32)],
        compiler_params=pltpu.CompilerParams(
            dimension_semantics=("arbitrary", "arbitrary", "arbitrary"),
            vmem_limit_bytes=VMEM_LIMIT_BYTES),
        name="moba_attn",
    )(slopes, q, by_block(k), by_block(v), kmean)


def _ssm_matrices(lam_re, lam_im, log_dt, b_re, b_im, c_re, c_im, d_skip):
    T, G, P, Hc = SSM_CHUNK, SSM_GROUPS, SSM_STATE, SSM_GROUP
    dt = jnp.exp(log_dt)[:, None]
    mag = jnp.exp(lam_re * dt)
    ar = mag * jnp.cos(lam_im * dt)
    ai = mag * jnp.sin(lam_im * dt)
    nr, ni = ar - 1.0, ai
    den = lam_re * lam_re + lam_im * lam_im
    fr = (nr * lam_re + ni * lam_im) / den
    fi = (ni * lam_re - nr * lam_im) / den
    bbr = fr[..., None] * b_re - fi[..., None] * b_im
    bbi = fr[..., None] * b_im + fi[..., None] * b_re
    pr, pi = [jnp.ones_like(ar)], [jnp.zeros_like(ar)]
    for _ in range(T):
        pr, pi = pr + [pr[-1] * ar - pi[-1] * ai], pi + [pr[-1] * ai + pi[-1] * ar]
    pw_r, pw_i = jnp.stack(pr), jnp.stack(pi)

    ca_r = c_re[None] * pw_r[:, :, None, :] - c_im[None] * pw_i[:, :, None, :]
    ca_i = c_re[None] * pw_i[:, :, None, :] + c_im[None] * pw_r[:, :, None, :]
    kern = jnp.sum(ca_r[:T, :, :, :, None] * bbr[None, :, None, :, :]
                   - ca_i[:T, :, :, :, None] * bbi[None, :, None, :, :], axis=3)
    skip = d_skip.reshape(G, Hc)[:, :, None] * jnp.eye(Hc, dtype=F32)[None]
    kern = kern.at[0].add(skip)

    GB = SSM_LANE_GROUPS
    CB = G // GB
    k_c = kern.transpose(1, 0, 3, 2).reshape(CB, GB, T, Hc, Hc).transpose(0, 2, 1, 3, 4)
    k_c = jnp.tile(k_c.reshape(CB, T, GB * Hc, Hc), (1, 1, 1, GB))
    rev_r, rev_i = pw_r[T - 1::-1], pw_i[T - 1::-1]
    n_r = rev_r[..., None] * bbr[None] - rev_i[..., None] * bbi[None]
    n_i = rev_r[..., None] * bbi[None] + rev_i[..., None] * bbr[None]
    n_c = jnp.concatenate([n_r, n_i], axis=2).transpose(0, 1, 3, 2)
    n_c = n_c.reshape(T, CB, GB, Hc, 2 * P).transpose(1, 0, 2, 3, 4).reshape(CB, T * LANES, 2 * P)
    m_c = jnp.stack([ca_r[1:], -ca_i[1:]])
    m_c = m_c.reshape(2, T, CB, GB, Hc, P).transpose(2, 0, 3, 5, 1, 4).reshape(
        CB, 2 * GB * P, T * Hc)
    a_chunk = jnp.stack([pw_r[T].reshape(1, G * P), pw_i[T].reshape(1, G * P)])
    return k_c.astype(BF16), n_c.astype(BF16), m_c.astype(BF16), a_chunk


def _div(x, d):
    return x >> (d.bit_length() - 1)


def _mod(x, d):
    return x & (d - 1)


def _spread_block_diag(compact, src_of_col, row_group, col_group):
    n_rows, n_src = compact.shape
    n_cols = n_rows
    src = lax.broadcasted_iota(jnp.int32, (n_src, n_cols), 0)
    col = lax.broadcasted_iota(jnp.int32, (n_src, n_cols), 1)
    spread = jnp.where(src == src_of_col(col), 1.0, 0.0).astype(BF16)
    full = jnp.dot(compact.astype(BF16), spread, preferred_element_type=F32)
    r = lax.broadcasted_iota(jnp.int32, full.shape, 0)
    c = lax.broadcasted_iota(jnp.int32, full.shape, 1)
    return jnp.where(row_group(r) == col_group(c), full, 0.0).astype(BF16)


def _chunk_rows(u_ref):
    n = u_ref.shape[0] // SSM_CHUNK
    return jnp.concatenate([u_ref[pl.ds(s, n, stride=SSM_CHUNK), :] for s in range(SSM_CHUNK)],
                           axis=1).astype(BF16)


def _ssm_kernel(*refs, tiles_per_seq):
    T, GB, P, Hc = SSM_CHUNK, SSM_LANE_GROUPS, SSM_STATE, SSM_GROUP
    nb = SSM_WIDTH // LANES
    u_refs, (k_c_ref, n_c_ref, m_c_ref, a_ref) = refs[:nb], refs[nb:nb + 4]
    y_refs = refs[nb + 4:2 * nb + 4]
    n_s, w_s, m_s, er_s, ei_s, hr_s, hi_s, cr_s, ci_s = refs[2 * nb + 4:]
    half = GB * P
    step = pl.program_id(0)

    @pl.when(step == 0)
    def _():
        r = lax.broadcasted_iota(jnp.int32, (LANES, LANES), 0)
        c = lax.broadcasted_iota(jnp.int32, (LANES, LANES), 1)
        same_group = _div(r, Hc) == _div(c, Hc)
        for cb in range(nb):
            n_s[cb] = _spread_block_diag(
                n_c_ref[cb], lambda c: _div(c, GB * P) * P + _mod(c, P),
                lambda r: _mod(_div(r, Hc), GB), lambda c: _mod(_div(c, P), GB))
            w_s[cb] = jnp.zeros(w_s.shape[1:], BF16)
            for tau in range(T):
                blk = jnp.where(same_group, k_c_ref[cb, tau], 0.0).astype(BF16)
                for s in range(T - tau):
                    w_s[cb, s * LANES:(s + 1) * LANES,
                        (s + tau) * LANES:(s + tau + 1) * LANES] = blk
            m_s[cb] = _spread_block_diag(
                m_c_ref[cb], lambda c: _div(c, LANES) * Hc + _mod(c, Hc),
                lambda r: _mod(_div(r, P), GB), lambda c: _mod(_div(c, Hc), GB))

    @pl.when(step % tiles_per_seq == 0)
    def _():
        cr_s[...] = jnp.zeros_like(cr_s)
        ci_s[...] = jnp.zeros_like(ci_s)

    u_rows = [_chunk_rows(u_ref) for u_ref in u_refs]
    for cb in range(nb):
        e = jnp.dot(u_rows[cb], n_s[cb], preferred_element_type=F32)
        er_s[:, cb * half:(cb + 1) * half] = e[:, :half]
        ei_s[:, cb * half:(cb + 1) * half] = e[:, half:]

    ar = a_ref[0]
    ai = a_ref[1]

    def scan_row(c, carry):
        hr, hi = carry
        hr_s[pl.ds(c, 1), :] = hr
        hi_s[pl.ds(c, 1), :] = hi
        er = er_s[pl.ds(c, 1), :]
        ei = ei_s[pl.ds(c, 1), :]
        return ar * hr - ai * hi + er, ar * hi + ai * hr + ei

    hr, hi = lax.fori_loop(0, er_s.shape[0], scan_row, (cr_s[...], ci_s[...]))
    cr_s[...] = hr
    ci_s[...] = hi

    n = er_s.shape[0]
    for cb in range(nb):
        cols = slice(cb * half, (cb + 1) * half)
        h = jnp.concatenate([hr_s[:, cols], hi_s[:, cols]], axis=1).astype(BF16)
        y = (jnp.dot(u_rows[cb], w_s[cb], preferred_element_type=F32)
             + jnp.dot(h, m_s[cb], preferred_element_type=F32))
        for t in range(T):
            y_refs[cb][pl.ds(t, n, stride=T), :] = y[:, t * LANES:(t + 1) * LANES]


def _s5_ssm(u, n_batch, ops):
    k_c, n_c, m_c, a_chunk = ops
    tokens = u.shape[0]
    T = SSM_CHUNK
    nb = SSM_WIDTH // LANES
    rt = SSM_ROW_TILE
    ct = rt // T
    op_dim = T * LANES
    n_state = SSM_GROUPS * SSM_STATE
    assert (tokens // n_batch) % rt == 0
    u_specs = [pl.BlockSpec((rt, LANES), functools.partial(lambda cb, r: (r, cb), cb))
               for cb in range(nb)]
    y_spec = pl.BlockSpec((rt, LANES), lambda r: (r, 0))
    return pl.pallas_call(
        functools.partial(_ssm_kernel, tiles_per_seq=tokens // n_batch // rt),
        grid=(tokens // rt,),
        in_specs=u_specs + [_const_spec(k_c.shape), _const_spec(n_c.shape),
                            _const_spec(m_c.shape), _const_spec(a_chunk.shape)],
        out_specs=[y_spec] * nb,
        out_shape=[jax.ShapeDtypeStruct((tokens, LANES), F32)] * nb,
        scratch_shapes=[pltpu.VMEM((nb, op_dim, op_dim), BF16)] * 3
        + [pltpu.VMEM((ct, n_state), F32)] * 4 + [pltpu.VMEM((1, n_state), F32)] * 2,
        compiler_params=pltpu.CompilerParams(dimension_semantics=("arbitrary",),
                                             vmem_limit_bytes=VMEM_LIMIT_BYTES),
        name="ssm",
    )(*([u] * nb), k_c, n_c, m_c, a_chunk)


def _post_kernel(x1_ref, attn_ref, *refs):
    n_y = SSM_WIDTH // LANES
    y_refs = refs[:n_y]
    (ga_ref, gs_ref, wup_ref, wglu_ref, wout_ref, n3_ref, wg_ref, wu_ref, wd_ref, nf_ref,
     o_ref) = refs[n_y:]
    groups = _row_groups(x1_ref.shape[0])
    dot = functools.partial(jnp.dot, preferred_element_type=F32)
    attn = [dot(attn_ref[r, :], wup_ref[...]) for r in groups]
    glu = []
    for r in groups:
        y = jnp.concatenate([y_ref[r, :] for y_ref in y_refs], axis=1)
        glu.append(dot(jax.nn.gelu(y, approximate=True).astype(BF16), wglu_ref[...]))
    x2 = []
    for g, r in enumerate(groups):
        ssm_out = glu[g][:, :D_MODEL] * jax.nn.sigmoid(glu[g][:, D_MODEL:])
        merged = (jax.nn.sigmoid(ga_ref[r, :]) * attn[g]
                  + jax.nn.sigmoid(gs_ref[r, :]) * ssm_out)
        x2.append(x1_ref[r, :] + dot(merged.astype(BF16), wout_ref[...]))
    x3 = _swiglu_half_step(x2, n3_ref[...], wg_ref, wu_ref, wd_ref)
    for r, x3_g in zip(groups, x3):
        o_ref[r, :] = _rms(x3_g, nf_ref[...])


def _post(x1, attn, ys, ga, gs, wup, wglu, wout, n3, wg, wu, wd, nf):
    tokens = x1.shape[0]
    tm = TOKEN_TILE
    row = lambda w: pl.BlockSpec((tm, w), lambda i: (i, 0))
    consts = (wup, wglu, wout, n3, wg, wu, wd, nf)
    return pl.pallas_call(
        _post_kernel,
        grid=(tokens // tm,),
        in_specs=[row(D_MODEL), row(ATTN_WIDTH)] + [row(LANES)] * len(ys)
        + [row(D_MODEL), row(D_MODEL)] + [_const_spec(c.shape) for c in consts],
        out_specs=row(D_MODEL),
        out_shape=jax.ShapeDtypeStruct((tokens, D_MODEL), F32),
        compiler_params=pltpu.CompilerParams(
            dimension_semantics=("arbitrary",), vmem_limit_bytes=VMEM_LIMIT_BYTES),
        name="post",
    )(x1, attn, *ys, ga, gs, *consts)


def kernel(x, ffn1_norm, ffn1_w_gate, ffn1_w_up, ffn1_w_down, mix_norm, w_in, w_attn_up, ssm_lambda_re, ssm_lambda_im, ssm_log_dt, ssm_b_re, ssm_b_im, ssm_c_re, ssm_c_im, ssm_d, w_ssm_glu, w_out, ffn2_norm, ffn2_w_gate, ffn2_w_up, ffn2_w_down, final_norm):
    B, S, D = x.shape
    depth = ffn1_norm.shape[0]
    assert depth == 1, "final norm is fused into the single layer's last stage"
    tokens = B * S
    slopes = jnp.asarray(2.0 ** (-8.0 * np.arange(1, N_HEADS + 1) / N_HEADS), dtype=F32)
    bf = lambda w: w.astype(BF16)
    xt = x.reshape(tokens, D)
    for l in range(depth):
        x1, q, k, v, u, ga, gs, kmean = _ffn1_proj(
            xt, ffn1_norm[l][None], bf(ffn1_w_gate[l]), bf(ffn1_w_up[l]), bf(ffn1_w_down[l]),
            mix_norm[l][None], bf(w_in[l]))
        attn = _moba_attention(q, k, v, kmean.reshape(-1, ATTN_WIDTH), slopes, B)
        mats = _ssm_matrices(ssm_lambda_re[l], ssm_lambda_im[l], ssm_log_dt[l], ssm_b_re[l],
                             ssm_b_im[l], ssm_c_re[l], ssm_c_im[l], ssm_d[l])
        ys = _s5_ssm(u, B, mats)
        xt = _post(x1, attn, ys, ga, gs, bf(w_attn_up[l]), bf(w_ssm_glu[l]), bf(w_out[l]),
                   ffn2_norm[l][None], bf(ffn2_w_gate[l]), bf(ffn2_w_up[l]), bf(ffn2_w_down[l]),
                   final_norm[None])
    return xt.reshape(B, S, D)
```

```python
import functools
import math

import numpy as np
import jax
import jax.numpy as jnp
from jax import lax
from jax.experimental import pallas as pl
from jax.experimental.pallas import tpu as pltpu

F32 = jnp.float32
BF16 = jnp.bfloat16
HIGHEST = lax.Precision.HIGHEST

D_MODEL = 1024
N_HEADS = 8
HEAD_DIM = 64
ATTN_WIDTH = N_HEADS * HEAD_DIM
MOBA_BLOCK = 256
MOBA_TOPK = 3
SSM_GROUP = 16
SSM_GROUPS = 32
SSM_WIDTH = SSM_GROUP * SSM_GROUPS
SSM_STATE = 64
D_FF = 2816
EPS = 1e-6
LOG2_E = math.log2(math.e)

VMEM_LIMIT_BYTES = 56 * 1024 * 1024
TOKEN_TILE = MOBA_BLOCK
ROW_SPLITS = 2
HEADS_PER_STEP = 8
SOFTMAX_ROWS = 32
MXU_DEPTH = 256
BF16_ROWS = 16
V_ROWS = HEAD_DIM + BF16_ROWS
MASKED = -1e30
LANES = 128
SSM_CHUNK = 8
SSM_LANE_GROUPS = LANES // SSM_GROUP
SSM_ROW_TILE = 2048


def _const_spec(shape):
    nd = len(shape)
    return pl.BlockSpec(shape, lambda *_: (0,) * nd, pipeline_mode=pl.Buffered(1))


def _rms(x, g):
    return x * lax.rsqrt(jnp.mean(x * x, axis=-1, keepdims=True) + EPS) * g


def _row_groups(n_rows):
    rows = n_rows // ROW_SPLITS
    return [slice(g * rows, (g + 1) * rows) for g in range(ROW_SPLITS)]


def _swiglu_half_step(xs, g_norm, wg_ref, wu_ref, wd_ref):
    gate, up = [], []
    for x in xs:
        h = _rms(x, g_norm).astype(BF16)
        gate.append(jnp.dot(h, wg_ref[...], preferred_element_type=F32))
        up.append(jnp.dot(h, wu_ref[...], preferred_element_type=F32))
    out = []
    for x, g, u in zip(xs, gate, up):
        act = (g * jax.nn.sigmoid(g) * u).astype(BF16)
        out.append(x + 0.5 * jnp.dot(act, wd_ref[...], preferred_element_type=F32))
    return out


def _ffn1_proj_kernel(x_ref, n1_ref, wg_ref, wu_ref, wd_ref, n2_ref, win_ref,
                      x1_ref, q_ref, k_ref, v_ref, u_ref, ga_ref, gs_ref, kmean_ref):
    groups = _row_groups(x_ref.shape[0])
    x1 = _swiglu_half_step([x_ref[r, :] for r in groups], n1_ref[...], wg_ref, wu_ref, wd_ref)
    hs = []
    for r, x1_g in zip(groups, x1):
        x1_ref[r, :] = x1_g
        hs.append(_rms(x1_g, n2_ref[...]).astype(BF16))
    k_sum = None
    col = 0
    for ref in (q_ref, k_ref, v_ref, u_ref, ga_ref, gs_ref):
        width = ref.shape[-1]
        for r, h in zip(groups, hs):
            y = jnp.dot(h, win_ref[:, col:col + width], preferred_element_type=F32)
            ref[r, :] = y.astype(ref.dtype)
            if ref is k_ref:
                part = jnp.sum(y, axis=0, keepdims=True)
                k_sum = part if k_sum is None else k_sum + part
        col += width
    kmean_ref[...] = k_sum / x_ref.shape[0]


def _ffn1_proj(x, n1, wg, wu, wd, n2, win):
    tokens = x.shape[0]
    tm = TOKEN_TILE
    assert tm == MOBA_BLOCK
    row = lambda w: pl.BlockSpec((tm, w), lambda i: (i, 0))
    widths = (ATTN_WIDTH, ATTN_WIDTH, ATTN_WIDTH, SSM_WIDTH, D_MODEL, D_MODEL)
    dtypes = (F32, BF16, BF16, F32, F32, F32)
    return pl.pallas_call(
        _ffn1_proj_kernel,
        grid=(tokens // tm,),
        in_specs=[row(D_MODEL), _const_spec(n1.shape), _const_spec(wg.shape),
                  _const_spec(wu.shape), _const_spec(wd.shape), _const_spec(n2.shape),
                  _const_spec(win.shape)],
        out_specs=[row(D_MODEL)] + [row(w) for w in widths]
        + [pl.BlockSpec((None, 1, ATTN_WIDTH), lambda i: (i, 0, 0))],
        out_shape=[jax.ShapeDtypeStruct((tokens, D_MODEL), F32)]
        + [jax.ShapeDtypeStruct((tokens, w), dt) for w, dt in zip(widths, dtypes)]
        + [jax.ShapeDtypeStruct((tokens // tm, 1, ATTN_WIDTH), F32)],
        compiler_params=pltpu.CompilerParams(
            dimension_semantics=("arbitrary",), vmem_limit_bytes=VMEM_LIMIT_BYTES),
        name="ffn1_proj",
    )(x, n1, wg, wu, wd, n2, win)


def _split3(x):
    hi = x.astype(BF16).astype(F32)
    mid = (x - hi).astype(BF16).astype(F32)
    return hi, mid, x - hi - mid


def _moba_kernel(slopes_ref, q_ref, k_ref, v_ref, kmean_ref, o_ref, kms_ref, kaug_ref, qa_ref,
                 vt_ref, t_a, t_b, p_a, p_b, acc_ref):
    hg = pl.program_id(1)
    i = pl.program_id(2)
    blk = MOBA_BLOCK
    n_blocks = k_ref.shape[0]
    n_heads = HEADS_PER_STEP
    pair_w = MXU_DEPTH // 2
    pos_col = n_blocks
    neg_inf = -jnp.inf
    slopes = [slopes_ref[hg * n_heads + hh] * LOG2_E for hh in range(n_heads)]
    v_rows = lambda hh: slice(hh * V_ROWS, (hh + 1) * V_ROWS)
    aug_id = lax.broadcasted_iota(jnp.int32, (BF16_ROWS, blk), 0)
    pair_row_head = lax.broadcasted_iota(jnp.int32, (pair_w, blk), 0) // HEAD_DIM
    aug_col = lax.broadcasted_iota(jnp.int32, kaug_ref.shape, 1)

    @pl.when(i == 0)
    def _():
        kmean = kmean_ref[...]
        lane_head = lax.broadcasted_iota(jnp.int32, kmean.shape, 1) // HEAD_DIM
        km = jnp.concatenate([jnp.where(lane_head == hh, kmean, 0.0) for hh in range(n_heads)],
                             axis=0)
        km_hi = km.astype(BF16)
        kms_ref[...] = jnp.concatenate([km_hi, (km - km_hi.astype(F32)).astype(BF16)], axis=0)
        key = lax.broadcasted_iota(jnp.int32, kaug_ref.shape, 0).astype(F32)
        kaug_ref[...] = jnp.where((aug_col >= pos_col) & (aug_col < pos_col + 3), key,
                                  0.0).astype(BF16)
        ones_row = jnp.where(aug_id == 0, 1.0, 0.0).astype(BF16)

        def transpose_values(j, _):
            vt = v_ref[j].astype(F32).T.astype(BF16)
            for hh in range(n_heads):
                vt_ref[j, hh * V_ROWS:hh * V_ROWS + HEAD_DIM, :] = vt[hh * HEAD_DIM:
                                                                      (hh + 1) * HEAD_DIM]
                vt_ref[j, hh * V_ROWS + HEAD_DIM:(hh + 1) * V_ROWS, :] = ones_row
            return _
        lax.fori_loop(0, n_blocks, transpose_values, 0)

    qt = q_ref[...].T

    for hh in range(n_heads):
        pair = hh * HEAD_DIM // pair_w
        in_pair = slice(pair * pair_w, (pair + 1) * pair_w)
        qa_ref[hh, :pair_w, :] = (jnp.where(pair_row_head == hh - pair * (pair_w // HEAD_DIM),
                                            qt[in_pair], 0.0)
                                  * (HEAD_DIM ** -0.5 * LOG2_E)).astype(BF16)
        qa_ref[hh, pair_w:pair_w + n_blocks, :] = jnp.zeros((n_blocks, blk), BF16)
        pieces = _split3(jnp.full((1, blk), slopes[hh], F32)) * 2
        tile = jnp.zeros((BF16_ROWS, blk), F32)
        for n, piece in enumerate(pieces):
            tile = jnp.where(aug_id == n, piece, tile)
        qa_ref[hh, pair_w + pos_col:pair_w + pos_col + BF16_ROWS, :] = tile.astype(BF16)
        qa_ref[hh, pair_w + pos_col + BF16_ROWS:, :] = jnp.zeros(
            (MXU_DEPTH - pair_w - pos_col - BF16_ROWS, blk), BF16)

    def block_gates():
        qt_hi = qt.astype(BF16)
        qt_lo = (qt - qt_hi.astype(F32)).astype(BF16)
        gate_hi = jnp.dot(kms_ref[...], qt_hi, preferred_element_type=F32)
        return (gate_hi[:n_heads * n_blocks] + gate_hi[n_heads * n_blocks:]
                + jnp.dot(kms_ref[:n_heads * n_blocks, :], qt_lo, preferred_element_type=F32))

    def select_blocks(gates):
        shape = (n_heads, n_blocks, blk)
        blk_id = lax.broadcasted_iota(jnp.int32, shape, 1).astype(F32)
        g = jnp.where(blk_id < i.astype(F32), gates.reshape(shape), neg_inf)
        sel = jnp.zeros(shape, jnp.bool_)
        for _ in range(MOBA_TOPK):
            top = jnp.max(g, axis=1, keepdims=True)
            first = jnp.min(jnp.where(g == top, blk_id, float(n_blocks)), axis=1, keepdims=True)
            pick = (blk_id == first) & (top > neg_inf)
            sel = sel | pick
            g = jnp.where(pick, neg_inf, g)
        mask_rows = jnp.where(sel, 0.0, MASKED).astype(BF16)
        for hh in range(n_heads):
            qa_ref[hh, pair_w:pair_w + n_blocks, :] = mask_rows[hh]

    last_block = n_blocks - 1
    chunks = [slice(c, c + SOFTMAX_ROWS) for c in range(0, blk, SOFTMAX_ROWS)]

    def qk_into(t_ref, j, own):
        kj = k_ref[j]
        extras = kaug_ref[...]
        if not own:
            offset = ((j - i) * blk).astype(F32)
            extras = jnp.where(aug_col == j, 1.0,
                               jnp.where((aug_col >= pos_col + 3) & (aug_col < pos_col + 6),
                                         offset, extras.astype(F32))).astype(BF16)
        col_max = []
        for hh in range(n_heads):
            pair = hh * HEAD_DIM // pair_w
            keys = jnp.concatenate([kj[:, pair * pair_w:(pair + 1) * pair_w], extras], axis=1)
            t = jnp.dot(keys, qa_ref[hh], preferred_element_type=F32)
            t_ref[hh] = t
            col_max.append(None if own else jnp.max(t, axis=0, keepdims=True))
        return tuple(col_max)

    def pv(p_ref, j, alpha):
        vj = vt_ref[j]
        for hh in range(n_heads):
            acc_ref[hh] = alpha[hh] * acc_ref[hh] + jnp.dot(vj[v_rows(hh)], p_ref[hh],
                                                            preferred_element_type=F32)

    def softmax_into(t_ref, p_ref, hh, m, cmax):
        causal = cmax is None

        def logits(c):
            t = t_ref[hh, c, :]
            if not causal:
                return t
            key = lax.broadcasted_iota(jnp.int32, t.shape, 0) + c.start
            query = lax.broadcasted_iota(jnp.int32, t.shape, 1)
            return jnp.where(key <= query, t, neg_inf)

        if causal:
            cmax = logits(chunks[0])
            for c in chunks[1:]:
                cmax = jnp.maximum(cmax, logits(c))
            cmax = jnp.max(cmax, axis=0, keepdims=True)
        m_new = cmax if m is None else jnp.maximum(m, cmax)
        for c in chunks:
            p_ref[hh, c, :] = jnp.exp2(logits(c) - m_new).astype(BF16)
        alpha = jnp.ones_like(m_new) if m is None else jnp.exp2(m - m_new)
        return alpha, m_new

    gates = block_gates()
    qk_into(t_b, i, True)
    select_blocks(gates)
    cmax_first = qk_into(t_a, 0, False)
    own = [softmax_into(t_b, p_b, hh, None, None) for hh in range(n_heads)]
    alpha0, m0 = (tuple(x) for x in zip(*own))
    acc_ref[...] = jnp.zeros_like(acc_ref)

    def body(n, carry):
        prev_j, a_prev, m, cmax_a = carry
        j0 = 2 * n
        j1 = jnp.minimum(j0 + 1, last_block)
        cmax_b = qk_into(t_b, j1, False)
        pv(p_b, prev_j, a_prev)
        st0 = [softmax_into(t_a, p_a, hh, m[hh], cmax_a[hh]) for hh in range(n_heads)]
        a0, m = (tuple(x) for x in zip(*st0))
        cmax_a = qk_into(t_a, jnp.minimum(j0 + 2, last_block), False)
        pv(p_a, j0, a0)
        st1 = [softmax_into(t_b, p_b, hh, m[hh], cmax_b[hh]) for hh in range(n_heads)]
        a1, m = (tuple(x) for x in zip(*st1))
        return j1, a1, m, cmax_a

    def double_body(n, carry):
        return body(2 * n + 1, body(2 * n, carry))

    n_pairs = (i + 1) // 2
    carry = lax.fori_loop(0, n_pairs // 2, double_body, (i, alpha0, m0, cmax_first))
    prev_j, a_prev, _, _ = lax.fori_loop(n_pairs // 2 * 2, n_pairs, body, carry)
    pv(p_b, prev_j, a_prev)
    out_t = jnp.concatenate([acc_ref[hh, :HEAD_DIM, :] / acc_ref[hh, HEAD_DIM:HEAD_DIM + 1, :]
                             for hh in range(n_heads)], axis=0)
    o_ref[...] = out_t.T.astype(o_ref.dtype)


def _moba_attention(q, k, v, kmean, slopes, n_batch):
    tokens, width = q.shape
    blk = MOBA_BLOCK
    n_blocks = tokens // n_batch // blk
    step_width = HEADS_PER_STEP * HEAD_DIM
    q_spec = pl.BlockSpec((blk, step_width), lambda b, g, i: (b * n_blocks + i, g))
    kv_spec = pl.BlockSpec((n_blocks, blk, step_width), lambda b, g, i: (b, 0, g),
                           pipeline_mode=pl.Buffered(1))
    by_block = lambda a: a.reshape(tokens // blk, blk, width)
    return pl.pallas_call(
        _moba_kernel,
        grid=(n_batch, width // step_width, n_blocks),
        in_specs=[pl.BlockSpec(memory_space=pltpu.SMEM), q_spec, kv_spec, kv_spec,
                  pl.BlockSpec((n_blocks, step_width), lambda b, g, i: (b, g))],
        out_specs=q_spec,
        out_shape=jax.ShapeDtypeStruct((tokens, width), BF16),
        scratch_shapes=[pltpu.VMEM((2 * HEADS_PER_STEP * n_blocks, step_width), BF16),
                        pltpu.VMEM((blk, MXU_DEPTH // 2), BF16),
                        pltpu.VMEM((HEADS_PER_STEP, MXU_DEPTH, blk), BF16),
                        pltpu.VMEM((n_blocks, HEADS_PER_STEP * V_ROWS, blk), BF16),
                        pltpu.VMEM((HEADS_PER_STEP, blk, blk), F32),
                        pltpu.VMEM((HEADS_PER_STEP, blk, blk), F32),
                        pltpu.VMEM((HEADS_PER_STEP, blk, blk), BF16),
                        pltpu.VMEM((HEADS_PER_STEP, blk, blk), BF16),
                        pltpu.VMEM((HEADS_PER_STEP, V_ROWS, blk), F32)],
        compiler_params=pltpu.CompilerParams(
            dimension_semantics=("arbitrary", "arbitrary", "arbitrary"),
            vmem_limit_bytes=VMEM_LIMIT_BYTES),
        name="moba_attn",
    )(slopes, q, by_block(k), by_block(v), kmean)


def _ssm_matrices(lam_re, lam_im, log_dt, b_re, b_im, c_re, c_im, d_skip):
    T, G, P, Hc = SSM_CHUNK, SSM_GROUPS, SSM_STATE, SSM_GROUP
    dt = jnp.exp(log_dt)[:, None]
    mag = jnp.exp(lam_re * dt)
    ar = mag * jnp.cos(lam_im * dt)
    ai = mag * jnp.sin(lam_im * dt)
    nr, ni = ar - 1.0, ai
    den = lam_re * lam_re + lam_im * lam_im
    fr = (nr * lam_re + ni * lam_im) / den
    fi = (ni * lam_re - nr * lam_im) / den
    bbr = fr[..., None] * b_re - fi[..., None] * b_im
    bbi = fr[..., None] * b_im + fi[..., None] * b_re
    pr, pi = [jnp.ones_like(ar)], [jnp.zeros_like(ar)]
    for _ in range(T):
        pr, pi = pr + [pr[-1] * ar - pi[-1] * ai], pi + [pr[-1] * ai + pi[-1] * ar]
    pw_r, pw_i = jnp.stack(pr), jnp.stack(pi)

    ca_r = c_re[None] * pw_r[:, :, None, :] - c_im[None] * pw_i[:, :, None, :]
    ca_i = c_re[None] * pw_i[:, :, None, :] + c_im[None] * pw_r[:, :, None, :]
    kern = jnp.sum(ca_r[:T, :, :, :, None] * bbr[None, :, None, :, :]
                   - ca_i[:T, :, :, :, None] * bbi[None, :, None, :, :], axis=3)
    skip = d_skip.reshape(G, Hc)[:, :, None] * jnp.eye(Hc, dtype=F32)[None]
    kern = kern.at[0].add(skip)

    GB = SSM_LANE_GROUPS
    CB = G // GB
    k_c = kern.transpose(1, 0, 3, 2).reshape(CB, GB, T, Hc, Hc).transpose(0, 2, 1, 3, 4)
    k_c = jnp.tile(k_c.reshape(CB, T, GB * Hc, Hc), (1, 1, 1, GB))
    rev_r, rev_i = pw_r[T - 1::-1], pw_i[T - 1::-1]
    n_r = rev_r[..., None] * bbr[None] - rev_i[..., None] * bbi[None]
    n_i = rev_r[..., None] * bbi[None] + rev_i[..., None] * bbr[None]
    n_c = jnp.concatenate([n_r, n_i], axis=2).transpose(0, 1, 3, 2)
    n_c = n_c.reshape(T, CB, GB, Hc, 2 * P).transpose(1, 0, 2, 3, 4).reshape(CB, T * LANES, 2 * P)
    m_c = jnp.stack([ca_r[1:], -ca_i[1:]])
    m_c = m_c.reshape(2, T, CB, GB, Hc, P).transpose(2, 0, 3, 5, 1, 4).reshape(
        CB, 2 * GB * P, T * Hc)
    a_chunk = jnp.stack([pw_r[T].reshape(1, G * P), pw_i[T].reshape(1, G * P)])
    return k_c.astype(BF16), n_c.astype(BF16), m_c.astype(BF16), a_chunk


def _div(x, d):
    return x >> (d.bit_length() - 1)


def _mod(x, d):
    return x & (d - 1)


def _spread_block_diag(compact, src_of_col, row_group, col_group):
    n_rows, n_src = compact.shape
    n_cols = n_rows
    src = lax.broadcasted_iota(jnp.int32, (n_src, n_cols), 0)
    col = lax.broadcasted_iota(jnp.int32, (n_src, n_cols), 1)
    spread = jnp.where(src == src_of_col(col), 1.0, 0.0).astype(BF16)
    full = jnp.dot(compact.astype(BF16), spread, preferred_element_type=F32)
    r = lax.broadcasted_iota(jnp.int32, full.shape, 0)
    c = lax.broadcasted_iota(jnp.int32, full.shape, 1)
    return jnp.where(row_group(r) == col_group(c), full, 0.0).astype(BF16)


def _chunk_rows(u_ref):
    n = u_ref.shape[0] // SSM_CHUNK
    return jnp.concatenate([u_ref[pl.ds(s, n, stride=SSM_CHUNK), :] for s in range(SSM_CHUNK)],
                           axis=1).astype(BF16)


def _ssm_kernel(*refs, tiles_per_seq):
    T, GB, P, Hc = SSM_CHUNK, SSM_LANE_GROUPS, SSM_STATE, SSM_GROUP
    nb = SSM_WIDTH // LANES
    u_refs, (k_c_ref, n_c_ref, m_c_ref, a_ref) = refs[:nb], refs[nb:nb + 4]
    y_refs = refs[nb + 4:2 * nb + 4]
    n_s, w_s, m_s, er_s, ei_s, hr_s, hi_s, cr_s, ci_s = refs[2 * nb + 4:]
    half = GB * P
    step = pl.program_id(0)

    @pl.when(step == 0)
    def _():
        r = lax.broadcasted_iota(jnp.int32, (LANES, LANES), 0)
        c = lax.broadcasted_iota(jnp.int32, (LANES, LANES), 1)
        same_group = _div(r, Hc) == _div(c, Hc)
        for cb in range(nb):
            n_s[cb] = _spread_block_diag(
                n_c_ref[cb], lambda c: _div(c, GB * P) * P + _mod(c, P),
                lambda r: _mod(_div(r, Hc), GB), lambda c: _mod(_div(c, P), GB))
            w_s[cb] = jnp.zeros(w_s.shape[1:], BF16)
            for tau in range(T):
                blk = jnp.where(same_group, k_c_ref[cb, tau], 0.0).astype(BF16)
                for s in range(T - tau):
                    w_s[cb, s * LANES:(s + 1) * LANES,
                        (s + tau) * LANES:(s + tau + 1) * LANES] = blk
            m_s[cb] = _spread_block_diag(
                m_c_ref[cb], lambda c: _div(c, LANES) * Hc + _mod(c, Hc),
                lambda r: _mod(_div(r, P), GB), lambda c: _mod(_div(c, Hc), GB))

    @pl.when(step % tiles_per_seq == 0)
    def _():
        cr_s[...] = jnp.zeros_like(cr_s)
        ci_s[...] = jnp.zeros_like(ci_s)

    u_rows = [_chunk_rows(u_ref) for u_ref in u_refs]
    for cb in range(nb):
        e = jnp.dot(u_rows[cb], n_s[cb], preferred_element_type=F32)
        er_s[:, cb * half:(cb + 1) * half] = e[:, :half]
        ei_s[:, cb * half:(cb + 1) * half] = e[:, half:]

    n = er_s.shape[0]
    for cb in range(nb):
        for lo in range(0, T * LANES, MXU_DEPTH):
            hi = lo + MXU_DEPTH
            y = jnp.dot(u_rows[cb][:, :hi], w_s[cb, :hi, lo:hi], preferred_element_type=F32)
            for t in range(lo // LANES, hi // LANES):
                y_refs[cb][pl.ds(t, n, stride=T), :] = y[:, t * LANES - lo:(t + 1) * LANES - lo]

    ar = a_ref[0]
    ai = a_ref[1]
    hr, hi = cr_s[...], ci_s[...]
    for c in range(n):
        hr_s[c:c + 1, :] = hr
        hi_s[c:c + 1, :] = hi
        hr, hi = (ar * hr - ai * hi + er_s[c:c + 1, :], ar * hi + ai * hr + ei_s[c:c + 1, :])
    cr_s[...] = hr
    ci_s[...] = hi

    for cb in range(nb):
        cols = slice(cb * half, (cb + 1) * half)
        h = jnp.concatenate([hr_s[:, cols], hi_s[:, cols]], axis=1).astype(BF16)
        y = jnp.dot(h, m_s[cb], preferred_element_type=F32)
        for t in range(T):
            rows_t = pl.ds(t, n, stride=T)
            y_refs[cb][rows_t, :] = y_refs[cb][rows_t, :] + y[:, t * LANES:(t + 1) * LANES]


def _s5_ssm(u, n_batch, ops):
    k_c, n_c, m_c, a_chunk = ops
    tokens = u.shape[0]
    T = SSM_CHUNK
    nb = SSM_WIDTH // LANES
    rt = SSM_ROW_TILE
    ct = rt // T
    op_dim = T * LANES
    n_state = SSM_GROUPS * SSM_STATE
    assert (tokens // n_batch) % rt == 0
    u_specs = [pl.BlockSpec((rt, LANES), functools.partial(lambda cb, r: (r, cb), cb))
               for cb in range(nb)]
    y_spec = pl.BlockSpec((rt, LANES), lambda r: (r, 0))
    return pl.pallas_call(
        functools.partial(_ssm_kernel, tiles_per_seq=tokens // n_batch // rt),
        grid=(tokens // rt,),
        in_specs=u_specs + [_const_spec(k_c.shape), _const_spec(n_c.shape),
                            _const_spec(m_c.shape), _const_spec(a_chunk.shape)],
        out_specs=[y_spec] * nb,
        out_shape=[jax.ShapeDtypeStruct((tokens, LANES), F32)] * nb,
        scratch_shapes=[pltpu.VMEM((nb, op_dim, op_dim), BF16)] * 3
        + [pltpu.VMEM((ct, n_state), F32)] * 4 + [pltpu.VMEM((1, n_state), F32)] * 2,
        compiler_params=pltpu.CompilerParams(dimension_semantics=("arbitrary",),
                                             vmem_limit_bytes=VMEM_LIMIT_BYTES),
        name="ssm",
    )(*([u] * nb), k_c, n_c, m_c, a_chunk)


def _post_kernel(x1_ref, attn_ref, *refs):
    n_y = SSM_WIDTH // LANES
    y_refs = refs[:n_y]
    (ga_ref, gs_ref, wup_ref, wglu_ref, wout_ref, n3_ref, wg_ref, wu_ref, wd_ref, nf_ref,
     o_ref) = refs[n_y:]
    groups = _row_groups(x1_ref.shape[0])
    dot = functools.partial(jnp.dot, preferred_element_type=F32)
    attn = [dot(attn_ref[r, :], wup_ref[...]) for r in groups]
    glu = []
    for r in groups:
        y = jnp.concatenate([y_ref[r, :] for y_ref in y_refs], axis=1)
        glu.append(dot(jax.nn.gelu(y, approximate=True).astype(BF16), wglu_ref[...]))
    x2 = []
    for g, r in enumerate(groups):
        ssm_out = glu[g][:, :D_MODEL] * jax.nn.sigmoid(glu[g][:, D_MODEL:])
        merged = (jax.nn.sigmoid(ga_ref[r, :]) * attn[g]
                  + jax.nn.sigmoid(gs_ref[r, :]) * ssm_out)
        x2.append(x1_ref[r, :] + dot(merged.astype(BF16), wout_ref[...]))
    x3 = _swiglu_half_step(x2, n3_ref[...], wg_ref, wu_ref, wd_ref)
    for r, x3_g in zip(groups, x3):
        o_ref[r, :] = _rms(x3_g, nf_ref[...])


def _post(x1, attn, ys, ga, gs, wup, wglu, wout, n3, wg, wu, wd, nf):
    tokens = x1.shape[0]
    tm = TOKEN_TILE
    row = lambda w: pl.BlockSpec((tm, w), lambda i: (i, 0))
    consts = (wup, wglu, wout, n3, wg, wu, wd, nf)
    return pl.pallas_call(
        _post_kernel,
        grid=(tokens // tm,),
        in_specs=[row(D_MODEL), row(ATTN_WIDTH)] + [row(LANES)] * len(ys)
        + [row(D_MODEL), row(D_MODEL)] + [_const_spec(c.shape) for c in consts],
        out_specs=row(D_MODEL),
        out_shape=jax.ShapeDtypeStruct((tokens, D_MODEL), F32),
        compiler_params=pltpu.CompilerParams(
            dimension_semantics=("arbitrary",), vmem_limit_bytes=VMEM_LIMIT_BYTES),
        name="post",
    )(x1, attn, *ys, ga, gs, *consts)


def kernel(x, ffn1_norm, ffn1_w_gate, ffn1_w_up, ffn1_w_down, mix_norm, w_in, w_attn_up, ssm_lambda_re, ssm_lambda_im, ssm_log_dt, ssm_b_re, ssm_b_im, ssm_c_re, ssm_c_im, ssm_d, w_ssm_glu, w_out, ffn2_norm, ffn2_w_gate, ffn2_w_up, ffn2_w_down, final_norm):
    B, S, D = x.shape
    depth = ffn1_norm.shape[0]
    assert depth == 1, "final norm is fused into the single layer's last stage"
    tokens = B * S
    slopes = jnp.asarray(2.0 ** (-8.0 * np.arange(1, N_HEADS + 1) / N_HEADS), dtype=F32)
    bf = lambda w: w.astype(BF16)
    xt = x.reshape(tokens, D)
    for l in range(depth):
        x1, q, k, v, u, ga, gs, kmean = _ffn1_proj(
            xt, ffn1_norm[l][None], bf(ffn1_w_gate[l]), bf(ffn1_w_up[l]), bf(ffn1_w_down[l]),
            mix_norm[l][None], bf(w_in[l]))
        attn = _moba_attention(q, k, v, kmean.reshape(-1, ATTN_WIDTH), slopes, B)
        mats = _ssm_matrices(ssm_lambda_re[l], ssm_lambda_im[l], ssm_log_dt[l], ssm_b_re[l],
                             ssm_b_im[l], ssm_c_re[l], ssm_c_im[l], ssm_d[l])
        ys = _s5_ssm(u, B, mats)
        xt = _post(x1, attn, ys, ga, gs, bf(w_attn_up[l]), bf(w_ssm_glu[l]), bf(w_out[l]),
                   ffn2_norm[l][None], bf(ffn2_w_gate[l]), bf(ffn2_w_up[l]), bf(ffn2_w_down[l]),
                   final_norm[None])
    return xt.reshape(B, S, D)
```

```python
import functools
import math

import numpy as np
import jax
import jax.numpy as jnp
from jax import lax
from jax.experimental import pallas as pl
from jax.experimental.pallas import tpu as pltpu

F32 = jnp.float32
BF16 = jnp.bfloat16

D_MODEL = 1024
N_HEADS = 8
HEAD_DIM = 64
ATTN_WIDTH = N_HEADS * HEAD_DIM
MOBA_BLOCK = 256
MOBA_TOPK = 3
SSM_GROUP = 16
SSM_GROUPS = 32
SSM_WIDTH = SSM_GROUP * SSM_GROUPS
SSM_STATE = 64
EPS = 1e-6
LOG2_E = math.log2(math.e)

VMEM_LIMIT_BYTES = 56 * 1024 * 1024
TOKEN_TILE = MOBA_BLOCK
ROW_SPLITS = 2
HEADS_PER_STEP = 8
SOFTMAX_ROWS = 32
MXU_DEPTH = 256
BF16_ROWS = 16
V_ROWS = HEAD_DIM + BF16_ROWS
MASKED = -1e30
LANES = 128
SSM_CHUNK = 8
SSM_LANE_GROUPS = LANES // SSM_GROUP
SSM_ROW_TILE = 2048


def _const_spec(shape):
    nd = len(shape)
    return pl.BlockSpec(shape, lambda *_: (0,) * nd, pipeline_mode=pl.Buffered(1))


def _rms(x, g):
    return x * lax.rsqrt(jnp.mean(x * x, axis=-1, keepdims=True) + EPS) * g


def _row_groups(n_rows):
    rows = n_rows // ROW_SPLITS
    return [slice(g * rows, (g + 1) * rows) for g in range(ROW_SPLITS)]


def _swiglu_half_step(xs, g_norm, wg_ref, wu_ref, wd_ref):
    gate, up = [], []
    for x in xs:
        h = _rms(x, g_norm).astype(BF16)
        gate.append(jnp.dot(h, wg_ref[...], preferred_element_type=F32))
        up.append(jnp.dot(h, wu_ref[...], preferred_element_type=F32))
    out = []
    for x, g, u in zip(xs, gate, up):
        act = (g * jax.nn.sigmoid(g) * u).astype(BF16)
        out.append(x + 0.5 * jnp.dot(act, wd_ref[...], preferred_element_type=F32))
    return out


def _ffn1_proj_kernel(x_ref, n1_ref, wg_ref, wu_ref, wd_ref, n2_ref, win_ref,
                      x1_ref, q_ref, k_ref, vt_ref, u_ref, ga_ref, gs_ref, kmean_ref, v_s):
    groups = _row_groups(x_ref.shape[0])
    x1 = _swiglu_half_step([x_ref[r, :] for r in groups], n1_ref[...], wg_ref, wu_ref, wd_ref)
    hs = []
    for r, x1_g in zip(groups, x1):
        x1_ref[r, :] = x1_g
        hs.append(_rms(x1_g, n2_ref[...]).astype(BF16))
    k_sum = None
    col = 0
    for ref in (q_ref, k_ref, v_s, u_ref, ga_ref, gs_ref):
        width = ref.shape[-1]
        for r, h in zip(groups, hs):
            y = jnp.dot(h, win_ref[:, col:col + width], preferred_element_type=F32)
            ref[r, :] = y.astype(ref.dtype)
            if ref is k_ref:
                part = jnp.sum(y, axis=0, keepdims=True)
                k_sum = part if k_sum is None else k_sum + part
        col += width
    kmean_ref[...] = k_sum / x_ref.shape[0]
    vt = v_s[...].T.astype(BF16)
    ones_row = jnp.where(lax.broadcasted_iota(jnp.int32, (BF16_ROWS, vt.shape[1]), 0) == 0,
                         1.0, 0.0).astype(BF16)
    for hh in range(N_HEADS):
        vt_ref[hh * V_ROWS:hh * V_ROWS + HEAD_DIM, :] = vt[hh * HEAD_DIM:(hh + 1) * HEAD_DIM]
        vt_ref[hh * V_ROWS + HEAD_DIM:(hh + 1) * V_ROWS, :] = ones_row


def _ffn1_proj(x, n1, wg, wu, wd, n2, win):
    tokens = x.shape[0]
    tm = TOKEN_TILE
    assert tm == MOBA_BLOCK
    row = lambda w: pl.BlockSpec((tm, w), lambda i: (i, 0))
    per_block = lambda r, c: pl.BlockSpec((None, r, c), lambda i: (i, 0, 0))
    rows = lambda w, dt: jax.ShapeDtypeStruct((tokens, w), dt)
    blocks = lambda r, c, dt: jax.ShapeDtypeStruct((tokens // tm, r, c), dt)
    return pl.pallas_call(
        _ffn1_proj_kernel,
        grid=(tokens // tm,),
        in_specs=[row(D_MODEL), _const_spec(n1.shape), _const_spec(wg.shape),
                  _const_spec(wu.shape), _const_spec(wd.shape), _const_spec(n2.shape),
                  _const_spec(win.shape)],
        out_specs=[row(D_MODEL), row(ATTN_WIDTH), row(ATTN_WIDTH),
                   per_block(N_HEADS * V_ROWS, tm), row(SSM_WIDTH), row(D_MODEL), row(D_MODEL),
                   per_block(1, ATTN_WIDTH)],
        out_shape=[rows(D_MODEL, F32), rows(ATTN_WIDTH, F32), rows(ATTN_WIDTH, BF16),
                   blocks(N_HEADS * V_ROWS, tm, BF16), rows(SSM_WIDTH, F32), rows(D_MODEL, F32),
                   rows(D_MODEL, F32), blocks(1, ATTN_WIDTH, F32)],
        scratch_shapes=[pltpu.VMEM((tm, ATTN_WIDTH), F32)],
        compiler_params=pltpu.CompilerParams(
            dimension_semantics=("arbitrary",), vmem_limit_bytes=VMEM_LIMIT_BYTES),
        name="ffn1_proj",
    )(x, n1, wg, wu, wd, n2, win)


def _split3(x):
    hi = x.astype(BF16).astype(F32)
    mid = (x - hi).astype(BF16).astype(F32)
    return hi, mid, x - hi - mid


def _moba_kernel(slopes_ref, q_ref, k_ref, vt_ref, kmean_ref, o_ref, kms_ref, kaug_ref, qa_ref,
                 t_a, t_b, p_a, p_b, acc_ref):
    hg = pl.program_id(1)
    i = pl.program_id(2)
    blk = MOBA_BLOCK
    n_blocks = k_ref.shape[0]
    n_heads = HEADS_PER_STEP
    pair_w = MXU_DEPTH // 2
    pos_col = n_blocks
    neg_inf = -jnp.inf
    slopes = [slopes_ref[hg * n_heads + hh] * LOG2_E for hh in range(n_heads)]
    v_rows = lambda hh: slice(hh * V_ROWS, (hh + 1) * V_ROWS)
    aug_id = lax.broadcasted_iota(jnp.int32, (BF16_ROWS, blk), 0)
    pair_row_head = lax.broadcasted_iota(jnp.int32, (pair_w, blk), 0) // HEAD_DIM
    aug_col = lax.broadcasted_iota(jnp.int32, kaug_ref.shape, 1)

    @pl.when(i == 0)
    def _():
        kmean = kmean_ref[...]
        lane_head = lax.broadcasted_iota(jnp.int32, kmean.shape, 1) // HEAD_DIM
        km = jnp.concatenate([jnp.where(lane_head == hh, kmean, 0.0) for hh in range(n_heads)],
                             axis=0)
        km_hi = km.astype(BF16)
        kms_ref[...] = jnp.concatenate([km_hi, (km - km_hi.astype(F32)).astype(BF16)], axis=0)
        key = lax.broadcasted_iota(jnp.int32, kaug_ref.shape, 0).astype(F32)
        kaug_ref[...] = jnp.where((aug_col >= pos_col) & (aug_col < pos_col + 3), key,
                                  0.0).astype(BF16)

    qt = q_ref[...].T

    for hh in range(n_heads):
        pair = hh * HEAD_DIM // pair_w
        in_pair = slice(pair * pair_w, (pair + 1) * pair_w)
        qa_ref[hh, :pair_w, :] = (jnp.where(pair_row_head == hh - pair * (pair_w // HEAD_DIM),
                                            qt[in_pair], 0.0)
                                  * (HEAD_DIM ** -0.5 * LOG2_E)).astype(BF16)
        qa_ref[hh, pair_w:pair_w + n_blocks, :] = jnp.zeros((n_blocks, blk), BF16)
        pieces = _split3(jnp.full((1, blk), slopes[hh], F32)) * 2
        tile = jnp.zeros((BF16_ROWS, blk), F32)
        for n, piece in enumerate(pieces):
            tile = jnp.where(aug_id == n, piece, tile)
        qa_ref[hh, pair_w + pos_col:pair_w + pos_col + BF16_ROWS, :] = tile.astype(BF16)
        qa_ref[hh, pair_w + pos_col + BF16_ROWS:, :] = jnp.zeros(
            (MXU_DEPTH - pair_w - pos_col - BF16_ROWS, blk), BF16)

    def block_gates():
        qt_hi = qt.astype(BF16)
        qt_lo = (qt - qt_hi.astype(F32)).astype(BF16)
        gate_hi = jnp.dot(kms_ref[...], qt_hi, preferred_element_type=F32)
        return (gate_hi[:n_heads * n_blocks] + gate_hi[n_heads * n_blocks:]
                + jnp.dot(kms_ref[:n_heads * n_blocks, :], qt_lo, preferred_element_type=F32))

    def select_blocks(gates):
        shape = (n_heads, n_blocks, blk)
        blk_id = lax.broadcasted_iota(jnp.int32, shape, 1).astype(F32)
        g = jnp.where(blk_id < i.astype(F32), gates.reshape(shape), neg_inf)
        sel = jnp.zeros(shape, jnp.bool_)
        for _ in range(MOBA_TOPK):
            top = jnp.max(g, axis=1, keepdims=True)
            first = jnp.min(jnp.where(g == top, blk_id, float(n_blocks)), axis=1, keepdims=True)
            pick = (blk_id == first) & (top > neg_inf)
            sel = sel | pick
            g = jnp.where(pick, neg_inf, g)
        mask_rows = jnp.where(sel, 0.0, MASKED).astype(BF16)
        for hh in range(n_heads):
            qa_ref[hh, pair_w:pair_w + n_blocks, :] = mask_rows[hh]

    last_block = n_blocks - 1
    chunks = [slice(c, c + SOFTMAX_ROWS) for c in range(0, blk, SOFTMAX_ROWS)]

    def qk_into(t_ref, j, own):
        kj = k_ref[j]
        extras = kaug_ref[...]
        if not own:
            offset = ((j - i) * blk).astype(F32)
            extras = jnp.where(aug_col == j, 1.0,
                               jnp.where((aug_col >= pos_col + 3) & (aug_col < pos_col + 6),
                                         offset, extras.astype(F32))).astype(BF16)
        col_max = []
        for hh in range(n_heads):
            pair = hh * HEAD_DIM // pair_w
            keys = jnp.concatenate([kj[:, pair * pair_w:(pair + 1) * pair_w], extras], axis=1)
            t = jnp.dot(keys, qa_ref[hh], preferred_element_type=F32)
            t_ref[hh] = t
            col_max.append(None if own else jnp.max(t, axis=0, keepdims=True))
        return tuple(col_max)

    def pv(p_ref, j, alpha):
        vj = vt_ref[j]
        for hh in range(n_heads):
            acc_ref[hh] = alpha[hh] * acc_ref[hh] + jnp.dot(vj[v_rows(hh)], p_ref[hh],
                                                            preferred_element_type=F32)

    def softmax_into(t_ref, p_ref, hh, m, cmax):
        causal = cmax is None

        def logits(c):
            t = t_ref[hh, c, :]
            if not causal:
                return t
            key = lax.broadcasted_iota(jnp.int32, t.shape, 0) + c.start
            query = lax.broadcasted_iota(jnp.int32, t.shape, 1)
            return jnp.where(key <= query, t, neg_inf)

        if causal:
            cmax = logits(chunks[0])
            for c in chunks[1:]:
                cmax = jnp.maximum(cmax, logits(c))
            cmax = jnp.max(cmax, axis=0, keepdims=True)
        m_new = cmax if m is None else jnp.maximum(m, cmax)
        for c in chunks:
            p_ref[hh, c, :] = jnp.exp2(logits(c) - m_new).astype(BF16)
        alpha = jnp.ones_like(m_new) if m is None else jnp.exp2(m - m_new)
        return alpha, m_new

    gates = block_gates()
    qk_into(t_b, i, True)
    select_blocks(gates)
    cmax_first = qk_into(t_a, 0, False)
    own = [softmax_into(t_b, p_b, hh, None, None) for hh in range(n_heads)]
    alpha0, m0 = (tuple(x) for x in zip(*own))
    acc_ref[...] = jnp.zeros_like(acc_ref)

    def body(n, carry):
        prev_j, a_prev, m, cmax_a = carry
        j0 = 2 * n
        j1 = jnp.minimum(j0 + 1, last_block)
        cmax_b = qk_into(t_b, j1, False)
        pv(p_b, prev_j, a_prev)
        st0 = [softmax_into(t_a, p_a, hh, m[hh], cmax_a[hh]) for hh in range(n_heads)]
        a0, m = (tuple(x) for x in zip(*st0))
        cmax_a = qk_into(t_a, jnp.minimum(j0 + 2, last_block), False)
        pv(p_a, j0, a0)
        st1 = [softmax_into(t_b, p_b, hh, m[hh], cmax_b[hh]) for hh in range(n_heads)]
        a1, m = (tuple(x) for x in zip(*st1))
        return j1, a1, m, cmax_a

    def double_body(n, carry):
        return body(2 * n + 1, body(2 * n, carry))

    n_pairs = (i + 1) // 2
    carry = lax.fori_loop(0, n_pairs // 2, double_body, (i, alpha0, m0, cmax_first))
    prev_j, a_prev, _, _ = lax.fori_loop(n_pairs // 2 * 2, n_pairs, body, carry)
    pv(p_b, prev_j, a_prev)
    out_t = jnp.concatenate([acc_ref[hh, :HEAD_DIM, :] / acc_ref[hh, HEAD_DIM:HEAD_DIM + 1, :]
                             for hh in range(n_heads)], axis=0)
    o_ref[...] = out_t.T.astype(o_ref.dtype)


def _moba_attention(q, k, vt, kmean, slopes, n_batch):
    tokens, width = q.shape
    blk = MOBA_BLOCK
    n_blocks = tokens // n_batch // blk
    step_width = HEADS_PER_STEP * HEAD_DIM
    q_spec = pl.BlockSpec((blk, step_width), lambda b, g, i: (b * n_blocks + i, g))
    return pl.pallas_call(
        _moba_kernel,
        grid=(n_batch, width // step_width, n_blocks),
        in_specs=[pl.BlockSpec(memory_space=pltpu.SMEM), q_spec,
                  pl.BlockSpec((n_blocks, blk, step_width), lambda b, g, i: (b, 0, g)),
                  pl.BlockSpec((n_blocks, HEADS_PER_STEP * V_ROWS, blk),
                               lambda b, g, i: (b, g, 0)),
                  pl.BlockSpec((n_blocks, step_width), lambda b, g, i: (b, g))],
        out_specs=q_spec,
        out_shape=jax.ShapeDtypeStruct((tokens, width), BF16),
        scratch_shapes=[pltpu.VMEM((2 * HEADS_PER_STEP * n_blocks, step_width), BF16),
                        pltpu.VMEM((blk, MXU_DEPTH // 2), BF16),
                        pltpu.VMEM((HEADS_PER_STEP, MXU_DEPTH, blk), BF16),
                        pltpu.VMEM((HEADS_PER_STEP, blk, blk), F32),
                        pltpu.VMEM((HEADS_PER_STEP, blk, blk), F32),
                        pltpu.VMEM((HEADS_PER_STEP, blk, blk), BF16),
                        pltpu.VMEM((HEADS_PER_STEP, blk, blk), BF16),
                        pltpu.VMEM((HEADS_PER_STEP, V_ROWS, blk), F32)],
        compiler_params=pltpu.CompilerParams(
            dimension_semantics=("arbitrary", "arbitrary", "arbitrary"),
            vmem_limit_bytes=VMEM_LIMIT_BYTES),
        name="moba_attn",
    )(slopes, q, k.reshape(tokens // blk, blk, width), vt, kmean)


def _ssm_matrices(lam_re, lam_im, log_dt, b_re, b_im, c_re, c_im, d_skip):
    T, G, P, Hc = SSM_CHUNK, SSM_GROUPS, SSM_STATE, SSM_GROUP
    dt = jnp.exp(log_dt)[:, None]
    mag = jnp.exp(lam_re * dt)
    ar = mag * jnp.cos(lam_im * dt)
    ai = mag * jnp.sin(lam_im * dt)
    nr, ni = ar - 1.0, ai
    den = lam_re * lam_re + lam_im * lam_im
    fr = (nr * lam_re + ni * lam_im) / den
    fi = (ni * lam_re - nr * lam_im) / den
    bbr = fr[..., None] * b_re - fi[..., None] * b_im
    bbi = fr[..., None] * b_im + fi[..., None] * b_re
    pr, pi = [jnp.ones_like(ar)], [jnp.zeros_like(ar)]
    for _ in range(T):
        pr, pi = pr + [pr[-1] * ar - pi[-1] * ai], pi + [pr[-1] * ai + pi[-1] * ar]
    pw_r, pw_i = jnp.stack(pr), jnp.stack(pi)

    ca_r = c_re[None] * pw_r[:, :, None, :] - c_im[None] * pw_i[:, :, None, :]
    ca_i = c_re[None] * pw_i[:, :, None, :] + c_im[None] * pw_r[:, :, None, :]
    kern = jnp.sum(ca_r[:T, :, :, :, None] * bbr[None, :, None, :, :]
                   - ca_i[:T, :, :, :, None] * bbi[None, :, None, :, :], axis=3)
    skip = d_skip.reshape(G, Hc)[:, :, None] * jnp.eye(Hc, dtype=F32)[None]
    kern = kern.at[0].add(skip)

    GB = SSM_LANE_GROUPS
    CB = G // GB
    k_c = kern.transpose(1, 0, 3, 2).reshape(CB, GB, T, Hc, Hc).transpose(0, 2, 1, 3, 4)
    k_c = jnp.tile(k_c.reshape(CB, T, GB * Hc, Hc), (1, 1, 1, GB))
    rev_r, rev_i = pw_r[T - 1::-1], pw_i[T - 1::-1]
    n_r = rev_r[..., None] * bbr[None] - rev_i[..., None] * bbi[None]
    n_i = rev_r[..., None] * bbi[None] + rev_i[..., None] * bbr[None]
    n_c = jnp.concatenate([n_r, n_i], axis=2).transpose(0, 1, 3, 2)
    n_c = n_c.reshape(T, CB, GB, Hc, 2 * P).transpose(1, 0, 2, 3, 4).reshape(CB, T * LANES, 2 * P)
    m_c = jnp.stack([ca_r[1:], -ca_i[1:]])
    m_c = m_c.reshape(2, T, CB, GB, Hc, P).transpose(2, 0, 3, 5, 1, 4).reshape(
        CB, 2 * GB * P, T * Hc)
    a_chunk = jnp.stack([pw_r[T].reshape(1, G * P), pw_i[T].reshape(1, G * P)])
    return k_c.astype(BF16), n_c.astype(BF16), m_c.astype(BF16), a_chunk


def _div(x, d):
    return x >> (d.bit_length() - 1)


def _mod(x, d):
    return x & (d - 1)


def _spread_block_diag(compact, src_of_col, row_group, col_group):
    n_rows, n_src = compact.shape
    n_cols = n_rows
    src = lax.broadcasted_iota(jnp.int32, (n_src, n_cols), 0)
    col = lax.broadcasted_iota(jnp.int32, (n_src, n_cols), 1)
    spread = jnp.where(src == src_of_col(col), 1.0, 0.0).astype(BF16)
    full = jnp.dot(compact.astype(BF16), spread, preferred_element_type=F32)
    r = lax.broadcasted_iota(jnp.int32, full.shape, 0)
    c = lax.broadcasted_iota(jnp.int32, full.shape, 1)
    return jnp.where(row_group(r) == col_group(c), full, 0.0).astype(BF16)


def _chunk_rows(u_ref):
    n = u_ref.shape[0] // SSM_CHUNK
    return jnp.concatenate([u_ref[pl.ds(s, n, stride=SSM_CHUNK), :] for s in range(SSM_CHUNK)],
                           axis=1).astype(BF16)


def _ssm_kernel(*refs, tiles_per_seq):
    T, GB, P, Hc = SSM_CHUNK, SSM_LANE_GROUPS, SSM_STATE, SSM_GROUP
    nb = SSM_WIDTH // LANES
    u_refs, (k_c_ref, n_c_ref, m_c_ref, a_ref) = refs[:nb], refs[nb:nb + 4]
    y_refs = refs[nb + 4:2 * nb + 4]
    n_s, w_s, m_s, er_s, ei_s, hr_s, hi_s, cr_s, ci_s = refs[2 * nb + 4:]
    half = GB * P
    step = pl.program_id(0)

    @pl.when(step == 0)
    def _():
        r = lax.broadcasted_iota(jnp.int32, (LANES, LANES), 0)
        c = lax.broadcasted_iota(jnp.int32, (LANES, LANES), 1)
        same_group = _div(r, Hc) == _div(c, Hc)
        for cb in range(nb):
            n_s[cb] = _spread_block_diag(
                n_c_ref[cb], lambda c: _div(c, GB * P) * P + _mod(c, P),
                lambda r: _mod(_div(r, Hc), GB), lambda c: _mod(_div(c, P), GB))
            w_s[cb] = jnp.zeros(w_s.shape[1:], BF16)
            for tau in range(T):
                blk = jnp.where(same_group, k_c_ref[cb, tau], 0.0).astype(BF16)
                for s in range(T - tau):
                    w_s[cb, s * LANES:(s + 1) * LANES,
                        (s + tau) * LANES:(s + tau + 1) * LANES] = blk
            m_s[cb] = _spread_block_diag(
                m_c_ref[cb], lambda c: _div(c, LANES) * Hc + _mod(c, Hc),
                lambda r: _mod(_div(r, P), GB), lambda c: _mod(_div(c, Hc), GB))

    @pl.when(step % tiles_per_seq == 0)
    def _():
        cr_s[...] = jnp.zeros_like(cr_s)
        ci_s[...] = jnp.zeros_like(ci_s)

    u_rows = [_chunk_rows(u_ref) for u_ref in u_refs]
    for cb in range(nb):
        e = jnp.dot(u_rows[cb], n_s[cb], preferred_element_type=F32)
        er_s[:, cb * half:(cb + 1) * half] = e[:, :half]
        ei_s[:, cb * half:(cb + 1) * half] = e[:, half:]

    n = er_s.shape[0]
    for cb in range(nb):
        for lo in range(0, T * LANES, MXU_DEPTH):
            hi = lo + MXU_DEPTH
            y = jnp.dot(u_rows[cb][:, :hi], w_s[cb, :hi, lo:hi], preferred_element_type=F32)
            for t in range(lo // LANES, hi // LANES):
                y_refs[cb][pl.ds(t, n, stride=T), :] = y[:, t * LANES - lo:(t + 1) * LANES - lo]

    ar = a_ref[0]
    ai = a_ref[1]
    hr, hi = cr_s[...], ci_s[...]
    for c in range(n):
        hr_s[c:c + 1, :] = hr
        hi_s[c:c + 1, :] = hi
        hr, hi = (ar * hr - ai * hi + er_s[c:c + 1, :], ar * hi + ai * hr + ei_s[c:c + 1, :])
    cr_s[...] = hr
    ci_s[...] = hi

    for cb in range(nb):
        cols = slice(cb * half, (cb + 1) * half)
        h = jnp.concatenate([hr_s[:, cols], hi_s[:, cols]], axis=1).astype(BF16)
        y = jnp.dot(h, m_s[cb], preferred_element_type=F32)
        for t in range(T):
            rows_t = pl.ds(t, n, stride=T)
            y_refs[cb][rows_t, :] = y_refs[cb][rows_t, :] + y[:, t * LANES:(t + 1) * LANES]


def _s5_ssm(u, n_batch, ops):
    k_c, n_c, m_c, a_chunk = ops
    tokens = u.shape[0]
    T = SSM_CHUNK
    nb = SSM_WIDTH // LANES
    rt = SSM_ROW_TILE
    ct = rt // T
    op_dim = T * LANES
    n_state = SSM_GROUPS * SSM_STATE
    assert (tokens // n_batch) % rt == 0
    u_specs = [pl.BlockSpec((rt, LANES), functools.partial(lambda cb, r: (r, cb), cb))
               for cb in range(nb)]
    y_spec = pl.BlockSpec((rt, LANES), lambda r: (r, 0))
    return pl.pallas_call(
        functools.partial(_ssm_kernel, tiles_per_seq=tokens // n_batch // rt),
        grid=(tokens // rt,),
        in_specs=u_specs + [_const_spec(k_c.shape), _const_spec(n_c.shape),
                            _const_spec(m_c.shape), _const_spec(a_chunk.shape)],
        out_specs=[y_spec] * nb,
        out_shape=[jax.ShapeDtypeStruct((tokens, LANES), F32)] * nb,
        scratch_shapes=[pltpu.VMEM((nb, op_dim, op_dim), BF16)] * 3
        + [pltpu.VMEM((ct, n_state), F32)] * 4 + [pltpu.VMEM((1, n_state), F32)] * 2,
        compiler_params=pltpu.CompilerParams(dimension_semantics=("arbitrary",),
                                             vmem_limit_bytes=VMEM_LIMIT_BYTES),
        name="ssm",
    )(*([u] * nb), k_c, n_c, m_c, a_chunk)


def _post_kernel(x1_ref, attn_ref, *refs):
    n_y = SSM_WIDTH // LANES
    y_refs = refs[:n_y]
    (ga_ref, gs_ref, wup_ref, wglu_ref, wout_ref, n3_ref, wg_ref, wu_ref, wd_ref, nf_ref,
     o_ref) = refs[n_y:]
    groups = _row_groups(x1_ref.shape[0])
    dot = functools.partial(jnp.dot, preferred_element_type=F32)
    attn = [dot(attn_ref[r, :], wup_ref[...]) for r in groups]
    glu = []
    for r in groups:
        y = jnp.concatenate([y_ref[r, :] for y_ref in y_refs], axis=1)
        glu.append(dot(jax.nn.gelu(y, approximate=True).astype(BF16), wglu_ref[...]))
    x2 = []
    for g, r in enumerate(groups):
        ssm_out = glu[g][:, :D_MODEL] * jax.nn.sigmoid(glu[g][:, D_MODEL:])
        merged = (jax.nn.sigmoid(ga_ref[r, :]) * attn[g]
                  + jax.nn.sigmoid(gs_ref[r, :]) * ssm_out)
        x2.append(x1_ref[r, :] + dot(merged.astype(BF16), wout_ref[...]))
    x3 = _swiglu_half_step(x2, n3_ref[...], wg_ref, wu_ref, wd_ref)
    for r, x3_g in zip(groups, x3):
        o_ref[r, :] = _rms(x3_g, nf_ref[...])


def _post(x1, attn, ys, ga, gs, wup, wglu, wout, n3, wg, wu, wd, nf):
    tokens = x1.shape[0]
    tm = TOKEN_TILE
    row = lambda w: pl.BlockSpec((tm, w), lambda i: (i, 0))
    consts = (wup, wglu, wout, n3, wg, wu, wd, nf)
    return pl.pallas_call(
        _post_kernel,
        grid=(tokens // tm,),
        in_specs=[row(D_MODEL), row(ATTN_WIDTH)] + [row(LANES)] * len(ys)
        + [row(D_MODEL), row(D_MODEL)] + [_const_spec(c.shape) for c in consts],
        out_specs=row(D_MODEL),
        out_shape=jax.ShapeDtypeStruct((tokens, D_MODEL), F32),
        compiler_params=pltpu.CompilerParams(
            dimension_semantics=("arbitrary",), vmem_limit_bytes=VMEM_LIMIT_BYTES),
        name="post",
    )(x1, attn, *ys, ga, gs, *consts)


def kernel(x, ffn1_norm, ffn1_w_gate, ffn1_w_up, ffn1_w_down, mix_norm, w_in, w_attn_up, ssm_lambda_re, ssm_lambda_im, ssm_log_dt, ssm_b_re, ssm_b_im, ssm_c_re, ssm_c_im, ssm_d, w_ssm_glu, w_out, ffn2_norm, ffn2_w_gate, ffn2_w_up, ffn2_w_down, final_norm):
    B, S, D = x.shape
    depth = ffn1_norm.shape[0]
    assert depth == 1, "final norm is fused into the single layer's last stage"
    tokens = B * S
    slopes = jnp.asarray(2.0 ** (-8.0 * np.arange(1, N_HEADS + 1) / N_HEADS), dtype=F32)
    bf = lambda w: w.astype(BF16)
    xt = x.reshape(tokens, D)
    for l in range(depth):
        x1, q, k, vt, u, ga, gs, kmean = _ffn1_proj(
            xt, ffn1_norm[l][None], bf(ffn1_w_gate[l]), bf(ffn1_w_up[l]), bf(ffn1_w_down[l]),
            mix_norm[l][None], bf(w_in[l]))
        attn = _moba_attention(q, k, vt, kmean.reshape(-1, ATTN_WIDTH), slopes, B)
        mats = _ssm_matrices(ssm_lambda_re[l], ssm_lambda_im[l], ssm_log_dt[l], ssm_b_re[l],
                             ssm_b_im[l], ssm_c_re[l], ssm_c_im[l], ssm_d[l])
        ys = _s5_ssm(u, B, mats)
        xt = _post(x1, attn, ys, ga, gs, bf(w_attn_up[l]), bf(w_ssm_glu[l]), bf(w_out[l]),
                   ffn2_norm[l][None], bf(ffn2_w_gate[l]), bf(ffn2_w_up[l]), bf(ffn2_w_down[l]),
                   final_norm[None])
    return xt.reshape(B, S, D)
```

```python
import functools
import math

import numpy as np
import jax
import jax.numpy as jnp
from jax import lax
from jax.experimental import pallas as pl
from jax.experimental.pallas import tpu as pltpu

F32 = jnp.float32
BF16 = jnp.bfloat16

D_MODEL = 1024
N_HEADS = 8
HEAD_DIM = 64
ATTN_WIDTH = N_HEADS * HEAD_DIM
MOBA_BLOCK = 256
MOBA_TOPK = 3
SSM_GROUP = 16
SSM_GROUPS = 32
SSM_WIDTH = SSM_GROUP * SSM_GROUPS
SSM_STATE = 64
EPS = 1e-6
LOG2_E = math.log2(math.e)

VMEM_LIMIT_BYTES = 56 * 1024 * 1024
TOKEN_TILE = MOBA_BLOCK
ROW_SPLITS = 2
HEADS_PER_STEP = 8
SOFTMAX_ROWS = 32
MXU_DEPTH = 256
BF16_ROWS = 16
V_ROWS = HEAD_DIM + BF16_ROWS
MASKED = -1e30
LANES = 128
SSM_CHUNK = 8
SSM_LANE_GROUPS = LANES // SSM_GROUP
SSM_ROW_TILE = 2048


def _const_spec(shape):
    nd = len(shape)
    return pl.BlockSpec(shape, lambda *_: (0,) * nd, pipeline_mode=pl.Buffered(1))


def _rms(x, g):
    return x * lax.rsqrt(jnp.mean(x * x, axis=-1, keepdims=True) + EPS) * g


def _row_groups(n_rows):
    rows = n_rows // ROW_SPLITS
    return [slice(g * rows, (g + 1) * rows) for g in range(ROW_SPLITS)]


def _swiglu_half_step(xs, g_norm, wg_ref, wu_ref, wd_ref):
    gate, up = [], []
    for x in xs:
        h = _rms(x, g_norm).astype(BF16)
        gate.append(jnp.dot(h, wg_ref[...], preferred_element_type=F32))
        up.append(jnp.dot(h, wu_ref[...], preferred_element_type=F32))
    out = []
    for x, g, u in zip(xs, gate, up):
        act = (g * jax.nn.sigmoid(g) * u).astype(BF16)
        out.append(x + 0.5 * jnp.dot(act, wd_ref[...], preferred_element_type=F32))
    return out


def _ffn1_proj_kernel(x_ref, n1_ref, wg_ref, wu_ref, wd_ref, n2_ref, win_ref,
                      x1_ref, q_ref, k_ref, vt_ref, u_ref, ga_ref, gs_ref, kmean_ref, v_s):
    groups = _row_groups(x_ref.shape[0])
    x1 = _swiglu_half_step([x_ref[r, :] for r in groups], n1_ref[...], wg_ref, wu_ref, wd_ref)
    hs = []
    for r, x1_g in zip(groups, x1):
        x1_ref[r, :] = x1_g
        hs.append(_rms(x1_g, n2_ref[...]).astype(BF16))
    k_sum = None
    col = 0
    for ref in (q_ref, k_ref, v_s, u_ref, ga_ref, gs_ref):
        width = ref.shape[-1]
        for r, h in zip(groups, hs):
            y = jnp.dot(h, win_ref[:, col:col + width], preferred_element_type=F32)
            ref[r, :] = y.astype(ref.dtype)
            if ref is k_ref:
                part = jnp.sum(y, axis=0, keepdims=True)
                k_sum = part if k_sum is None else k_sum + part
        col += width
    kmean_ref[...] = k_sum / x_ref.shape[0]
    vt = v_s[...].T.astype(BF16)
    ones_row = jnp.where(lax.broadcasted_iota(jnp.int32, (BF16_ROWS, vt.shape[1]), 0) == 0,
                         1.0, 0.0).astype(BF16)
    for hh in range(N_HEADS):
        vt_ref[hh * V_ROWS:hh * V_ROWS + HEAD_DIM, :] = vt[hh * HEAD_DIM:(hh + 1) * HEAD_DIM]
        vt_ref[hh * V_ROWS + HEAD_DIM:(hh + 1) * V_ROWS, :] = ones_row


def _ffn1_proj(x, n1, wg, wu, wd, n2, win):
    tokens = x.shape[0]
    tm = TOKEN_TILE
    assert tm == MOBA_BLOCK
    row = lambda w: pl.BlockSpec((tm, w), lambda i: (i, 0))
    per_block = lambda r, c: pl.BlockSpec((None, r, c), lambda i: (i, 0, 0))
    rows = lambda w, dt: jax.ShapeDtypeStruct((tokens, w), dt)
    blocks = lambda r, c, dt: jax.ShapeDtypeStruct((tokens // tm, r, c), dt)
    return pl.pallas_call(
        _ffn1_proj_kernel,
        grid=(tokens // tm,),
        in_specs=[row(D_MODEL), _const_spec(n1.shape), _const_spec(wg.shape),
                  _const_spec(wu.shape), _const_spec(wd.shape), _const_spec(n2.shape),
                  _const_spec(win.shape)],
        out_specs=[row(D_MODEL), row(ATTN_WIDTH), row(ATTN_WIDTH),
                   per_block(N_HEADS * V_ROWS, tm), row(SSM_WIDTH), row(D_MODEL), row(D_MODEL),
                   per_block(1, ATTN_WIDTH)],
        out_shape=[rows(D_MODEL, F32), rows(ATTN_WIDTH, F32), rows(ATTN_WIDTH, BF16),
                   blocks(N_HEADS * V_ROWS, tm, BF16), rows(SSM_WIDTH, F32), rows(D_MODEL, F32),
                   rows(D_MODEL, F32), blocks(1, ATTN_WIDTH, F32)],
        scratch_shapes=[pltpu.VMEM((tm, ATTN_WIDTH), F32)],
        compiler_params=pltpu.CompilerParams(
            dimension_semantics=("arbitrary",), vmem_limit_bytes=VMEM_LIMIT_BYTES),
        name="ffn1_proj",
    )(x, n1, wg, wu, wd, n2, win)


def _split3(x):
    hi = x.astype(BF16).astype(F32)
    mid = (x - hi).astype(BF16).astype(F32)
    return hi, mid, x - hi - mid


def _moba_kernel(slopes_ref, q_ref, k_ref, vt_ref, kmean_ref, o_ref, kms_ref, kaug_ref, qa_ref,
                 t_a, t_b, p_a, p_b, acc_ref):
    hg = pl.program_id(1)
    i = pl.program_id(2)
    blk = MOBA_BLOCK
    n_blocks = k_ref.shape[0]
    n_heads = HEADS_PER_STEP
    pair_w = MXU_DEPTH // 2
    pos_col = n_blocks
    neg_inf = -jnp.inf
    slopes = [slopes_ref[hg * n_heads + hh] * LOG2_E for hh in range(n_heads)]
    v_rows = lambda hh: slice(hh * V_ROWS, (hh + 1) * V_ROWS)
    aug_id = lax.broadcasted_iota(jnp.int32, (BF16_ROWS, blk), 0)
    pair_row_head = lax.broadcasted_iota(jnp.int32, (pair_w, blk), 0) // HEAD_DIM
    aug_col = lax.broadcasted_iota(jnp.int32, kaug_ref.shape[1:], 1)

    @pl.when(i == 0)
    def _():
        kmean = kmean_ref[...]
        lane_head = lax.broadcasted_iota(jnp.int32, kmean.shape, 1) // HEAD_DIM
        km = jnp.concatenate([jnp.where(lane_head == hh, kmean, 0.0) for hh in range(n_heads)],
                             axis=0)
        km_hi = km.astype(BF16)
        kms_ref[...] = jnp.concatenate([km_hi, (km - km_hi.astype(F32)).astype(BF16)], axis=0)
        key = lax.broadcasted_iota(jnp.int32, kaug_ref.shape[1:], 0).astype(F32)
        in_block = jnp.where((aug_col >= pos_col) & (aug_col < pos_col + 3), key, 0.0)
        is_offset = (aug_col >= pos_col + 3) & (aug_col < pos_col + 6)
        for j in range(n_blocks):
            kaug_ref[j] = jnp.where(aug_col == j, 1.0,
                                    jnp.where(is_offset, float(j * blk), in_block)).astype(BF16)

    qt = q_ref[...].T

    for hh in range(n_heads):
        pair = hh * HEAD_DIM // pair_w
        in_pair = slice(pair * pair_w, (pair + 1) * pair_w)
        qa_ref[hh, :pair_w, :] = (jnp.where(pair_row_head == hh - pair * (pair_w // HEAD_DIM),
                                            qt[in_pair], 0.0)
                                  * (HEAD_DIM ** -0.5 * LOG2_E)).astype(BF16)
        qa_ref[hh, pair_w:pair_w + n_blocks, :] = jnp.zeros((n_blocks, blk), BF16)
        pieces = _split3(jnp.full((1, blk), slopes[hh], F32)) * 2
        tile = jnp.zeros((BF16_ROWS, blk), F32)
        for n, piece in enumerate(pieces):
            tile = jnp.where(aug_id == n, piece, tile)
        qa_ref[hh, pair_w + pos_col:pair_w + pos_col + BF16_ROWS, :] = tile.astype(BF16)
        qa_ref[hh, pair_w + pos_col + BF16_ROWS:, :] = jnp.zeros(
            (MXU_DEPTH - pair_w - pos_col - BF16_ROWS, blk), BF16)

    def block_gates():
        qt_hi = qt.astype(BF16)
        qt_lo = (qt - qt_hi.astype(F32)).astype(BF16)
        gate_hi = jnp.dot(kms_ref[...], qt_hi, preferred_element_type=F32)
        return (gate_hi[:n_heads * n_blocks] + gate_hi[n_heads * n_blocks:]
                + jnp.dot(kms_ref[:n_heads * n_blocks, :], qt_lo, preferred_element_type=F32))

    def select_blocks(gates):
        shape = (n_heads, n_blocks, blk)
        blk_id = lax.broadcasted_iota(jnp.int32, shape, 1).astype(F32)
        g = jnp.where(blk_id < i.astype(F32), gates.reshape(shape), neg_inf)
        sel = jnp.zeros(shape, jnp.bool_)
        for _ in range(MOBA_TOPK):
            top = jnp.max(g, axis=1, keepdims=True)
            first = jnp.min(jnp.where(g == top, blk_id, float(n_blocks)), axis=1, keepdims=True)
            pick = (blk_id == first) & (top > neg_inf)
            sel = sel | pick
            g = jnp.where(pick, neg_inf, g)
        mask_rows = jnp.where(sel, 0.0, MASKED).astype(BF16)
        for hh in range(n_heads):
            qa_ref[hh, pair_w:pair_w + n_blocks, :] = mask_rows[hh]

    last_block = n_blocks - 1
    chunks = [slice(c, c + SOFTMAX_ROWS) for c in range(0, blk, SOFTMAX_ROWS)]

    def qk_into(t_ref, j, own):
        kj = k_ref[j]
        extras = kaug_ref[j]
        col_max = []
        for hh in range(n_heads):
            pair = hh * HEAD_DIM // pair_w
            keys = jnp.concatenate([kj[:, pair * pair_w:(pair + 1) * pair_w], extras], axis=1)
            t = jnp.dot(keys, qa_ref[hh], preferred_element_type=F32)
            t_ref[hh] = t
            col_max.append(None if own else jnp.max(t, axis=0, keepdims=True))
        return tuple(col_max)

    def pv(p_ref, j, alpha):
        vj = vt_ref[j]
        for hh in range(n_heads):
            acc_ref[hh] = alpha[hh] * acc_ref[hh] + jnp.dot(vj[v_rows(hh)], p_ref[hh],
                                                            preferred_element_type=F32)

    def softmax_into(t_ref, p_ref, hh, m, cmax):
        causal = cmax is None

        def logits(c):
            t = t_ref[hh, c, :]
            if not causal:
                return t
            key = lax.broadcasted_iota(jnp.int32, t.shape, 0) + c.start
            query = lax.broadcasted_iota(jnp.int32, t.shape, 1)
            return jnp.where(key <= query, t, neg_inf)

        if causal:
            cmax = logits(chunks[0])
            for c in chunks[1:]:
                cmax = jnp.maximum(cmax, logits(c))
            cmax = jnp.max(cmax, axis=0, keepdims=True)
        m_new = cmax if m is None else jnp.maximum(m, cmax)
        for c in chunks:
            p_ref[hh, c, :] = jnp.exp2(logits(c) - m_new).astype(BF16)
        alpha = jnp.ones_like(m_new) if m is None else jnp.exp2(m - m_new)
        return alpha, m_new

    gates = block_gates()
    qk_into(t_b, i, True)
    select_blocks(gates)
    cmax_first = qk_into(t_a, 0, False)
    own = [softmax_into(t_b, p_b, hh, None, None) for hh in range(n_heads)]
    alpha0, m0 = (tuple(x) for x in zip(*own))
    acc_ref[...] = jnp.zeros_like(acc_ref)

    def body(n, carry):
        prev_j, a_prev, m, cmax_a = carry
        j0 = 2 * n
        j1 = jnp.minimum(j0 + 1, last_block)
        cmax_b = qk_into(t_b, j1, False)
        pv(p_b, prev_j, a_prev)
        st0 = [softmax_into(t_a, p_a, hh, m[hh], cmax_a[hh]) for hh in range(n_heads)]
        a0, m = (tuple(x) for x in zip(*st0))
        cmax_a = qk_into(t_a, jnp.minimum(j0 + 2, last_block), False)
        pv(p_a, j0, a0)
        st1 = [softmax_into(t_b, p_b, hh, m[hh], cmax_b[hh]) for hh in range(n_heads)]
        a1, m = (tuple(x) for x in zip(*st1))
        return j1, a1, m, cmax_a

    def double_body(n, carry):
        return body(2 * n + 1, body(2 * n, carry))

    n_pairs = (i + 1) // 2
    carry = lax.fori_loop(0, n_pairs // 2, double_body, (i, alpha0, m0, cmax_first))
    prev_j, a_prev, _, _ = lax.fori_loop(n_pairs // 2 * 2, n_pairs, body, carry)
    pv(p_b, prev_j, a_prev)
    out_t = jnp.concatenate([acc_ref[hh, :HEAD_DIM, :] / acc_ref[hh, HEAD_DIM:HEAD_DIM + 1, :]
                             for hh in range(n_heads)], axis=0)
    o_ref[...] = out_t.T.astype(o_ref.dtype)


def _moba_attention(q, k, vt, kmean, slopes, n_batch):
    tokens, width = q.shape
    blk = MOBA_BLOCK
    n_blocks = tokens // n_batch // blk
    step_width = HEADS_PER_STEP * HEAD_DIM
    q_spec = pl.BlockSpec((blk, step_width), lambda b, g, i: (b * n_blocks + i, g))
    return pl.pallas_call(
        _moba_kernel,
        grid=(n_batch, width // step_width, n_blocks),
        in_specs=[pl.BlockSpec(memory_space=pltpu.SMEM), q_spec,
                  pl.BlockSpec((n_blocks, blk, step_width), lambda b, g, i: (b, 0, g)),
                  pl.BlockSpec((n_blocks, HEADS_PER_STEP * V_ROWS, blk),
                               lambda b, g, i: (b, g, 0)),
                  pl.BlockSpec((n_blocks, step_width), lambda b, g, i: (b, g))],
        out_specs=q_spec,
        out_shape=jax.ShapeDtypeStruct((tokens, width), BF16),
        scratch_shapes=[pltpu.VMEM((2 * HEADS_PER_STEP * n_blocks, step_width), BF16),
                        pltpu.VMEM((n_blocks, blk, MXU_DEPTH // 2), BF16),
                        pltpu.VMEM((HEADS_PER_STEP, MXU_DEPTH, blk), BF16),
                        pltpu.VMEM((HEADS_PER_STEP, blk, blk), F32),
                        pltpu.VMEM((HEADS_PER_STEP, blk, blk), F32),
                        pltpu.VMEM((HEADS_PER_STEP, blk, blk), BF16),
                        pltpu.VMEM((HEADS_PER_STEP, blk, blk), BF16),
                        pltpu.VMEM((HEADS_PER_STEP, V_ROWS, blk), F32)],
        compiler_params=pltpu.CompilerParams(
            dimension_semantics=("arbitrary", "arbitrary", "arbitrary"),
            vmem_limit_bytes=VMEM_LIMIT_BYTES),
        name="moba_attn",
    )(slopes, q, k.reshape(tokens // blk, blk, width), vt, kmean)


def _ssm_matrices(lam_re, lam_im, log_dt, b_re, b_im, c_re, c_im, d_skip):
    T, G, P, Hc = SSM_CHUNK, SSM_GROUPS, SSM_STATE, SSM_GROUP
    dt = jnp.exp(log_dt)[:, None]
    mag = jnp.exp(lam_re * dt)
    ar = mag * jnp.cos(lam_im * dt)
    ai = mag * jnp.sin(lam_im * dt)
    nr, ni = ar - 1.0, ai
    den = lam_re * lam_re + lam_im * lam_im
    fr = (nr * lam_re + ni * lam_im) / den
    fi = (ni * lam_re - nr * lam_im) / den
    bbr = fr[..., None] * b_re - fi[..., None] * b_im
    bbi = fr[..., None] * b_im + fi[..., None] * b_re
    pr, pi = [jnp.ones_like(ar)], [jnp.zeros_like(ar)]
    for _ in range(T):
        pr, pi = pr + [pr[-1] * ar - pi[-1] * ai], pi + [pr[-1] * ai + pi[-1] * ar]
    pw_r, pw_i = jnp.stack(pr), jnp.stack(pi)

    ca_r = c_re[None] * pw_r[:, :, None, :] - c_im[None] * pw_i[:, :, None, :]
    ca_i = c_re[None] * pw_i[:, :, None, :] + c_im[None] * pw_r[:, :, None, :]
    kern = jnp.sum(ca_r[:T, :, :, :, None] * bbr[None, :, None, :, :]
                   - ca_i[:T, :, :, :, None] * bbi[None, :, None, :, :], axis=3)
    skip = d_skip.reshape(G, Hc)[:, :, None] * jnp.eye(Hc, dtype=F32)[None]
    kern = kern.at[0].add(skip)

    GB = SSM_LANE_GROUPS
    CB = G // GB
    k_c = kern.transpose(1, 0, 3, 2).reshape(CB, GB, T, Hc, Hc).transpose(0, 2, 1, 3, 4)
    k_c = jnp.tile(k_c.reshape(CB, T, GB * Hc, Hc), (1, 1, 1, GB))
    rev_r, rev_i = pw_r[T - 1::-1], pw_i[T - 1::-1]
    n_r = rev_r[..., None] * bbr[None] - rev_i[..., None] * bbi[None]
    n_i = rev_r[..., None] * bbi[None] + rev_i[..., None] * bbr[None]
    n_c = jnp.concatenate([n_r, n_i], axis=2).transpose(0, 1, 3, 2)
    n_c = n_c.reshape(T, CB, GB, Hc, 2 * P).transpose(1, 0, 2, 3, 4).reshape(CB, T * LANES, 2 * P)
    m_c = jnp.stack([ca_r[1:], -ca_i[1:]])
    m_c = m_c.reshape(2, T, CB, GB, Hc, P).transpose(2, 0, 3, 5, 1, 4).reshape(
        CB, 2 * GB * P, T * Hc)
    a_chunk = jnp.stack([pw_r[T].reshape(1, G * P), pw_i[T].reshape(1, G * P)])
    return k_c.astype(BF16), n_c.astype(BF16), m_c.astype(BF16), a_chunk


def _div(x, d):
    return x >> (d.bit_length() - 1)


def _mod(x, d):
    return x & (d - 1)


def _spread_block_diag(compact, src_of_col, row_group, col_group):
    n_rows, n_src = compact.shape
    n_cols = n_rows
    src = lax.broadcasted_iota(jnp.int32, (n_src, n_cols), 0)
    col = lax.broadcasted_iota(jnp.int32, (n_src, n_cols), 1)
    spread = jnp.where(src == src_of_col(col), 1.0, 0.0).astype(BF16)
    full = jnp.dot(compact.astype(BF16), spread, preferred_element_type=F32)
    r = lax.broadcasted_iota(jnp.int32, full.shape, 0)
    c = lax.broadcasted_iota(jnp.int32, full.shape, 1)
    return jnp.where(row_group(r) == col_group(c), full, 0.0).astype(BF16)


def _chunk_rows(u_ref):
    n = u_ref.shape[0] // SSM_CHUNK
    return jnp.concatenate([u_ref[pl.ds(s, n, stride=SSM_CHUNK), :] for s in range(SSM_CHUNK)],
                           axis=1).astype(BF16)


def _ssm_kernel(*refs, tiles_per_seq):
    T, GB, P, Hc = SSM_CHUNK, SSM_LANE_GROUPS, SSM_STATE, SSM_GROUP
    nb = SSM_WIDTH // LANES
    u_refs, (k_c_ref, n_c_ref, m_c_ref, a_ref) = refs[:nb], refs[nb:nb + 4]
    y_refs = refs[nb + 4:2 * nb + 4]
    n_s, w_s, m_s, er_s, ei_s, hr_s, hi_s, cr_s, ci_s = refs[2 * nb + 4:]
    half = GB * P
    step = pl.program_id(0)

    @pl.when(step == 0)
    def _():
        r = lax.broadcasted_iota(jnp.int32, (LANES, LANES), 0)
        c = lax.broadcasted_iota(jnp.int32, (LANES, LANES), 1)
        same_group = _div(r, Hc) == _div(c, Hc)
        for cb in range(nb):
            n_s[cb] = _spread_block_diag(
                n_c_ref[cb], lambda c: _div(c, GB * P) * P + _mod(c, P),
                lambda r: _mod(_div(r, Hc), GB), lambda c: _mod(_div(c, P), GB))
            w_s[cb] = jnp.zeros(w_s.shape[1:], BF16)
            for tau in range(T):
                blk = jnp.where(same_group, k_c_ref[cb, tau], 0.0).astype(BF16)
                for s in range(T - tau):
                    w_s[cb, s * LANES:(s + 1) * LANES,
                        (s + tau) * LANES:(s + tau + 1) * LANES] = blk
            m_s[cb] = _spread_block_diag(
                m_c_ref[cb], lambda c: _div(c, LANES) * Hc + _mod(c, Hc),
                lambda r: _mod(_div(r, P), GB), lambda c: _mod(_div(c, Hc), GB))

    @pl.when(step % tiles_per_seq == 0)
    def _():
        cr_s[...] = jnp.zeros_like(cr_s)
        ci_s[...] = jnp.zeros_like(ci_s)

    u_rows = [_chunk_rows(u_ref) for u_ref in u_refs]
    for cb in range(nb):
        e = jnp.dot(u_rows[cb], n_s[cb], preferred_element_type=F32)
        er_s[:, cb * half:(cb + 1) * half] = e[:, :half]
        ei_s[:, cb * half:(cb + 1) * half] = e[:, half:]

    n = er_s.shape[0]
    for cb in range(nb):
        for lo in range(0, T * LANES, MXU_DEPTH):
            hi = lo + MXU_DEPTH
            y = jnp.dot(u_rows[cb][:, :hi], w_s[cb, :hi, lo:hi], preferred_element_type=F32)
            for t in range(lo // LANES, hi // LANES):
                y_refs[cb][pl.ds(t, n, stride=T), :] = y[:, t * LANES - lo:(t + 1) * LANES - lo]

    ar = a_ref[0]
    ai = a_ref[1]
    hr, hi = cr_s[...], ci_s[...]
    for c in range(n):
        hr_s[c:c + 1, :] = hr
        hi_s[c:c + 1, :] = hi
        hr, hi = (ar * hr - ai * hi + er_s[c:c + 1, :], ar * hi + ai * hr + ei_s[c:c + 1, :])
    cr_s[...] = hr
    ci_s[...] = hi

    for cb in range(nb):
        cols = slice(cb * half, (cb + 1) * half)
        h = jnp.concatenate([hr_s[:, cols], hi_s[:, cols]], axis=1).astype(BF16)
        y = jnp.dot(h, m_s[cb], preferred_element_type=F32)
        for t in range(T):
            rows_t = pl.ds(t, n, stride=T)
            y_refs[cb][rows_t, :] = y_refs[cb][rows_t, :] + y[:, t * LANES:(t + 1) * LANES]


def _s5_ssm(u, n_batch, ops):
    k_c, n_c, m_c, a_chunk = ops
    tokens = u.shape[0]
    T = SSM_CHUNK
    nb = SSM_WIDTH // LANES
    rt = SSM_ROW_TILE
    ct = rt // T
    op_dim = T * LANES
    n_state = SSM_GROUPS * SSM_STATE
    assert (tokens // n_batch) % rt == 0
    u_specs = [pl.BlockSpec((rt, LANES), functools.partial(lambda cb, r: (r, cb), cb))
               for cb in range(nb)]
    y_spec = pl.BlockSpec((rt, LANES), lambda r: (r, 0))
    return pl.pallas_call(
        functools.partial(_ssm_kernel, tiles_per_seq=tokens // n_batch // rt),
        grid=(tokens // rt,),
        in_specs=u_specs + [_const_spec(k_c.shape), _const_spec(n_c.shape),
                            _const_spec(m_c.shape), _const_spec(a_chunk.shape)],
        out_specs=[y_spec] * nb,
        out_shape=[jax.ShapeDtypeStruct((tokens, LANES), F32)] * nb,
        scratch_shapes=[pltpu.VMEM((nb, op_dim, op_dim), BF16)] * 3
        + [pltpu.VMEM((ct, n_state), F32)] * 4 + [pltpu.VMEM((1, n_state), F32)] * 2,
        compiler_params=pltpu.CompilerParams(dimension_semantics=("arbitrary",),
                                             vmem_limit_bytes=VMEM_LIMIT_BYTES),
        name="ssm",
    )(*([u] * nb), k_c, n_c, m_c, a_chunk)


def _post_kernel(x1_ref, attn_ref, *refs):
    n_y = SSM_WIDTH // LANES
    y_refs = refs[:n_y]
    (ga_ref, gs_ref, wup_ref, wglu_ref, wout_ref, n3_ref, wg_ref, wu_ref, wd_ref, nf_ref,
     o_ref) = refs[n_y:]
    groups = _row_groups(x1_ref.shape[0])
    dot = functools.partial(jnp.dot, preferred_element_type=F32)
    attn = [dot(attn_ref[r, :], wup_ref[...]) for r in groups]
    glu = []
    for r in groups:
        y = jnp.concatenate([y_ref[r, :] for y_ref in y_refs], axis=1)
        glu.append(dot(jax.nn.gelu(y, approximate=True).astype(BF16), wglu_ref[...]))
    x2 = []
    for g, r in enumerate(groups):
        ssm_out = glu[g][:, :D_MODEL] * jax.nn.sigmoid(glu[g][:, D_MODEL:])
        merged = (jax.nn.sigmoid(ga_ref[r, :]) * attn[g]
                  + jax.nn.sigmoid(gs_ref[r, :]) * ssm_out)
        x2.append(x1_ref[r, :] + dot(merged.astype(BF16), wout_ref[...]))
    x3 = _swiglu_half_step(x2, n3_ref[...], wg_ref, wu_ref, wd_ref)
    for r, x3_g in zip(groups, x3):
        o_ref[r, :] = _rms(x3_g, nf_ref[...])


def _post(x1, attn, ys, ga, gs, wup, wglu, wout, n3, wg, wu, wd, nf):
    tokens = x1.shape[0]
    tm = TOKEN_TILE
    row = lambda w: pl.BlockSpec((tm, w), lambda i: (i, 0))
    consts = (wup, wglu, wout, n3, wg, wu, wd, nf)
    return pl.pallas_call(
        _post_kernel,
        grid=(tokens // tm,),
        in_specs=[row(D_MODEL), row(ATTN_WIDTH)] + [row(LANES)] * len(ys)
        + [row(D_MODEL), row(D_MODEL)] + [_const_spec(c.shape) for c in consts],
        out_specs=row(D_MODEL),
        out_shape=jax.ShapeDtypeStruct((tokens, D_MODEL), F32),
        compiler_params=pltpu.CompilerParams(
            dimension_semantics=("arbitrary",), vmem_limit_bytes=VMEM_LIMIT_BYTES),
        name="post",
    )(x1, attn, *ys, ga, gs, *consts)


def kernel(x, ffn1_norm, ffn1_w_gate, ffn1_w_up, ffn1_w_down, mix_norm, w_in, w_attn_up, ssm_lambda_re, ssm_lambda_im, ssm_log_dt, ssm_b_re, ssm_b_im, ssm_c_re, ssm_c_im, ssm_d, w_ssm_glu, w_out, ffn2_norm, ffn2_w_gate, ffn2_w_up, ffn2_w_down, final_norm):
    B, S, D = x.shape
    depth = ffn1_norm.shape[0]
    assert depth == 1, "final norm is fused into the single layer's last stage"
    tokens = B * S
    slopes = jnp.asarray(2.0 ** (-8.0 * np.arange(1, N_HEADS + 1) / N_HEADS), dtype=F32)
    bf = lambda w: w.astype(BF16)
    xt = x.reshape(tokens, D)
    for l in range(depth):
        x1, q, k, vt, u, ga, gs, kmean = _ffn1_proj(
            xt, ffn1_norm[l][None], bf(ffn1_w_gate[l]), bf(ffn1_w_up[l]), bf(ffn1_w_down[l]),
            mix_norm[l][None], bf(w_in[l]))
        attn = _moba_attention(q, k, vt, kmean.reshape(-1, ATTN_WIDTH), slopes, B)
        mats = _ssm_matrices(ssm_lambda_re[l], ssm_lambda_im[l], ssm_log_dt[l], ssm_b_re[l],
                             ssm_b_im[l], ssm_c_re[l], ssm_c_im[l], ssm_d[l])
        ys = _s5_ssm(u, B, mats)
        xt = _post(x1, attn, ys, ga, gs, bf(w_attn_up[l]), bf(w_ssm_glu[l]), bf(w_out[l]),
                   ffn2_norm[l][None], bf(ffn2_w_gate[l]), bf(ffn2_w_up[l]), bf(ffn2_w_down[l]),
                   final_norm[None])
    return xt.reshape(B, S, D)
```

```python
import functools
import math

import numpy as np
import jax
import jax.numpy as jnp
from jax import lax
from jax.experimental import pallas as pl
from jax.experimental.pallas import tpu as pltpu

F32 = jnp.float32
BF16 = jnp.bfloat16

D_MODEL = 1024
N_HEADS = 8
HEAD_DIM = 64
ATTN_WIDTH = N_HEADS * HEAD_DIM
MOBA_BLOCK = 256
MOBA_TOPK = 3
SSM_GROUP = 16
SSM_GROUPS = 32
SSM_WIDTH = SSM_GROUP * SSM_GROUPS
SSM_STATE = 64
EPS = 1e-6
LOG2_E = math.log2(math.e)

VMEM_LIMIT_BYTES = 56 * 1024 * 1024
TOKEN_TILE = MOBA_BLOCK
ROW_SPLITS = 2
HEADS_PER_STEP = 8
PAIRS_PER_TRIP = 3
SOFTMAX_ROWS = 32
MXU_DEPTH = 256
BF16_ROWS = 16
V_ROWS = HEAD_DIM + BF16_ROWS
MASKED = -1e30
LANES = 128
SSM_CHUNK = 8
SSM_LANE_GROUPS = LANES // SSM_GROUP
SSM_ROW_TILE = 2048


def _const_spec(shape):
    nd = len(shape)
    return pl.BlockSpec(shape, lambda *_: (0,) * nd, pipeline_mode=pl.Buffered(1))


def _rms(x, g):
    return x * lax.rsqrt(jnp.mean(x * x, axis=-1, keepdims=True) + EPS) * g


def _row_groups(n_rows):
    rows = n_rows // ROW_SPLITS
    return [slice(g * rows, (g + 1) * rows) for g in range(ROW_SPLITS)]


def _swiglu_half_step(xs, g_norm, wg_ref, wu_ref, wd_ref):
    gate, up = [], []
    for x in xs:
        h = _rms(x, g_norm).astype(BF16)
        gate.append(jnp.dot(h, wg_ref[...], preferred_element_type=F32))
        up.append(jnp.dot(h, wu_ref[...], preferred_element_type=F32))
    out = []
    for x, g, u in zip(xs, gate, up):
        act = (g * jax.nn.sigmoid(g) * u).astype(BF16)
        out.append(x + 0.5 * jnp.dot(act, wd_ref[...], preferred_element_type=F32))
    return out


def _ffn1_proj_kernel(x_ref, n1_ref, wg_ref, wu_ref, wd_ref, n2_ref, win_ref,
                      x1_ref, q_ref, k_ref, vt_ref, u_ref, ga_ref, gs_ref, kmean_ref, v_s):
    groups = _row_groups(x_ref.shape[0])
    x1 = _swiglu_half_step([x_ref[r, :] for r in groups], n1_ref[...], wg_ref, wu_ref, wd_ref)
    hs = []
    for r, x1_g in zip(groups, x1):
        x1_ref[r, :] = x1_g
        hs.append(_rms(x1_g, n2_ref[...]).astype(BF16))
    k_sum = None
    col = 0
    for ref in (q_ref, k_ref, v_s, u_ref, ga_ref, gs_ref):
        width = ref.shape[-1]
        for r, h in zip(groups, hs):
            y = jnp.dot(h, win_ref[:, col:col + width], preferred_element_type=F32)
            ref[r, :] = y.astype(ref.dtype)
            if ref is k_ref:
                part = jnp.sum(y, axis=0, keepdims=True)
                k_sum = part if k_sum is None else k_sum + part
        col += width
    kmean_ref[...] = k_sum / x_ref.shape[0]
    vt = v_s[...].T.astype(BF16)
    ones_row = jnp.where(lax.broadcasted_iota(jnp.int32, (BF16_ROWS, vt.shape[1]), 0) == 0,
                         1.0, 0.0).astype(BF16)
    for hh in range(N_HEADS):
        vt_ref[hh * V_ROWS:hh * V_ROWS + HEAD_DIM, :] = vt[hh * HEAD_DIM:(hh + 1) * HEAD_DIM]
        vt_ref[hh * V_ROWS + HEAD_DIM:(hh + 1) * V_ROWS, :] = ones_row


def _ffn1_proj(x, n1, wg, wu, wd, n2, win):
    tokens = x.shape[0]
    tm = TOKEN_TILE
    assert tm == MOBA_BLOCK
    row = lambda w: pl.BlockSpec((tm, w), lambda i: (i, 0))
    per_block = lambda r, c: pl.BlockSpec((None, r, c), lambda i: (i, 0, 0))
    rows = lambda w, dt: jax.ShapeDtypeStruct((tokens, w), dt)
    blocks = lambda r, c, dt: jax.ShapeDtypeStruct((tokens // tm, r, c), dt)
    return pl.pallas_call(
        _ffn1_proj_kernel,
        grid=(tokens // tm,),
        in_specs=[row(D_MODEL), _const_spec(n1.shape), _const_spec(wg.shape),
                  _const_spec(wu.shape), _const_spec(wd.shape), _const_spec(n2.shape),
                  _const_spec(win.shape)],
        out_specs=[row(D_MODEL), row(ATTN_WIDTH), row(ATTN_WIDTH),
                   per_block(N_HEADS * V_ROWS, tm), row(SSM_WIDTH), row(D_MODEL), row(D_MODEL),
                   per_block(1, ATTN_WIDTH)],
        out_shape=[rows(D_MODEL, F32), rows(ATTN_WIDTH, F32), rows(ATTN_WIDTH, BF16),
                   blocks(N_HEADS * V_ROWS, tm, BF16), rows(SSM_WIDTH, F32), rows(D_MODEL, F32),
                   rows(D_MODEL, F32), blocks(1, ATTN_WIDTH, F32)],
        scratch_shapes=[pltpu.VMEM((tm, ATTN_WIDTH), F32)],
        compiler_params=pltpu.CompilerParams(
            dimension_semantics=("arbitrary",), vmem_limit_bytes=VMEM_LIMIT_BYTES),
        name="ffn1_proj",
    )(x, n1, wg, wu, wd, n2, win)


def _split3(x):
    hi = x.astype(BF16).astype(F32)
    mid = (x - hi).astype(BF16).astype(F32)
    return hi, mid, x - hi - mid


def _moba_kernel(slopes_ref, q_ref, k_ref, vt_ref, kmean_ref, o_ref, kms_ref, kaug_ref, qa_ref,
                 t_a, t_b, p_a, p_b, acc_ref):
    hg = pl.program_id(1)
    i = pl.program_id(2)
    blk = MOBA_BLOCK
    n_blocks = k_ref.shape[0]
    n_heads = HEADS_PER_STEP
    pair_w = MXU_DEPTH // 2
    pos_col = n_blocks
    neg_inf = -jnp.inf
    slopes = [slopes_ref[hg * n_heads + hh] * LOG2_E for hh in range(n_heads)]
    v_rows = lambda hh: slice(hh * V_ROWS, (hh + 1) * V_ROWS)
    aug_id = lax.broadcasted_iota(jnp.int32, (BF16_ROWS, blk), 0)
    pair_row_head = lax.broadcasted_iota(jnp.int32, (pair_w, blk), 0) // HEAD_DIM
    aug_col = lax.broadcasted_iota(jnp.int32, kaug_ref.shape[1:], 1)

    @pl.when(i == 0)
    def _():
        kmean = kmean_ref[...]
        lane_head = lax.broadcasted_iota(jnp.int32, kmean.shape, 1) // HEAD_DIM
        km = jnp.concatenate([jnp.where(lane_head == hh, kmean, 0.0) for hh in range(n_heads)],
                             axis=0)
        km_hi = km.astype(BF16)
        kms_ref[...] = jnp.concatenate([km_hi, (km - km_hi.astype(F32)).astype(BF16)], axis=0)
        key = lax.broadcasted_iota(jnp.int32, kaug_ref.shape[1:], 0).astype(F32)
        in_block = jnp.where((aug_col >= pos_col) & (aug_col < pos_col + 3), key, 0.0)
        is_offset = (aug_col >= pos_col + 3) & (aug_col < pos_col + 6)
        for j in range(n_blocks):
            kaug_ref[j] = jnp.where(aug_col == j, 1.0,
                                    jnp.where(is_offset, float(j * blk), in_block)).astype(BF16)

    qt = q_ref[...].T

    for hh in range(n_heads):
        pair = hh * HEAD_DIM // pair_w
        in_pair = slice(pair * pair_w, (pair + 1) * pair_w)
        qa_ref[hh, :pair_w, :] = (jnp.where(pair_row_head == hh - pair * (pair_w // HEAD_DIM),
                                            qt[in_pair], 0.0)
                                  * (HEAD_DIM ** -0.5 * LOG2_E)).astype(BF16)
        qa_ref[hh, pair_w:pair_w + n_blocks, :] = jnp.zeros((n_blocks, blk), BF16)
        pieces = _split3(jnp.full((1, blk), slopes[hh], F32)) * 2
        tile = jnp.zeros((BF16_ROWS, blk), F32)
        for n, piece in enumerate(pieces):
            tile = jnp.where(aug_id == n, piece, tile)
        qa_ref[hh, pair_w + pos_col:pair_w + pos_col + BF16_ROWS, :] = tile.astype(BF16)
        qa_ref[hh, pair_w + pos_col + BF16_ROWS:, :] = jnp.zeros(
            (MXU_DEPTH - pair_w - pos_col - BF16_ROWS, blk), BF16)

    def block_gates():
        qt_hi = qt.astype(BF16)
        qt_lo = (qt - qt_hi.astype(F32)).astype(BF16)
        gate_hi = jnp.dot(kms_ref[...], qt_hi, preferred_element_type=F32)
        return (gate_hi[:n_heads * n_blocks] + gate_hi[n_heads * n_blocks:]
                + jnp.dot(kms_ref[:n_heads * n_blocks, :], qt_lo, preferred_element_type=F32))

    def select_blocks(gates):
        shape = (n_heads, n_blocks, blk)
        blk_id = lax.broadcasted_iota(jnp.int32, shape, 1).astype(F32)
        g = jnp.where(blk_id < i.astype(F32), gates.reshape(shape), neg_inf)
        sel = jnp.zeros(shape, jnp.bool_)
        for _ in range(MOBA_TOPK):
            top = jnp.max(g, axis=1, keepdims=True)
            first = jnp.min(jnp.where(g == top, blk_id, float(n_blocks)), axis=1, keepdims=True)
            pick = (blk_id == first) & (top > neg_inf)
            sel = sel | pick
            g = jnp.where(pick, neg_inf, g)
        mask_rows = jnp.where(sel, 0.0, MASKED).astype(BF16)
        for hh in range(n_heads):
            qa_ref[hh, pair_w:pair_w + n_blocks, :] = mask_rows[hh]

    last_block = n_blocks - 1
    chunks = [slice(c, c + SOFTMAX_ROWS) for c in range(0, blk, SOFTMAX_ROWS)]

    def qk_into(t_ref, j, own):
        kj = k_ref[j]
        extras = kaug_ref[j]
        col_max = []
        for hh in range(n_heads):
            pair = hh * HEAD_DIM // pair_w
            keys = jnp.concatenate([kj[:, pair * pair_w:(pair + 1) * pair_w], extras], axis=1)
            t = jnp.dot(keys, qa_ref[hh], preferred_element_type=F32)
            t_ref[hh] = t
            col_max.append(None if own else jnp.max(t, axis=0, keepdims=True))
        return tuple(col_max)

    def pv(p_ref, j, alpha):
        vj = vt_ref[j]
        for hh in range(n_heads):
            acc_ref[hh] = alpha[hh] * acc_ref[hh] + jnp.dot(vj[v_rows(hh)], p_ref[hh],
                                                            preferred_element_type=F32)

    def softmax_into(t_ref, p_ref, hh, m, cmax):
        causal = cmax is None

        def logits(c):
            t = t_ref[hh, c, :]
            if not causal:
                return t
            key = lax.broadcasted_iota(jnp.int32, t.shape, 0) + c.start
            query = lax.broadcasted_iota(jnp.int32, t.shape, 1)
            return jnp.where(key <= query, t, neg_inf)

        if causal:
            cmax = logits(chunks[0])
            for c in chunks[1:]:
                cmax = jnp.maximum(cmax, logits(c))
            cmax = jnp.max(cmax, axis=0, keepdims=True)
        m_new = cmax if m is None else jnp.maximum(m, cmax)
        for c in chunks:
            p_ref[hh, c, :] = jnp.exp2(logits(c) - m_new).astype(BF16)
        alpha = jnp.ones_like(m_new) if m is None else jnp.exp2(m - m_new)
        return alpha, m_new

    gates = block_gates()
    qk_into(t_b, i, True)
    select_blocks(gates)
    cmax_first = qk_into(t_a, 0, False)
    own = [softmax_into(t_b, p_b, hh, None, None) for hh in range(n_heads)]
    alpha0, m0 = (tuple(x) for x in zip(*own))
    acc_ref[...] = jnp.zeros_like(acc_ref)

    def body(n, carry):
        prev_j, a_prev, m, cmax_a = carry
        j0 = 2 * n
        j1 = jnp.minimum(j0 + 1, last_block)
        cmax_b = qk_into(t_b, j1, False)
        pv(p_b, prev_j, a_prev)
        st0 = [softmax_into(t_a, p_a, hh, m[hh], cmax_a[hh]) for hh in range(n_heads)]
        a0, m = (tuple(x) for x in zip(*st0))
        cmax_a = qk_into(t_a, jnp.minimum(j0 + 2, last_block), False)
        pv(p_a, j0, a0)
        st1 = [softmax_into(t_b, p_b, hh, m[hh], cmax_b[hh]) for hh in range(n_heads)]
        a1, m = (tuple(x) for x in zip(*st1))
        return j1, a1, m, cmax_a

    def trip(n, carry):
        for k in range(PAIRS_PER_TRIP):
            carry = body(PAIRS_PER_TRIP * n + k, carry)
        return carry

    n_pairs = (i + 1) // 2
    n_trips = n_pairs // PAIRS_PER_TRIP
    carry = lax.fori_loop(0, n_trips, trip, (i, alpha0, m0, cmax_first))
    prev_j, a_prev, _, _ = lax.fori_loop(n_trips * PAIRS_PER_TRIP, n_pairs, body, carry)
    pv(p_b, prev_j, a_prev)
    out_t = jnp.concatenate([acc_ref[hh, :HEAD_DIM, :] / acc_ref[hh, HEAD_DIM:HEAD_DIM + 1, :]
                             for hh in range(n_heads)], axis=0)
    o_ref[...] = out_t.T.astype(o_ref.dtype)


def _moba_attention(q, k, vt, kmean, slopes, n_batch):
    tokens, width = q.shape
    blk = MOBA_BLOCK
    n_blocks = tokens // n_batch // blk
    step_width = HEADS_PER_STEP * HEAD_DIM
    q_spec = pl.BlockSpec((blk, step_width), lambda b, g, i: (b * n_blocks + i, g))
    return pl.pallas_call(
        _moba_kernel,
        grid=(n_batch, width // step_width, n_blocks),
        in_specs=[pl.BlockSpec(memory_space=pltpu.SMEM), q_spec,
                  pl.BlockSpec((n_blocks, blk, step_width), lambda b, g, i: (b, 0, g)),
                  pl.BlockSpec((n_blocks, HEADS_PER_STEP * V_ROWS, blk),
                               lambda b, g, i: (b, g, 0)),
                  pl.BlockSpec((n_blocks, step_width), lambda b, g, i: (b, g))],
        out_specs=q_spec,
        out_shape=jax.ShapeDtypeStruct((tokens, width), BF16),
        scratch_shapes=[pltpu.VMEM((2 * HEADS_PER_STEP * n_blocks, step_width), BF16),
                        pltpu.VMEM((n_blocks, blk, MXU_DEPTH // 2), BF16),
                        pltpu.VMEM((HEADS_PER_STEP, MXU_DEPTH, blk), BF16),
                        pltpu.VMEM((HEADS_PER_STEP, blk, blk), F32),
                        pltpu.VMEM((HEADS_PER_STEP, blk, blk), F32),
                        pltpu.VMEM((HEADS_PER_STEP, blk, blk), BF16),
                        pltpu.VMEM((HEADS_PER_STEP, blk, blk), BF16),
                        pltpu.VMEM((HEADS_PER_STEP, V_ROWS, blk), F32)],
        compiler_params=pltpu.CompilerParams(
            dimension_semantics=("arbitrary", "arbitrary", "arbitrary"),
            vmem_limit_bytes=VMEM_LIMIT_BYTES),
        name="moba_attn",
    )(slopes, q, k.reshape(tokens // blk, blk, width), vt, kmean)


def _ssm_matrices(lam_re, lam_im, log_dt, b_re, b_im, c_re, c_im, d_skip):
    T, G, P, Hc = SSM_CHUNK, SSM_GROUPS, SSM_STATE, SSM_GROUP
    dt = jnp.exp(log_dt)[:, None]
    mag = jnp.exp(lam_re * dt)
    ar = mag * jnp.cos(lam_im * dt)
    ai = mag * jnp.sin(lam_im * dt)
    nr, ni = ar - 1.0, ai
    den = lam_re * lam_re + lam_im * lam_im
    fr = (nr * lam_re + ni * lam_im) / den
    fi = (ni * lam_re - nr * lam_im) / den
    bbr = fr[..., None] * b_re - fi[..., None] * b_im
    bbi = fr[..., None] * b_im + fi[..., None] * b_re
    pr, pi = [jnp.ones_like(ar)], [jnp.zeros_like(ar)]
    for _ in range(T):
        pr, pi = pr + [pr[-1] * ar - pi[-1] * ai], pi + [pr[-1] * ai + pi[-1] * ar]
    pw_r, pw_i = jnp.stack(pr), jnp.stack(pi)

    ca_r = c_re[None] * pw_r[:, :, None, :] - c_im[None] * pw_i[:, :, None, :]
    ca_i = c_re[None] * pw_i[:, :, None, :] + c_im[None] * pw_r[:, :, None, :]
    kern = jnp.sum(ca_r[:T, :, :, :, None] * bbr[None, :, None, :, :]
                   - ca_i[:T, :, :, :, None] * bbi[None, :, None, :, :], axis=3)
    skip = d_skip.reshape(G, Hc)[:, :, None] * jnp.eye(Hc, dtype=F32)[None]
    kern = kern.at[0].add(skip)

    GB = SSM_LANE_GROUPS
    CB = G // GB
    k_c = kern.transpose(1, 0, 3, 2).reshape(CB, GB, T, Hc, Hc).transpose(0, 2, 1, 3, 4)
    k_c = jnp.tile(k_c.reshape(CB, T, GB * Hc, Hc), (1, 1, 1, GB))
    rev_r, rev_i = pw_r[T - 1::-1], pw_i[T - 1::-1]
    n_r = rev_r[..., None] * bbr[None] - rev_i[..., None] * bbi[None]
    n_i = rev_r[..., None] * bbi[None] + rev_i[..., None] * bbr[None]
    n_c = jnp.concatenate([n_r, n_i], axis=2).transpose(0, 1, 3, 2)
    n_c = n_c.reshape(T, CB, GB, Hc, 2 * P).transpose(1, 0, 2, 3, 4).reshape(CB, T * LANES, 2 * P)
    m_c = jnp.stack([ca_r[1:], -ca_i[1:]])
    m_c = m_c.reshape(2, T, CB, GB, Hc, P).transpose(2, 0, 3, 5, 1, 4).reshape(
        CB, 2 * GB * P, T * Hc)
    a_chunk = jnp.stack([pw_r[T].reshape(1, G * P), pw_i[T].reshape(1, G * P)])
    return k_c.astype(BF16), n_c.astype(BF16), m_c.astype(BF16), a_chunk


def _div(x, d):
    return x >> (d.bit_length() - 1)


def _mod(x, d):
    return x & (d - 1)


def _spread_block_diag(compact, src_of_col, row_group, col_group):
    n_rows, n_src = compact.shape
    n_cols = n_rows
    src = lax.broadcasted_iota(jnp.int32, (n_src, n_cols), 0)
    col = lax.broadcasted_iota(jnp.int32, (n_src, n_cols), 1)
    spread = jnp.where(src == src_of_col(col), 1.0, 0.0).astype(BF16)
    full = jnp.dot(compact.astype(BF16), spread, preferred_element_type=F32)
    r = lax.broadcasted_iota(jnp.int32, full.shape, 0)
    c = lax.broadcasted_iota(jnp.int32, full.shape, 1)
    return jnp.where(row_group(r) == col_group(c), full, 0.0).astype(BF16)


def _chunk_rows(u_ref):
    n = u_ref.shape[0] // SSM_CHUNK
    return jnp.concatenate([u_ref[pl.ds(s, n, stride=SSM_CHUNK), :] for s in range(SSM_CHUNK)],
                           axis=1).astype(BF16)


def _ssm_kernel(*refs, tiles_per_seq):
    T, GB, P, Hc = SSM_CHUNK, SSM_LANE_GROUPS, SSM_STATE, SSM_GROUP
    nb = SSM_WIDTH // LANES
    u_refs, (k_c_ref, n_c_ref, m_c_ref, a_ref) = refs[:nb], refs[nb:nb + 4]
    y_refs = refs[nb + 4:2 * nb + 4]
    n_s, w_s, m_s, er_s, ei_s, hr_s, hi_s, cr_s, ci_s = refs[2 * nb + 4:]
    half = GB * P
    step = pl.program_id(0)

    @pl.when(step == 0)
    def _():
        r = lax.broadcasted_iota(jnp.int32, (LANES, LANES), 0)
        c = lax.broadcasted_iota(jnp.int32, (LANES, LANES), 1)
        same_group = _div(r, Hc) == _div(c, Hc)
        for cb in range(nb):
            n_s[cb] = _spread_block_diag(
                n_c_ref[cb], lambda c: _div(c, GB * P) * P + _mod(c, P),
                lambda r: _mod(_div(r, Hc), GB), lambda c: _mod(_div(c, P), GB))
            w_s[cb] = jnp.zeros(w_s.shape[1:], BF16)
            for tau in range(T):
                blk = jnp.where(same_group, k_c_ref[cb, tau], 0.0).astype(BF16)
                for s in range(T - tau):
                    w_s[cb, s * LANES:(s + 1) * LANES,
                        (s + tau) * LANES:(s + tau + 1) * LANES] = blk
            m_s[cb] = _spread_block_diag(
                m_c_ref[cb], lambda c: _div(c, LANES) * Hc + _mod(c, Hc),
                lambda r: _mod(_div(r, P), GB), lambda c: _mod(_div(c, Hc), GB))

    @pl.when(step % tiles_per_seq == 0)
    def _():
        cr_s[...] = jnp.zeros_like(cr_s)
        ci_s[...] = jnp.zeros_like(ci_s)

    u_rows = [_chunk_rows(u_ref) for u_ref in u_refs]
    for cb in range(nb):
        e = jnp.dot(u_rows[cb], n_s[cb], preferred_element_type=F32)
        er_s[:, cb * half:(cb + 1) * half] = e[:, :half]
        ei_s[:, cb * half:(cb + 1) * half] = e[:, half:]

    n = er_s.shape[0]
    for cb in range(nb):
        for lo in range(0, T * LANES, MXU_DEPTH):
            hi = lo + MXU_DEPTH
            y = jnp.dot(u_rows[cb][:, :hi], w_s[cb, :hi, lo:hi], preferred_element_type=F32)
            for t in range(lo // LANES, hi // LANES):
                y_refs[cb][pl.ds(t, n, stride=T), :] = y[:, t * LANES - lo:(t + 1) * LANES - lo]

    ar = a_ref[0]
    ai = a_ref[1]
    hr, hi = cr_s[...], ci_s[...]
    for c in range(n):
        hr_s[c:c + 1, :] = hr
        hi_s[c:c + 1, :] = hi
        hr, hi = (ar * hr - ai * hi + er_s[c:c + 1, :], ar * hi + ai * hr + ei_s[c:c + 1, :])
    cr_s[...] = hr
    ci_s[...] = hi

    for cb in range(nb):
        cols = slice(cb * half, (cb + 1) * half)
        h = jnp.concatenate([hr_s[:, cols], hi_s[:, cols]], axis=1).astype(BF16)
        y = jnp.dot(h, m_s[cb], preferred_element_type=F32)
        for t in range(T):
            rows_t = pl.ds(t, n, stride=T)
            y_refs[cb][rows_t, :] = y_refs[cb][rows_t, :] + y[:, t * LANES:(t + 1) * LANES]


def _s5_ssm(u, n_batch, ops):
    k_c, n_c, m_c, a_chunk = ops
    tokens = u.shape[0]
    T = SSM_CHUNK
    nb = SSM_WIDTH // LANES
    rt = SSM_ROW_TILE
    ct = rt // T
    op_dim = T * LANES
    n_state = SSM_GROUPS * SSM_STATE
    assert (tokens // n_batch) % rt == 0
    u_specs = [pl.BlockSpec((rt, LANES), functools.partial(lambda cb, r: (r, cb), cb))
               for cb in range(nb)]
    y_spec = pl.BlockSpec((rt, LANES), lambda r: (r, 0))
    return pl.pallas_call(
        functools.partial(_ssm_kernel, tiles_per_seq=tokens // n_batch // rt),
        grid=(tokens // rt,),
        in_specs=u_specs + [_const_spec(k_c.shape), _const_spec(n_c.shape),
                            _const_spec(m_c.shape), _const_spec(a_chunk.shape)],
        out_specs=[y_spec] * nb,
        out_shape=[jax.ShapeDtypeStruct((tokens, LANES), F32)] * nb,
        scratch_shapes=[pltpu.VMEM((nb, op_dim, op_dim), BF16)] * 3
        + [pltpu.VMEM((ct, n_state), F32)] * 4 + [pltpu.VMEM((1, n_state), F32)] * 2,
        compiler_params=pltpu.CompilerParams(dimension_semantics=("arbitrary",),
                                             vmem_limit_bytes=VMEM_LIMIT_BYTES),
        name="ssm",
    )(*([u] * nb), k_c, n_c, m_c, a_chunk)


def _post_kernel(x1_ref, attn_ref, *refs):
    n_y = SSM_WIDTH // LANES
    y_refs = refs[:n_y]
    (ga_ref, gs_ref, wup_ref, wglu_ref, wout_ref, n3_ref, wg_ref, wu_ref, wd_ref, nf_ref,
     o_ref) = refs[n_y:]
    groups = _row_groups(x1_ref.shape[0])
    dot = functools.partial(jnp.dot, preferred_element_type=F32)
    attn = [dot(attn_ref[r, :], wup_ref[...]) for r in groups]
    glu = []
    for r in groups:
        y = jnp.concatenate([y_ref[r, :] for y_ref in y_refs], axis=1)
        glu.append(dot(jax.nn.gelu(y, approximate=True).astype(BF16), wglu_ref[...]))
    x2 = []
    for g, r in enumerate(groups):
        ssm_out = glu[g][:, :D_MODEL] * jax.nn.sigmoid(glu[g][:, D_MODEL:])
        merged = (jax.nn.sigmoid(ga_ref[r, :]) * attn[g]
                  + jax.nn.sigmoid(gs_ref[r, :]) * ssm_out)
        x2.append(x1_ref[r, :] + dot(merged.astype(BF16), wout_ref[...]))
    x3 = _swiglu_half_step(x2, n3_ref[...], wg_ref, wu_ref, wd_ref)
    for r, x3_g in zip(groups, x3):
        o_ref[r, :] = _rms(x3_g, nf_ref[...])


def _post(x1, attn, ys, ga, gs, wup, wglu, wout, n3, wg, wu, wd, nf):
    tokens = x1.shape[0]
    tm = TOKEN_TILE
    row = lambda w: pl.BlockSpec((tm, w), lambda i: (i, 0))
    consts = (wup, wglu, wout, n3, wg, wu, wd, nf)
    return pl.pallas_call(
        _post_kernel,
        grid=(tokens // tm,),
        in_specs=[row(D_MODEL), row(ATTN_WIDTH)] + [row(LANES)] * len(ys)
        + [row(D_MODEL), row(D_MODEL)] + [_const_spec(c.shape) for c in consts],
        out_specs=row(D_MODEL),
        out_shape=jax.ShapeDtypeStruct((tokens, D_MODEL), F32),
        compiler_params=pltpu.CompilerParams(
            dimension_semantics=("arbitrary",), vmem_limit_bytes=VMEM_LIMIT_BYTES),
        name="post",
    )(x1, attn, *ys, ga, gs, *consts)


def kernel(x, ffn1_norm, ffn1_w_gate, ffn1_w_up, ffn1_w_down, mix_norm, w_in, w_attn_up, ssm_lambda_re, ssm_lambda_im, ssm_log_dt, ssm_b_re, ssm_b_im, ssm_c_re, ssm_c_im, ssm_d, w_ssm_glu, w_out, ffn2_norm, ffn2_w_gate, ffn2_w_up, ffn2_w_down, final_norm):
    B, S, D = x.shape
    depth = ffn1_norm.shape[0]
    assert depth == 1, "final norm is fused into the single layer's last stage"
    tokens = B * S
    slopes = jnp.asarray(2.0 ** (-8.0 * np.arange(1, N_HEADS + 1) / N_HEADS), dtype=F32)
    bf = lambda w: w.astype(BF16)
    xt = x.reshape(tokens, D)
    for l in range(depth):
        x1, q, k, vt, u, ga, gs, kmean = _ffn1_proj(
            xt, ffn1_norm[l][None], bf(ffn1_w_gate[l]), bf(ffn1_w_up[l]), bf(ffn1_w_down[l]),
            mix_norm[l][None], bf(w_in[l]))
        attn = _moba_attention(q, k, vt, kmean.reshape(-1, ATTN_WIDTH), slopes, B)
        mats = _ssm_matrices(ssm_lambda_re[l], ssm_lambda_im[l], ssm_log_dt[l], ssm_b_re[l],
                             ssm_b_im[l], ssm_c_re[l], ssm_c_im[l], ssm_d[l])
        ys = _s5_ssm(u, B, mats)
        xt = _post(x1, attn, ys, ga, gs, bf(w_attn_up[l]), bf(w_ssm_glu[l]), bf(w_out[l]),
                   ffn2_norm[l][None], bf(ffn2_w_gate[l]), bf(ffn2_w_up[l]), bf(ffn2_w_down[l]),
                   final_norm[None])
    return xt.reshape(B, S, D)
```

```python
import functools
import math

import numpy as np
import jax
import jax.numpy as jnp
from jax import lax
from jax.experimental import pallas as pl
from jax.experimental.pallas import tpu as pltpu

F32 = jnp.float32
BF16 = jnp.bfloat16

D_MODEL = 1024
N_HEADS = 8
HEAD_DIM = 64
ATTN_WIDTH = N_HEADS * HEAD_DIM
MOBA_BLOCK = 256
MOBA_TOPK = 3
SSM_GROUP = 16
SSM_GROUPS = 32
SSM_WIDTH = SSM_GROUP * SSM_GROUPS
SSM_STATE = 64
EPS = 1e-6
LOG2_E = math.log2(math.e)

VMEM_LIMIT_BYTES = 56 * 1024 * 1024
TOKEN_TILE = MOBA_BLOCK
POST_TILE = 512
ROW_SPLITS = 2
HEADS_PER_STEP = 8
SOFTMAX_ROWS = 32
MXU_DEPTH = 256
BF16_ROWS = 16
V_ROWS = HEAD_DIM + BF16_ROWS
MASKED = -1e30
LANES = 128
SSM_CHUNK = 8
SSM_LANE_GROUPS = LANES // SSM_GROUP
SSM_ROW_TILE = 2048


def _const_spec(shape):
    nd = len(shape)
    return pl.BlockSpec(shape, lambda *_: (0,) * nd, pipeline_mode=pl.Buffered(1))


def _rms(x, g):
    return x * lax.rsqrt(jnp.mean(x * x, axis=-1, keepdims=True) + EPS) * g


def _row_groups(n_rows):
    rows = n_rows // ROW_SPLITS
    return [slice(g * rows, (g + 1) * rows) for g in range(ROW_SPLITS)]


def _swiglu_half_step(xs, g_norm, wg_ref, wu_ref, wd_ref):
    gate, up = [], []
    for x in xs:
        h = _rms(x, g_norm).astype(BF16)
        gate.append(jnp.dot(h, wg_ref[...], preferred_element_type=F32))
        up.append(jnp.dot(h, wu_ref[...], preferred_element_type=F32))
    out = []
    for x, g, u in zip(xs, gate, up):
        act = (g * jax.nn.sigmoid(g) * u).astype(BF16)
        out.append(x + 0.5 * jnp.dot(act, wd_ref[...], preferred_element_type=F32))
    return out


def _ffn1_proj_kernel(x_ref, n1_ref, wg_ref, wu_ref, wd_ref, n2_ref, win_ref,
                      x1_ref, q_ref, k_ref, vt_ref, u_ref, ga_ref, gs_ref, kmean_ref, v_s):
    groups = _row_groups(x_ref.shape[0])
    x1 = _swiglu_half_step([x_ref[r, :] for r in groups], n1_ref[...], wg_ref, wu_ref, wd_ref)
    hs = []
    for r, x1_g in zip(groups, x1):
        x1_ref[r, :] = x1_g
        hs.append(_rms(x1_g, n2_ref[...]).astype(BF16))
    k_sum = None
    col = 0
    for ref in (q_ref, k_ref, v_s, u_ref, ga_ref, gs_ref):
        width = ref.shape[-1]
        for r, h in zip(groups, hs):
            y = jnp.dot(h, win_ref[:, col:col + width], preferred_element_type=F32)
            ref[r, :] = y.astype(ref.dtype)
            if ref is k_ref:
                part = jnp.sum(y, axis=0, keepdims=True)
                k_sum = part if k_sum is None else k_sum + part
        col += width
    kmean_ref[...] = k_sum / x_ref.shape[0]
    vt = v_s[...].T.astype(BF16)
    ones_row = jnp.where(lax.broadcasted_iota(jnp.int32, (BF16_ROWS, vt.shape[1]), 0) == 0,
                         1.0, 0.0).astype(BF16)
    for hh in range(N_HEADS):
        vt_ref[hh * V_ROWS:hh * V_ROWS + HEAD_DIM, :] = vt[hh * HEAD_DIM:(hh + 1) * HEAD_DIM]
        vt_ref[hh * V_ROWS + HEAD_DIM:(hh + 1) * V_ROWS, :] = ones_row


def _ffn1_proj(x, n1, wg, wu, wd, n2, win):
    tokens = x.shape[0]
    tm = TOKEN_TILE
    assert tm == MOBA_BLOCK
    row = lambda w: pl.BlockSpec((tm, w), lambda i: (i, 0))
    per_block = lambda r, c: pl.BlockSpec((None, r, c), lambda i: (i, 0, 0))
    rows = lambda w, dt: jax.ShapeDtypeStruct((tokens, w), dt)
    blocks = lambda r, c, dt: jax.ShapeDtypeStruct((tokens // tm, r, c), dt)
    return pl.pallas_call(
        _ffn1_proj_kernel,
        grid=(tokens // tm,),
        in_specs=[row(D_MODEL), _const_spec(n1.shape), _const_spec(wg.shape),
                  _const_spec(wu.shape), _const_spec(wd.shape), _const_spec(n2.shape),
                  _const_spec(win.shape)],
        out_specs=[row(D_MODEL), row(ATTN_WIDTH), row(ATTN_WIDTH),
                   per_block(N_HEADS * V_ROWS, tm), row(SSM_WIDTH), row(D_MODEL), row(D_MODEL),
                   per_block(1, ATTN_WIDTH)],
        out_shape=[rows(D_MODEL, F32), rows(ATTN_WIDTH, F32), rows(ATTN_WIDTH, BF16),
                   blocks(N_HEADS * V_ROWS, tm, BF16), rows(SSM_WIDTH, F32), rows(D_MODEL, F32),
                   rows(D_MODEL, F32), blocks(1, ATTN_WIDTH, F32)],
        scratch_shapes=[pltpu.VMEM((tm, ATTN_WIDTH), F32)],
        compiler_params=pltpu.CompilerParams(
            dimension_semantics=("arbitrary",), vmem_limit_bytes=VMEM_LIMIT_BYTES),
        name="ffn1_proj",
    )(x, n1, wg, wu, wd, n2, win)


def _split3(x):
    hi = x.astype(BF16).astype(F32)
    mid = (x - hi).astype(BF16).astype(F32)
    return hi, mid, x - hi - mid


def _moba_kernel(slopes_ref, q_ref, k_ref, vt_ref, kmean_ref, o_ref, kms_ref, kaug_ref, qa_ref,
                 t_a, t_b, p_a, p_b, acc_ref):
    hg = pl.program_id(1)
    i = pl.program_id(2)
    blk = MOBA_BLOCK
    n_blocks = k_ref.shape[0]
    n_heads = HEADS_PER_STEP
    pair_w = MXU_DEPTH // 2
    pos_col = n_blocks
    neg_inf = -jnp.inf
    slopes = [slopes_ref[hg * n_heads + hh] * LOG2_E for hh in range(n_heads)]
    v_rows = lambda hh: slice(hh * V_ROWS, (hh + 1) * V_ROWS)
    aug_id = lax.broadcasted_iota(jnp.int32, (BF16_ROWS, blk), 0)
    pair_row_head = lax.broadcasted_iota(jnp.int32, (pair_w, blk), 0) // HEAD_DIM
    aug_col = lax.broadcasted_iota(jnp.int32, kaug_ref.shape[1:], 1)

    @pl.when(i == 0)
    def _():
        kmean = kmean_ref[...]
        lane_head = lax.broadcasted_iota(jnp.int32, kmean.shape, 1) // HEAD_DIM
        km = jnp.concatenate([jnp.where(lane_head == hh, kmean, 0.0) for hh in range(n_heads)],
                             axis=0)
        km_hi = km.astype(BF16)
        kms_ref[...] = jnp.concatenate([km_hi, (km - km_hi.astype(F32)).astype(BF16)], axis=0)
        key = lax.broadcasted_iota(jnp.int32, kaug_ref.shape[1:], 0).astype(F32)
        in_block = jnp.where((aug_col >= pos_col) & (aug_col < pos_col + 3), key, 0.0)
        is_offset = (aug_col >= pos_col + 3) & (aug_col < pos_col + 6)
        for j in range(n_blocks):
            kaug_ref[j] = jnp.where(aug_col == j, 1.0,
                                    jnp.where(is_offset, float(j * blk), in_block)).astype(BF16)

    qt = q_ref[...].T

    for hh in range(n_heads):
        pair = hh * HEAD_DIM // pair_w
        in_pair = slice(pair * pair_w, (pair + 1) * pair_w)
        qa_ref[hh, :pair_w, :] = (jnp.where(pair_row_head == hh - pair * (pair_w // HEAD_DIM),
                                            qt[in_pair], 0.0)
                                  * (HEAD_DIM ** -0.5 * LOG2_E)).astype(BF16)
        qa_ref[hh, pair_w:pair_w + n_blocks, :] = jnp.zeros((n_blocks, blk), BF16)
        pieces = _split3(jnp.full((1, blk), slopes[hh], F32)) * 2
        tile = jnp.zeros((BF16_ROWS, blk), F32)
        for n, piece in enumerate(pieces):
            tile = jnp.where(aug_id == n, piece, tile)
        qa_ref[hh, pair_w + pos_col:pair_w + pos_col + BF16_ROWS, :] = tile.astype(BF16)
        qa_ref[hh, pair_w + pos_col + BF16_ROWS:, :] = jnp.zeros(
            (MXU_DEPTH - pair_w - pos_col - BF16_ROWS, blk), BF16)

    def block_gates():
        qt_hi = qt.astype(BF16)
        qt_lo = (qt - qt_hi.astype(F32)).astype(BF16)
        gate_hi = jnp.dot(kms_ref[...], qt_hi, preferred_element_type=F32)
        return (gate_hi[:n_heads * n_blocks] + gate_hi[n_heads * n_blocks:]
                + jnp.dot(kms_ref[:n_heads * n_blocks, :], qt_lo, preferred_element_type=F32))

    def select_blocks(gates):
        shape = (n_heads, n_blocks, blk)
        blk_id = lax.broadcasted_iota(jnp.int32, shape, 1).astype(F32)
        g = jnp.where(blk_id < i.astype(F32), gates.reshape(shape), neg_inf)
        sel = jnp.zeros(shape, jnp.bool_)
        for _ in range(MOBA_TOPK):
            top = jnp.max(g, axis=1, keepdims=True)
            first = jnp.min(jnp.where(g == top, blk_id, float(n_blocks)), axis=1, keepdims=True)
            pick = (blk_id == first) & (top > neg_inf)
            sel = sel | pick
            g = jnp.where(pick, neg_inf, g)
        mask_rows = jnp.where(sel, 0.0, MASKED).astype(BF16)
        for hh in range(n_heads):
            qa_ref[hh, pair_w:pair_w + n_blocks, :] = mask_rows[hh]

    last_block = n_blocks - 1
    chunks = [slice(c, c + SOFTMAX_ROWS) for c in range(0, blk, SOFTMAX_ROWS)]

    def qk_into(t_ref, j, own):
        kj = k_ref[j]
        extras = kaug_ref[j]
        col_max = []
        for hh in range(n_heads):
            pair = hh * HEAD_DIM // pair_w
            keys = jnp.concatenate([kj[:, pair * pair_w:(pair + 1) * pair_w], extras], axis=1)
            t = jnp.dot(keys, qa_ref[hh], preferred_element_type=F32)
            t_ref[hh] = t
            col_max.append(None if own else jnp.max(t, axis=0, keepdims=True))
        return tuple(col_max)

    def pv(p_ref, j, alpha):
        vj = vt_ref[j]
        for hh in range(n_heads):
            acc_ref[hh] = alpha[hh] * acc_ref[hh] + jnp.dot(vj[v_rows(hh)], p_ref[hh],
                                                            preferred_element_type=F32)

    def softmax_into(t_ref, p_ref, hh, m, cmax):
        causal = cmax is None

        def logits(c):
            t = t_ref[hh, c, :]
            if not causal:
                return t
            key = lax.broadcasted_iota(jnp.int32, t.shape, 0) + c.start
            query = lax.broadcasted_iota(jnp.int32, t.shape, 1)
            return jnp.where(key <= query, t, neg_inf)

        if causal:
            cmax = logits(chunks[0])
            for c in chunks[1:]:
                cmax = jnp.maximum(cmax, logits(c))
            cmax = jnp.max(cmax, axis=0, keepdims=True)
        m_new = cmax if m is None else jnp.maximum(m, cmax)
        for c in chunks:
            p_ref[hh, c, :] = jnp.exp2(logits(c) - m_new).astype(BF16)
        alpha = jnp.ones_like(m_new) if m is None else jnp.exp2(m - m_new)
        return alpha, m_new

    gates = block_gates()
    qk_into(t_b, i, True)
    select_blocks(gates)
    cmax_first = qk_into(t_a, 0, False)
    own = [softmax_into(t_b, p_b, hh, None, None) for hh in range(n_heads)]
    alpha0, m0 = (tuple(x) for x in zip(*own))
    acc_ref[...] = jnp.zeros_like(acc_ref)

    def body(n, carry):
        prev_j, a_prev, m, cmax_a = carry
        j0 = 2 * n
        j1 = jnp.minimum(j0 + 1, last_block)
        cmax_b = qk_into(t_b, j1, False)
        pv(p_b, prev_j, a_prev)
        st0 = [softmax_into(t_a, p_a, hh, m[hh], cmax_a[hh]) for hh in range(n_heads)]
        a0, m = (tuple(x) for x in zip(*st0))
        cmax_a = qk_into(t_a, jnp.minimum(j0 + 2, last_block), False)
        pv(p_a, j0, a0)
        st1 = [softmax_into(t_b, p_b, hh, m[hh], cmax_b[hh]) for hh in range(n_heads)]
        a1, m = (tuple(x) for x in zip(*st1))
        return j1, a1, m, cmax_a

    def double_body(n, carry):
        return body(2 * n + 1, body(2 * n, carry))

    n_pairs = (i + 1) // 2
    carry = lax.fori_loop(0, n_pairs // 2, double_body, (i, alpha0, m0, cmax_first))
    prev_j, a_prev, _, _ = lax.fori_loop(n_pairs // 2 * 2, n_pairs, body, carry)
    pv(p_b, prev_j, a_prev)
    out_t = jnp.concatenate([acc_ref[hh, :HEAD_DIM, :] / acc_ref[hh, HEAD_DIM:HEAD_DIM + 1, :]
                             for hh in range(n_heads)], axis=0)
    o_ref[...] = out_t.T.astype(o_ref.dtype)


def _moba_attention(q, k, vt, kmean, slopes, n_batch):
    tokens, width = q.shape
    blk = MOBA_BLOCK
    n_blocks = tokens // n_batch // blk
    step_width = HEADS_PER_STEP * HEAD_DIM
    q_spec = pl.BlockSpec((blk, step_width), lambda b, g, i: (b * n_blocks + i, g))
    return pl.pallas_call(
        _moba_kernel,
        grid=(n_batch, width // step_width, n_blocks),
        in_specs=[pl.BlockSpec(memory_space=pltpu.SMEM), q_spec,
                  pl.BlockSpec((n_blocks, blk, step_width), lambda b, g, i: (b, 0, g)),
                  pl.BlockSpec((n_blocks, HEADS_PER_STEP * V_ROWS, blk),
                               lambda b, g, i: (b, g, 0)),
                  pl.BlockSpec((n_blocks, step_width), lambda b, g, i: (b, g))],
        out_specs=q_spec,
        out_shape=jax.ShapeDtypeStruct((tokens, width), BF16),
        scratch_shapes=[pltpu.VMEM((2 * HEADS_PER_STEP * n_blocks, step_width), BF16),
                        pltpu.VMEM((n_blocks, blk, MXU_DEPTH // 2), BF16),
                        pltpu.VMEM((HEADS_PER_STEP, MXU_DEPTH, blk), BF16),
                        pltpu.VMEM((HEADS_PER_STEP, blk, blk), F32),
                        pltpu.VMEM((HEADS_PER_STEP, blk, blk), F32),
                        pltpu.VMEM((HEADS_PER_STEP, blk, blk), BF16),
                        pltpu.VMEM((HEADS_PER_STEP, blk, blk), BF16),
                        pltpu.VMEM((HEADS_PER_STEP, V_ROWS, blk), F32)],
        compiler_params=pltpu.CompilerParams(
            dimension_semantics=("arbitrary", "arbitrary", "arbitrary"),
            vmem_limit_bytes=VMEM_LIMIT_BYTES),
        name="moba_attn",
    )(slopes, q, k.reshape(tokens // blk, blk, width), vt, kmean)


def _ssm_matrices(lam_re, lam_im, log_dt, b_re, b_im, c_re, c_im, d_skip):
    T, G, P, Hc = SSM_CHUNK, SSM_GROUPS, SSM_STATE, SSM_GROUP
    dt = jnp.exp(log_dt)[:, None]
    mag = jnp.exp(lam_re * dt)
    ar = mag * jnp.cos(lam_im * dt)
    ai = mag * jnp.sin(lam_im * dt)
    nr, ni = ar - 1.0, ai
    den = lam_re * lam_re + lam_im * lam_im
    fr = (nr * lam_re + ni * lam_im) / den
    fi = (ni * lam_re - nr * lam_im) / den
    bbr = fr[..., None] * b_re - fi[..., None] * b_im
    bbi = fr[..., None] * b_im + fi[..., None] * b_re
    pr, pi = [jnp.ones_like(ar)], [jnp.zeros_like(ar)]
    for _ in range(T):
        pr, pi = pr + [pr[-1] * ar - pi[-1] * ai], pi + [pr[-1] * ai + pi[-1] * ar]
    pw_r, pw_i = jnp.stack(pr), jnp.stack(pi)

    ca_r = c_re[None] * pw_r[:, :, None, :] - c_im[None] * pw_i[:, :, None, :]
    ca_i = c_re[None] * pw_i[:, :, None, :] + c_im[None] * pw_r[:, :, None, :]
    kern = jnp.sum(ca_r[:T, :, :, :, None] * bbr[None, :, None, :, :]
                   - ca_i[:T, :, :, :, None] * bbi[None, :, None, :, :], axis=3)
    skip = d_skip.reshape(G, Hc)[:, :, None] * jnp.eye(Hc, dtype=F32)[None]
    kern = kern.at[0].add(skip)

    GB = SSM_LANE_GROUPS
    CB = G // GB
    k_c = kern.transpose(1, 0, 3, 2).reshape(CB, GB, T, Hc, Hc).transpose(0, 2, 1, 3, 4)
    k_c = jnp.tile(k_c.reshape(CB, T, GB * Hc, Hc), (1, 1, 1, GB))
    rev_r, rev_i = pw_r[T - 1::-1], pw_i[T - 1::-1]
    n_r = rev_r[..., None] * bbr[None] - rev_i[..., None] * bbi[None]
    n_i = rev_r[..., None] * bbi[None] + rev_i[..., None] * bbr[None]
    n_c = jnp.concatenate([n_r, n_i], axis=2).transpose(0, 1, 3, 2)
    n_c = n_c.reshape(T, CB, GB, Hc, 2 * P).transpose(1, 0, 2, 3, 4).reshape(CB, T * LANES, 2 * P)
    m_c = jnp.stack([ca_r[1:], -ca_i[1:]])
    m_c = m_c.reshape(2, T, CB, GB, Hc, P).transpose(2, 0, 3, 5, 1, 4).reshape(
        CB, 2 * GB * P, T * Hc)
    a_chunk = jnp.stack([pw_r[T].reshape(1, G * P), pw_i[T].reshape(1, G * P)])
    return k_c.astype(BF16), n_c.astype(BF16), m_c.astype(BF16), a_chunk


def _div(x, d):
    return x >> (d.bit_length() - 1)


def _mod(x, d):
    return x & (d - 1)


def _spread_block_diag(compact, src_of_col, row_group, col_group):
    n_rows, n_src = compact.shape
    n_cols = n_rows
    src = lax.broadcasted_iota(jnp.int32, (n_src, n_cols), 0)
    col = lax.broadcasted_iota(jnp.int32, (n_src, n_cols), 1)
    spread = jnp.where(src == src_of_col(col), 1.0, 0.0).astype(BF16)
    full = jnp.dot(compact.astype(BF16), spread, preferred_element_type=F32)
    r = lax.broadcasted_iota(jnp.int32, full.shape, 0)
    c = lax.broadcasted_iota(jnp.int32, full.shape, 1)
    return jnp.where(row_group(r) == col_group(c), full, 0.0).astype(BF16)


def _chunk_rows(u_ref):
    n = u_ref.shape[0] // SSM_CHUNK
    return jnp.concatenate([u_ref[pl.ds(s, n, stride=SSM_CHUNK), :] for s in range(SSM_CHUNK)],
                           axis=1).astype(BF16)


def _ssm_kernel(*refs, tiles_per_seq):
    T, GB, P, Hc = SSM_CHUNK, SSM_LANE_GROUPS, SSM_STATE, SSM_GROUP
    nb = SSM_WIDTH // LANES
    u_refs, (k_c_ref, n_c_ref, m_c_ref, a_ref) = refs[:nb], refs[nb:nb + 4]
    y_refs = refs[nb + 4:2 * nb + 4]
    n_s, w_s, m_s, er_s, ei_s, hr_s, hi_s, cr_s, ci_s = refs[2 * nb + 4:]
    half = GB * P
    step = pl.program_id(0)

    @pl.when(step == 0)
    def _():
        r = lax.broadcasted_iota(jnp.int32, (LANES, LANES), 0)
        c = lax.broadcasted_iota(jnp.int32, (LANES, LANES), 1)
        same_group = _div(r, Hc) == _div(c, Hc)
        for cb in range(nb):
            n_s[cb] = _spread_block_diag(
                n_c_ref[cb], lambda c: _div(c, GB * P) * P + _mod(c, P),
                lambda r: _mod(_div(r, Hc), GB), lambda c: _mod(_div(c, P), GB))
            w_s[cb] = jnp.zeros(w_s.shape[1:], BF16)
            for tau in range(T):
                blk = jnp.where(same_group, k_c_ref[cb, tau], 0.0).astype(BF16)
                for s in range(T - tau):
                    w_s[cb, s * LANES:(s + 1) * LANES,
                        (s + tau) * LANES:(s + tau + 1) * LANES] = blk
            m_s[cb] = _spread_block_diag(
                m_c_ref[cb], lambda c: _div(c, LANES) * Hc + _mod(c, Hc),
                lambda r: _mod(_div(r, P), GB), lambda c: _mod(_div(c, Hc), GB))

    @pl.when(step % tiles_per_seq == 0)
    def _():
        cr_s[...] = jnp.zeros_like(cr_s)
        ci_s[...] = jnp.zeros_like(ci_s)

    u_rows = [_chunk_rows(u_ref) for u_ref in u_refs]
    for cb in range(nb):
        e = jnp.dot(u_rows[cb], n_s[cb], preferred_element_type=F32)
        er_s[:, cb * half:(cb + 1) * half] = e[:, :half]
        ei_s[:, cb * half:(cb + 1) * half] = e[:, half:]

    n = er_s.shape[0]
    for cb in range(nb):
        for lo in range(0, T * LANES, MXU_DEPTH):
            hi = lo + MXU_DEPTH
            y = jnp.dot(u_rows[cb][:, :hi], w_s[cb, :hi, lo:hi], preferred_element_type=F32)
            for t in range(lo // LANES, hi // LANES):
                y_refs[cb][pl.ds(t, n, stride=T), :] = y[:, t * LANES - lo:(t + 1) * LANES - lo]

    ar = a_ref[0]
    ai = a_ref[1]
    hr, hi = cr_s[...], ci_s[...]
    for c in range(n):
        hr_s[c:c + 1, :] = hr
        hi_s[c:c + 1, :] = hi
        hr, hi = (ar * hr - ai * hi + er_s[c:c + 1, :], ar * hi + ai * hr + ei_s[c:c + 1, :])
    cr_s[...] = hr
    ci_s[...] = hi

    for cb in range(nb):
        cols = slice(cb * half, (cb + 1) * half)
        h = jnp.concatenate([hr_s[:, cols], hi_s[:, cols]], axis=1).astype(BF16)
        y = jnp.dot(h, m_s[cb], preferred_element_type=F32)
        for t in range(T):
            rows_t = pl.ds(t, n, stride=T)
            y_refs[cb][rows_t, :] = y_refs[cb][rows_t, :] + y[:, t * LANES:(t + 1) * LANES]


def _s5_ssm(u, n_batch, ops):
    k_c, n_c, m_c, a_chunk = ops
    tokens = u.shape[0]
    T = SSM_CHUNK
    nb = SSM_WIDTH // LANES
    rt = SSM_ROW_TILE
    ct = rt // T
    op_dim = T * LANES
    n_state = SSM_GROUPS * SSM_STATE
    assert (tokens // n_batch) % rt == 0
    u_specs = [pl.BlockSpec((rt, LANES), functools.partial(lambda cb, r: (r, cb), cb))
               for cb in range(nb)]
    y_spec = pl.BlockSpec((rt, LANES), lambda r: (r, 0))
    return pl.pallas_call(
        functools.partial(_ssm_kernel, tiles_per_seq=tokens // n_batch // rt),
        grid=(tokens // rt,),
        in_specs=u_specs + [_const_spec(k_c.shape), _const_spec(n_c.shape),
                            _const_spec(m_c.shape), _const_spec(a_chunk.shape)],
        out_specs=[y_spec] * nb,
        out_shape=[jax.ShapeDtypeStruct((tokens, LANES), F32)] * nb,
        scratch_shapes=[pltpu.VMEM((nb, op_dim, op_dim), BF16)] * 3
        + [pltpu.VMEM((ct, n_state), F32)] * 4 + [pltpu.VMEM((1, n_state), F32)] * 2,
        compiler_params=pltpu.CompilerParams(dimension_semantics=("arbitrary",),
                                             vmem_limit_bytes=VMEM_LIMIT_BYTES),
        name="ssm",
    )(*([u] * nb), k_c, n_c, m_c, a_chunk)


def _post_kernel(x1_ref, attn_ref, *refs):
    n_y = SSM_WIDTH // LANES
    y_refs = refs[:n_y]
    (ga_ref, gs_ref, wup_ref, wglu_ref, wout_ref, n3_ref, wg_ref, wu_ref, wd_ref, nf_ref,
     o_ref) = refs[n_y:]
    groups = _row_groups(x1_ref.shape[0])
    dot = functools.partial(jnp.dot, preferred_element_type=F32)
    attn = [dot(attn_ref[r, :], wup_ref[...]) for r in groups]
    glu = []
    for r in groups:
        y = jnp.concatenate([y_ref[r, :] for y_ref in y_refs], axis=1)
        glu.append(dot(jax.nn.gelu(y, approximate=True).astype(BF16), wglu_ref[...]))
    x2 = []
    for g, r in enumerate(groups):
        ssm_out = glu[g][:, :D_MODEL] * jax.nn.sigmoid(glu[g][:, D_MODEL:])
        merged = (jax.nn.sigmoid(ga_ref[r, :]) * attn[g]
                  + jax.nn.sigmoid(gs_ref[r, :]) * ssm_out)
        x2.append(x1_ref[r, :] + dot(merged.astype(BF16), wout_ref[...]))
    x3 = _swiglu_half_step(x2, n3_ref[...], wg_ref, wu_ref, wd_ref)
    for r, x3_g in zip(groups, x3):
        o_ref[r, :] = _rms(x3_g, nf_ref[...])


def _post(x1, attn, ys, ga, gs, wup, wglu, wout, n3, wg, wu, wd, nf):
    tokens = x1.shape[0]
    tm = POST_TILE
    row = lambda w: pl.BlockSpec((tm, w), lambda i: (i, 0))
    consts = (wup, wglu, wout, n3, wg, wu, wd, nf)
    return pl.pallas_call(
        _post_kernel,
        grid=(tokens // tm,),
        in_specs=[row(D_MODEL), row(ATTN_WIDTH)] + [row(LANES)] * len(ys)
        + [row(D_MODEL), row(D_MODEL)] + [_const_spec(c.shape) for c in consts],
        out_specs=row(D_MODEL),
        out_shape=jax.ShapeDtypeStruct((tokens, D_MODEL), F32),
        compiler_params=pltpu.CompilerParams(
            dimension_semantics=("arbitrary",), vmem_limit_bytes=VMEM_LIMIT_BYTES),
        name="post",
    )(x1, attn, *ys, ga, gs, *consts)


def kernel(x, ffn1_norm, ffn1_w_gate, ffn1_w_up, ffn1_w_down, mix_norm, w_in, w_attn_up, ssm_lambda_re, ssm_lambda_im, ssm_log_dt, ssm_b_re, ssm_b_im, ssm_c_re, ssm_c_im, ssm_d, w_ssm_glu, w_out, ffn2_norm, ffn2_w_gate, ffn2_w_up, ffn2_w_down, final_norm):
    B, S, D = x.shape
    depth = ffn1_norm.shape[0]
    assert depth == 1, "final norm is fused into the single layer's last stage"
    tokens = B * S
    slopes = jnp.asarray(2.0 ** (-8.0 * np.arange(1, N_HEADS + 1) / N_HEADS), dtype=F32)
    bf = lambda w: w.astype(BF16)
    xt = x.reshape(tokens, D)
    for l in range(depth):
        x1, q, k, vt, u, ga, gs, kmean = _ffn1_proj(
            xt, ffn1_norm[l][None], bf(ffn1_w_gate[l]), bf(ffn1_w_up[l]), bf(ffn1_w_down[l]),
            mix_norm[l][None], bf(w_in[l]))
        attn = _moba_attention(q, k, vt, kmean.reshape(-1, ATTN_WIDTH), slopes, B)
        mats = _ssm_matrices(ssm_lambda_re[l], ssm_lambda_im[l], ssm_log_dt[l], ssm_b_re[l],
                             ssm_b_im[l], ssm_c_re[l], ssm_c_im[l], ssm_d[l])
        ys = _s5_ssm(u, B, mats)
        xt = _post(x1, attn, ys, ga, gs, bf(w_attn_up[l]), bf(w_ssm_glu[l]), bf(w_out[l]),
                   ffn2_norm[l][None], bf(ffn2_w_gate[l]), bf(ffn2_w_up[l]), bf(ffn2_w_down[l]),
                   final_norm[None])
    return xt.reshape(B, S, D)
```
